```python
import jax
import jax.numpy as jnp
from jax import lax
import numpy as np

D_MODEL = 2048
BATCH = 2
SEQ = 4096
DEPTH = 2
DEC_BATCH = 8
DEC_SEQ = 4
PAST_LEN = 16384
PAGE_SIZE = 128

N_A_LAYERS = DEPTH // 2
N_B_LAYERS = DEPTH - N_A_LAYERS
CHUNK = 128
GMLP_WIDTH = D_MODEL
GMLP_GROUPS = 16
GMLP_GROUP_DIM = GMLP_WIDTH // GMLP_GROUPS
N_HEADS = 16
HEAD_DIM = D_MODEL // N_HEADS
SB_BLOCK = 128
SB_LOGIT_OFFSET = -8.0
PEER_HEADS = 8
PEER_NKEYS = 128
PEER_EXPERTS = PEER_NKEYS * PEER_NKEYS
PEER_KEY_DIM = 256
PEER_HALF = PEER_KEY_DIM // 2
PEER_TOPK = 16
PEER_TOKEN_BLOCK = 128
N_MOD = 6
EPS = 1e-6

kernel_name = 'yoco_gmlp_stickbreak_peer_step'


def rms_norm(x, g):
    xf = x.astype(jnp.float32)
    y = xf * lax.rsqrt(jnp.mean(xf * xf, axis=-1, keepdims=True) + EPS)
    return (y * g.astype(jnp.float32)).astype(x.dtype)


def modulate(h, shift, scale):
    return h * (1 + scale[:, None, :]) + shift[:, None, :]


def adaln(c, w, b, n):
    m = jax.nn.silu(c) @ w + b
    return jnp.split(m, n, axis=-1)


def gmlp_mixer(h, w_in, b_in, v_g, w_s, b_s, w_out):
    B, S, _ = h.shape
    z = jax.nn.gelu(h @ w_in + b_in, approximate=False)
    u, v = jnp.split(z, 2, axis=-1)
    v = rms_norm(v, v_g)
    pad = (-S) % CHUNK
    vp = jnp.pad(v, ((0, 0), (0, pad), (0, 0)))
    n_chunks = (S + pad) // CHUNK
    vc = vp.reshape(B, n_chunks, CHUNK, GMLP_GROUPS, GMLP_GROUP_DIM)
    causal = jnp.tril(jnp.ones((CHUNK, CHUNK), dtype=bool))
    w_mask = jnp.where(causal[None], w_s, jnp.zeros_like(w_s))
    mixed = jnp.einsum('gts,bnsgd->bntgd', w_mask, vc) + b_s.T[None, None, :, :, None]
    mixed = mixed.reshape(B, n_chunks * CHUNK, GMLP_WIDTH)[:, :S]
    return (u * mixed) @ w_out, v


def stick_breaking_block(q, k, v, q_pos, k_pos, logit_bias):
    z = jnp.einsum('bqhd,bkhd->bhqk', q, k, preferred_element_type=jnp.float32) * (HEAD_DIM ** -0.5)
    z = z + logit_bias.astype(jnp.float32)[None, :, None, None]
    mask = k_pos[None, :] < q_pos[:, None]
    log_1m = jnp.where(mask, jax.nn.log_sigmoid(-z), 0.0)
    log_surv = lax.cumsum(log_1m, axis=3, reverse=True) - log_1m
    a = jnp.where(mask, jnp.exp(jax.nn.log_sigmoid(z) + log_surv), 0.0)
    return jnp.einsum('bhqk,bkhd->bqhd', a.astype(v.dtype), v)


def stick_breaking(q, k, v, q_start, logit_bias):
    B, Tq, H, d = q.shape
    blk = min(SB_BLOCK, Tq)
    pad = (-Tq) % blk
    n_blk = (Tq + pad) // blk
    qp = jnp.pad(q, ((0, 0), (0, pad), (0, 0), (0, 0)))
    qb = qp.reshape(B, n_blk, blk, H, d).transpose(1, 0, 2, 3, 4)
    q_pos = (q_start + jnp.arange(n_blk * blk)).reshape(n_blk, blk)
    k_pos = jnp.arange(k.shape[1])
    out = lax.map(lambda a: stick_breaking_block(a[0], k, v, a[1], k_pos, logit_bias), (qb, q_pos))
    return out.transpose(1, 0, 2, 3, 4).reshape(B, n_blk * blk, H, d)[:, :Tq]


def shared_kv(x, c, mod_w, mod_b, norm_g, w_kv, k_norm_g):
    B, S, _ = x.shape
    shift, scale = adaln(c, mod_w, mod_b, 2)
    h = modulate(rms_norm(x, norm_g), shift, scale)
    kv = (h @ w_kv).reshape(B, S, 2, N_HEADS, HEAD_DIM)
    return rms_norm(kv[:, :, 0], k_norm_g), kv[:, :, 1]


def peer(h, w_q, subkeys, u_tab, v_tab):
    B, S, D = h.shape
    T = B * S
    xf = h.reshape(T, D)
    q = (xf @ w_q).reshape(T, PEER_HEADS, 2, PEER_HALF)
    s = jnp.einsum('thpc,pnc->thpn', q, subkeys, preferred_element_type=jnp.float32)
    s1, i1 = lax.top_k(s[:, :, 0], PEER_TOPK)
    s2, i2 = lax.top_k(s[:, :, 1], PEER_TOPK)
    n_cand = PEER_TOPK * PEER_TOPK
    cand = (s1[..., :, None] + s2[..., None, :]).reshape(T, PEER_HEADS, n_cand)
    cidx = (i1[..., :, None] * PEER_NKEYS + i2[..., None, :]).reshape(T, PEER_HEADS, n_cand)
    top_s, pos = lax.top_k(cand, PEER_TOPK)
    expert = jnp.take_along_axis(cidx, pos, axis=-1)
    gate = jax.nn.softmax(top_s, axis=-1).astype(h.dtype)
    blk = min(PEER_TOKEN_BLOCK, T)
    pad = (-T) % blk
    nb = (T + pad) // blk
    xb = jnp.pad(xf, ((0, pad), (0, 0))).reshape(nb, blk, D)
    eb = jnp.pad(expert, ((0, pad), (0, 0), (0, 0))).reshape(nb, blk, PEER_HEADS, PEER_TOPK)
    gb = jnp.pad(gate, ((0, pad), (0, 0), (0, 0))).reshape(nb, blk, PEER_HEADS, PEER_TOPK)

    def block(args):
        xt, et, gt = args
        act = jax.nn.gelu(jnp.einsum('td,thkd->thk', xt, u_tab[et]), approximate=False)
        return jnp.einsum('thk,thkd->td', gt * act, v_tab[et])

    out = lax.map(block, (xb, eb, gb)).reshape(nb * blk, D)[:T]
    return out.reshape(B, S, D)


def forward(x, c, past_k, past_v, p):
    B, S, _ = x.shape
    q_start = 0 if past_k is None else past_k.shape[1]
    gmlp_v = []
    k_new = v_new = k_all = v_all = None
    for layer in range(DEPTH):
        sh_m, sc_m, g_m, sh_f, sc_f, g_f = adaln(c, p['mod_w'][layer], p['mod_b'][layer], N_MOD)
        h = modulate(rms_norm(x, p['norm_mix_g'][layer]), sh_m, sc_m)
        if layer < N_A_LAYERS:
            mix, v_rows = gmlp_mixer(h, p['a_w_in'][layer], p['a_b_in'][layer], p['a_v_norm_g'][layer],
                                     p['a_w_s'][layer], p['a_b_s'][layer], p['a_w_out'][layer])
            gmlp_v.append(v_rows)
        else:
            i = layer - N_A_LAYERS
            q = rms_norm((h @ p['b_w_q'][i]).reshape(B, S, N_HEADS, HEAD_DIM), p['b_q_norm_g'][i])
            o = stick_breaking(q, k_all, v_all, q_start, p['b_logit_bias'][i])
            mix = o.reshape(B, S, N_HEADS * HEAD_DIM) @ p['b_w_o'][i]
        x = x + g_m[:, None, :] * mix
        h = modulate(rms_norm(x, p['norm_ffn_g'][layer]), sh_f, sc_f)
        x = x + g_f[:, None, :] * peer(h, p['peer_w_q'][layer], p['peer_subkeys'][layer],
                                       p['peer_u'][layer], p['peer_v'][layer])
        if layer == N_A_LAYERS - 1:
            k_new, v_new = shared_kv(x, c, p['kv_mod_w'], p['kv_mod_b'], p['kv_norm_g'],
                                     p['kv_w'], p['k_norm_g'])
            if past_k is None:
                k_all, v_all = k_new, v_new
            else:
                k_all = jnp.concatenate([past_k, k_new], axis=1)
                v_all = jnp.concatenate([past_v, v_new], axis=1)
    return x, k_new, v_new, jnp.stack(gmlp_v)


def setup_inputs(seed: int = 0):
    key = jax.random.key(seed)
    ks = iter(jax.random.split(key, 48))

    def nrm(shape, scale):
        return jax.random.normal(next(ks), shape, jnp.float32) * scale

    def gain(shape):
        return 1.0 + nrm(shape, 0.02)

    d = D_MODEL
    n_pages = PAST_LEN // PAGE_SIZE
    n_pool = (DEC_BATCH * n_pages * 5) // 4
    perm = jax.random.permutation(next(ks), n_pool)[:DEC_BATCH * n_pages]
    page_table = perm.reshape(DEC_BATCH, n_pages).astype(jnp.int32)
    return {
        'x_prompt': nrm((BATCH, SEQ, d), 1.0),
        'x_sample': nrm((DEC_BATCH, DEC_SEQ, d), 1.0),
        'cache_k': nrm((n_pool, PAGE_SIZE, N_HEADS, HEAD_DIM), 1.0),
        'cache_v': nrm((n_pool, PAGE_SIZE, N_HEADS, HEAD_DIM), 1.0),
        'page_table': page_table,
        'c_prompt': nrm((BATCH, d), 1.0),
        'c_sample': nrm((DEC_BATCH, d), 1.0),
        'mod_w': nrm((DEPTH, d, N_MOD * d), 0.3 * d ** -0.5),
        'mod_b': nrm((DEPTH, N_MOD * d), 0.01),
        'norm_mix_g': gain((DEPTH, d)),
        'norm_ffn_g': gain((DEPTH, d)),
        'a_w_in': nrm((N_A_LAYERS, d, 2 * GMLP_WIDTH), d ** -0.5),
        'a_b_in': nrm((N_A_LAYERS, 2 * GMLP_WIDTH), 0.01),
        'a_v_norm_g': gain((N_A_LAYERS, GMLP_WIDTH)),
        'a_w_s': nrm((N_A_LAYERS, GMLP_GROUPS, CHUNK, CHUNK), CHUNK ** -0.5),
        'a_b_s': gain((N_A_LAYERS, GMLP_GROUPS, CHUNK)),
        'a_w_out': nrm((N_A_LAYERS, GMLP_WIDTH, d), GMLP_WIDTH ** -0.5),
        'kv_mod_w': nrm((d, 2 * d), 0.3 * d ** -0.5),
        'kv_mod_b': nrm((2 * d,), 0.01),
        'kv_norm_g': gain((d,)),
        'kv_w': nrm((d, 2 * N_HEADS * HEAD_DIM), d ** -0.5),
        'k_norm_g': gain((HEAD_DIM,)),
        'b_w_q': nrm((N_B_LAYERS, d, N_HEADS * HEAD_DIM), d ** -0.5),
        'b_q_norm_g': gain((N_B_LAYERS, HEAD_DIM)),
        'b_logit_bias': SB_LOGIT_OFFSET + nrm((N_B_LAYERS, N_HEADS), 0.1),
        'b_w_o': nrm((N_B_LAYERS, N_HEADS * HEAD_DIM, d), (N_HEADS * HEAD_DIM) ** -0.5),
        'peer_w_q': nrm((DEPTH, d, PEER_HEADS * PEER_KEY_DIM), d ** -0.5),
        'peer_subkeys': nrm((DEPTH, 2, PEER_NKEYS, PEER_HALF), PEER_HALF ** -0.5),
        'peer_u': nrm((DEPTH, PEER_EXPERTS, d), d ** -0.5),
        'peer_v': nrm((DEPTH, PEER_EXPERTS, d), PEER_HEADS ** -0.5),
    }


def reference(x_prompt, x_sample, cache_k, cache_v, page_table, c_prompt, c_sample,
              mod_w, mod_b, norm_mix_g, norm_ffn_g,
              a_w_in, a_b_in, a_v_norm_g, a_w_s, a_b_s, a_w_out,
              kv_mod_w, kv_mod_b, kv_norm_g, kv_w, k_norm_g,
              b_w_q, b_q_norm_g, b_logit_bias, b_w_o,
              peer_w_q, peer_subkeys, peer_u, peer_v):
    p = {
        'mod_w': mod_w, 'mod_b': mod_b, 'norm_mix_g': norm_mix_g, 'norm_ffn_g': norm_ffn_g,
        'a_w_in': a_w_in, 'a_b_in': a_b_in, 'a_v_norm_g': a_v_norm_g, 'a_w_s': a_w_s,
        'a_b_s': a_b_s, 'a_w_out': a_w_out,
        'kv_mod_w': kv_mod_w, 'kv_mod_b': kv_mod_b, 'kv_norm_g': kv_norm_g, 'kv_w': kv_w,
        'k_norm_g': k_norm_g,
        'b_w_q': b_w_q, 'b_q_norm_g': b_q_norm_g, 'b_logit_bias': b_logit_bias, 'b_w_o': b_w_o,
        'peer_w_q': peer_w_q, 'peer_subkeys': peer_subkeys, 'peer_u': peer_u, 'peer_v': peer_v,
    }
    y_prompt, k_prompt, v_prompt, _ = forward(x_prompt, c_prompt, None, None, p)
    n_seq = page_table.shape[0]
    past_k = cache_k[page_table].reshape(n_seq, -1, N_HEADS, HEAD_DIM)
    past_v = cache_v[page_table].reshape(n_seq, -1, N_HEADS, HEAD_DIM)
    y_sample, k_sample, v_sample, gmlp_v_sample = forward(x_sample, c_sample, past_k, past_v, p)
    return (y_prompt, y_sample, k_prompt, v_prompt, k_sample, v_sample, gmlp_v_sample)
```

```python
import functools
import math

import jax
import jax.numpy as jnp
import numpy as np
from jax import lax
from jax.experimental import pallas as pl
from jax.experimental.pallas import tpu as pltpu

F32 = jnp.float32
BF16 = jnp.bfloat16
EPS = 1e-6

LANES = 128
SUBLANES = 8
BF16_ROWS = 16
VMEM_LIMIT = 56 * 1024 * 1024

CHUNK = 128
GROUPS = 16
N_HEADS = 16
HEAD_DIM = 128
PEER_HEADS = 8
PEER_NKEYS = 128
PEER_TOPK = 16
SAMPLE_ROWS = 128


def _cparams(sem):
    return pltpu.CompilerParams(dimension_semantics=sem, vmem_limit_bytes=VMEM_LIMIT)


def _const_spec(shape):
    nd = len(shape)
    return pl.BlockSpec(shape, lambda *_: (0,) * nd, pipeline_mode=pl.Buffered(1))


def _rms(x, g):
    ms = jnp.mean(x * x, axis=-1, keepdims=True)
    return x * lax.rsqrt(ms + EPS) * g


def _gelu(x):
    return 0.5 * x * (1.0 + lax.erf(x * np.float32(math.sqrt(0.5))))


def _head_rms(x, g):
    cols = []
    for h in range(x.shape[1] // HEAD_DIM):
        seg = x[:, h * HEAD_DIM:(h + 1) * HEAD_DIM]
        ms = jnp.mean(seg * seg, axis=-1, keepdims=True)
        cols.append(seg * lax.rsqrt(ms + EPS))
    return jnp.concatenate(cols, axis=1) * g


def _adaln_body(c_ref, w_ref, b_ref, o_ref):
    c = c_ref[...]
    s = c / (1.0 + jnp.exp(-c))
    o_ref[0] = jnp.dot(s.astype(BF16), w_ref[0].astype(BF16), preferred_element_type=F32) + b_ref[0]


def _adaln(c, w, b):
    n_l, d, n = w.shape
    m = c.shape[0]
    tn = 1024
    return pl.pallas_call(
        _adaln_body,
        grid=(n_l, n // tn),
        in_specs=[pl.BlockSpec((m, d), lambda l, j: (0, 0)),
                  pl.BlockSpec((1, d, tn), lambda l, j: (l, 0, j)),
                  pl.BlockSpec((1, 1, tn), lambda l, j: (l, 0, j))],
        out_specs=pl.BlockSpec((1, m, tn), lambda l, j: (l, 0, j)),
        out_shape=jax.ShapeDtypeStruct((n_l, m, n), F32),
        compiler_params=_cparams(("arbitrary", "arbitrary")),
        name="adaln",
    )(c, w, b.reshape(n_l, 1, n))


def _mod_spec(mod, rows):
    d = mod.shape[-1]
    if mod.shape[1] == 1:
        return pl.BlockSpec((1, 1, d), lambda b, i: (b, 0, 0))
    return pl.BlockSpec((1, rows, d), lambda b, i: (b, i, 0))


def _mixer_a_body(x_ref, sh_ref, sc_ref, gt_ref, ng_ref, win_ref, bin_ref, vg_ref, wmix_ref,
                  bmix_ref, wout_ref, x1_ref, v_ref, *, chunk):
    x = x_ref[0]
    rows = x.shape[0]
    h = _rms(x, ng_ref[...]) * (1.0 + sc_ref[0]) + sh_ref[0]
    z = jnp.dot(h.astype(BF16), win_ref[...], preferred_element_type=F32) + bin_ref[...]
    z = _gelu(z)
    width = z.shape[1] // 2
    u = z[:, :width]
    v = _rms(z[:, width:], vg_ref[...])
    v_ref[0] = v
    vb = v.astype(BF16)
    gd = width // GROUPS
    blocks = []
    for c in range(rows // chunk):
        cols = [jnp.dot(wmix_ref[g], vb[c * chunk:(c + 1) * chunk, g * gd:(g + 1) * gd],
                        preferred_element_type=F32) for g in range(GROUPS)]
        blocks.append(jnp.concatenate(cols, axis=1) + bmix_ref[...])
    mixed = blocks[0] if len(blocks) == 1 else jnp.concatenate(blocks, axis=0)
    um = (u * mixed).astype(BF16)
    mix = jnp.dot(um, wout_ref[...], preferred_element_type=F32)
    x1_ref[0] = x + gt_ref[0] * mix


def _mixer_a(x, shift, scale, gate, norm_g, w_in, b_in, v_g, wmix, bmix, w_out, rows):
    nb, s, d = x.shape
    width = w_out.shape[0]
    chunk = wmix.shape[1]
    xspec = pl.BlockSpec((1, rows, d), lambda b, i: (b, i, 0))
    return pl.pallas_call(
        functools.partial(_mixer_a_body, chunk=chunk),
        grid=(nb, s // rows),
        in_specs=[xspec, _mod_spec(shift, rows), _mod_spec(scale, rows), _mod_spec(gate, rows),
                  _const_spec((1, d)), _const_spec(w_in.shape), _const_spec((1, 2 * width)),
                  _const_spec((1, width)), _const_spec(wmix.shape), _const_spec(bmix.shape),
                  _const_spec(w_out.shape)],
        out_specs=[xspec, pl.BlockSpec((1, rows, width), lambda b, i: (b, i, 0))],
        out_shape=[jax.ShapeDtypeStruct((nb, s, d), F32), jax.ShapeDtypeStruct((nb, s, width), F32)],
        compiler_params=_cparams(("arbitrary", "arbitrary")),
        name="gmlp_mixer",
    )(x, shift, scale, gate, norm_g.reshape(1, d), w_in, b_in.reshape(1, -1), v_g.reshape(1, -1),
      wmix, bmix, w_out)


def _sort_network(n):
    pairs = []
    p = 1
    while p < n:
        k = p
        while k >= 1:
            for j in range(k % p, n - k, 2 * k):
                for i in range(min(k, n - j - k)):
                    if (i + j) // (2 * p) == (i + j + k) // (2 * p):
                        pairs.append((i + j, i + j + k))
            k //= 2
        p *= 2
    return pairs


_SORT16 = _sort_network(PEER_TOPK)
_BITONIC16 = [(i, i | d) for d in (8, 4, 2, 1) for i in range(PEER_TOPK) if not i & d]


def _apply_network(xs, pairs):
    xs = list(xs)
    for i, j in pairs:
        a, b = xs[i], xs[j]
        xs[i] = jnp.maximum(a, b)
        xs[j] = jnp.minimum(a, b)
    return xs


def _merge_top16(xs, ys):
    zs = [jnp.maximum(xs[r], ys[PEER_TOPK - 1 - r]) for r in range(PEER_TOPK)]
    return _apply_network(zs, _BITONIC16)


def _top16_of_keys(s):
    slabs = [s[SUBLANES * g:SUBLANES * (g + 1), :] for g in range(PEER_NKEYS // SUBLANES)]
    xs = _apply_network(slabs, _SORT16)
    for shift in (4, 2, 1):
        ys = [pltpu.roll(x, shift, axis=0) for x in xs]
        xs = _merge_top16(xs, ys)
    return xs


_CAND = [(r, c) for r in range(PEER_TOPK) for c in range(PEER_TOPK) if (r + 1) * (c + 1) <= PEER_TOPK]


def _route_body(x_ref, sh_ref, sc_ref, ng_ref, wq_ref, sk_ref,
                h2t_ref, r2_ref, e2_ref, c1_ref, e1_ref, s_ref):
    x = x_ref[0]
    h = _rms(x, ng_ref[...]) * (1.0 + sc_ref[0]) + sh_ref[0]
    ht = h.T.astype(BF16)
    h2t_ref[...] = ht
    qt = jnp.dot(wq_ref[...], ht, preferred_element_type=F32).astype(BF16)
    for hp in range(2 * PEER_HEADS):
        s_ref[hp] = jnp.dot(sk_ref[hp % 2], qt[hp * PEER_NKEYS:(hp + 1) * PEER_NKEYS, :],
                            preferred_element_type=F32)
    tb = x.shape[0]
    neg = jnp.full((SUBLANES, LANES), -jnp.inf, F32)
    sub = lax.broadcasted_iota(jnp.int32, (SUBLANES, LANES), 0)

    def chunk(lc, carry):
        lanes = pl.ds(pl.multiple_of(lc * LANES, LANES), LANES)
        top = [[neg] * PEER_TOPK, [neg] * PEER_TOPK]
        for hp in range(2 * PEER_HEADS):
            hh, p = hp // 2, hp % 2
            xs = _top16_of_keys(s_ref[hp, :, lanes])
            top[p] = [jnp.where(sub == hh, xs[r], top[p][r]) for r in range(PEER_TOPK)]
        a, b = top
        cand = {rc: a[rc[0]] + b[rc[1]] for rc in _CAND}
        g1 = [cand[(0, c)] for c in range(PEER_TOPK)]
        rest = [cand[rc] for rc in _CAND if rc[0] > 0]
        rest = rest + [neg] * (3 * PEER_TOPK - len(rest))
        gs = [_apply_network(rest[PEER_TOPK * k:PEER_TOPK * (k + 1)], _SORT16) for k in range(3)]
        m1 = _merge_top16(g1, gs[0])
        m2 = _merge_top16(gs[1], gs[2])
        zs = [jnp.maximum(m1[r], m2[PEER_TOPK - 1 - r]) for r in range(PEER_TOPK)]
        tau = functools.reduce(jnp.minimum, zs)
        top_sum = a[0] + b[0]
        zsum = jnp.zeros((SUBLANES, LANES), F32)
        for rc in _CAND:
            zsum = zsum + jnp.where(cand[rc] >= tau, jnp.exp(cand[rc] - top_sum), 0.0)
        zinv = 1.0 / zsum
        for hh in range(PEER_HEADS):
            def row(v):
                return jnp.broadcast_to(v[hh:hh + 1, :], (SUBLANES, LANES))
            bh = [row(b[c]) for c in range(PEER_TOPK)]
            tau_h, zinv_h, a0_h = row(tau), row(zinv), row(a[0])
            for gp in range(PEER_NKEYS // BF16_ROWS):
                r2s, e2s, c1s, e1s = [], [], [], []
                for g in (2 * gp, 2 * gp + 1):
                    rows = slice(SUBLANES * g, SUBLANES * (g + 1))
                    s1 = s_ref[2 * hh, rows, lanes]
                    s2 = s_ref[2 * hh + 1, rows, lanes]
                    r2 = jnp.zeros((SUBLANES, LANES), F32)
                    c1 = jnp.zeros((SUBLANES, LANES), F32)
                    for c in range(PEER_TOPK):
                        r2 = r2 + jnp.where(bh[c] > s2, 1.0, 0.0)
                        c1 = c1 + jnp.where(s1 + bh[c] >= tau_h, 1.0, 0.0)
                    r2s.append(r2)
                    c1s.append(c1)
                    e2s.append(jnp.exp(s2 - bh[0]))
                    e1s.append(jnp.exp(s1 - a0_h) * zinv_h)
                rows16 = slice(BF16_ROWS * gp, BF16_ROWS * (gp + 1))
                r2_ref[hh, rows16, lanes] = jnp.concatenate(r2s, axis=0).astype(BF16)
                e2_ref[hh, rows16, lanes] = jnp.concatenate(e2s, axis=0).astype(BF16)
                c1_ref[hh, rows16, lanes] = jnp.concatenate(c1s, axis=0)
                e1_ref[hh, rows16, lanes] = jnp.concatenate(e1s, axis=0)
        return carry

    lax.fori_loop(0, tb // LANES, chunk, 0)


def _peer_route(x, shift, scale, norm_g, wq_t, subkeys, tb):
    nb, s, d = x.shape
    t = nb * s
    nblk = s // tb
    tspec3 = pl.BlockSpec((PEER_HEADS, PEER_NKEYS, tb), lambda b, i: (0, 0, b * nblk + i))
    return pl.pallas_call(
        _route_body,
        grid=(nb, nblk),
        in_specs=[pl.BlockSpec((1, tb, d), lambda b, i: (b, i, 0)), _mod_spec(shift, tb), _mod_spec(scale, tb),
                  _const_spec((1, d)), _const_spec(wq_t.shape), _const_spec(subkeys.shape)],
        out_specs=[pl.BlockSpec((d, tb), lambda b, i: (0, b * nblk + i)), tspec3, tspec3, tspec3, tspec3],
        out_shape=[jax.ShapeDtypeStruct((d, t), BF16),
                   jax.ShapeDtypeStruct((PEER_HEADS, PEER_NKEYS, t), BF16),
                   jax.ShapeDtypeStruct((PEER_HEADS, PEER_NKEYS, t), BF16),
                   jax.ShapeDtypeStruct((PEER_HEADS, PEER_NKEYS, t), F32),
                   jax.ShapeDtypeStruct((PEER_HEADS, PEER_NKEYS, t), F32)],
        scratch_shapes=[pltpu.VMEM((2 * PEER_HEADS, PEER_NKEYS, tb), F32)],
        compiler_params=_cparams(("arbitrary", "arbitrary")),
        name="peer_route",
    )(x, shift, scale, norm_g.reshape(1, d), wq_t, subkeys)


def _peer_body(h2t_ref, u_ref, vt_ref, r2_ref, e2_ref, c1_ref, e1_ref, x_ref, g_ref, y_ref,
               acc_ref, act_ref, p_ref, *, n_i):
    eb = pl.program_id(2)

    @pl.when(eb == 0)
    def _():
        acc_ref[...] = jnp.zeros_like(acc_ref)

    act_ref[...] = jnp.dot(u_ref[...], h2t_ref[...], preferred_element_type=F32)
    tb = act_ref.shape[1]
    for il in range(n_i):
        i = eb * n_i + il
        c1i = c1_ref[i]
        e1i = e1_ref[i]
        for lc in range(tb // LANES):
            lanes = slice(lc * LANES, (lc + 1) * LANES)
            gate = [jnp.zeros((BF16_ROWS, LANES), BF16)] * (PEER_NKEYS // BF16_ROWS)
            for hh in range(PEER_HEADS):
                c1b = jnp.broadcast_to(c1i[hh:hh + 1, lanes], (BF16_ROWS, LANES)).astype(BF16)
                e1b = jnp.broadcast_to(e1i[hh:hh + 1, lanes], (BF16_ROWS, LANES)).astype(BF16)
                for rb in range(PEER_NKEYS // BF16_ROWS):
                    rows = slice(rb * BF16_ROWS, (rb + 1) * BF16_ROWS)
                    r2 = r2_ref[hh, rows, lanes]
                    e2 = e2_ref[hh, rows, lanes]
                    gate[rb] = gate[rb] + jnp.where(r2 < c1b, e2, jnp.zeros_like(e2)) * e1b
            for rb in range(PEER_NKEYS // BF16_ROWS):
                rows = slice(il * PEER_NKEYS + rb * BF16_ROWS, il * PEER_NKEYS + (rb + 1) * BF16_ROWS)
                a = act_ref[rows, lanes]
                p_ref[rows, lanes] = (_gelu(a) * gate[rb].astype(F32)).astype(BF16)
    acc_ref[...] += jnp.dot(vt_ref[...], p_ref[...], preferred_element_type=F32)

    @pl.when(eb == pl.num_programs(2) - 1)
    def _():
        y_ref[0] = x_ref[0] + g_ref[0] * acc_ref[...].T


def _peer_experts(x, gate, h2t, u_tab, vt_tab, r2, e2, c1, e1, tb, eb):
    nb, s, d = x.shape
    nblk = s // tb
    n_exp = u_tab.shape[0]
    n_i = eb // PEER_NKEYS
    t3 = pl.BlockSpec((PEER_HEADS, PEER_NKEYS, tb), lambda b, i, e: (0, 0, b * nblk + i))
    k3 = pl.BlockSpec((PEER_NKEYS, PEER_HEADS, tb), lambda b, i, e: (0, 0, b * nblk + i))
    xspec = pl.BlockSpec((1, tb, d), lambda b, i, e: (b, i, 0))
    if gate.shape[1] == 1:
        gspec = pl.BlockSpec((1, 1, d), lambda b, i, e: (b, 0, 0))
    else:
        gspec = pl.BlockSpec((1, tb, d), lambda b, i, e: (b, i, 0))
    return pl.pallas_call(
        functools.partial(_peer_body, n_i=n_i),
        grid=(nb, nblk, n_exp // eb),
        in_specs=[pl.BlockSpec((d, tb), lambda b, i, e: (0, b * nblk + i)),
                  pl.BlockSpec((eb, d), lambda b, i, e: (e, 0)),
                  pl.BlockSpec((d, eb), lambda b, i, e: (0, e)),
                  t3, t3, k3, k3, xspec, gspec],
        out_specs=xspec,
        out_shape=jax.ShapeDtypeStruct((nb, s, d), F32),
        scratch_shapes=[pltpu.VMEM((d, tb), F32), pltpu.VMEM((eb, tb), F32), pltpu.VMEM((eb, tb), BF16)],
        compiler_params=_cparams(("arbitrary", "arbitrary", "arbitrary")),
        name="peer_experts",
    )(h2t, u_tab, vt_tab, r2, e2, c1, e1, x, gate)


def _peer(x, shift, scale, gate, norm_g, wq_t, subkeys, u_tab, vt_tab, tb, eb):
    h2t, r2, e2, c1, e1 = _peer_route(x, shift, scale, norm_g, wq_t, subkeys, tb)
    c1 = jnp.transpose(c1, (1, 0, 2))
    e1 = jnp.transpose(e1, (1, 0, 2))
    return _peer_experts(x, gate, h2t, u_tab, vt_tab, r2, e2, c1, e1, tb, eb)


def _kvq_body(x_ref, ksh_ref, ksc_ref, msh_ref, msc_ref, kng_ref, mng_ref, wkv_ref, wq_ref,
              kg_ref, qg_ref, k_ref, v_ref, kb_ref, vb_ref, q_ref):
    x = x_ref[0]
    d = x.shape[1]
    ms = jnp.mean(x * x, axis=-1, keepdims=True)
    xn = x * lax.rsqrt(ms + EPS)
    hk = (xn * kng_ref[...]) * (1.0 + ksc_ref[0]) + ksh_ref[0]
    kv = jnp.dot(hk.astype(BF16), wkv_ref[...], preferred_element_type=F32)
    k = _head_rms(kv[:, :d], kg_ref[...])
    v = kv[:, d:]
    k_ref[0] = k
    v_ref[0] = v
    kb_ref[0] = k.astype(BF16)
    vb_ref[0] = v.astype(BF16)
    hq = (xn * mng_ref[...]) * (1.0 + msc_ref[0]) + msh_ref[0]
    q = jnp.dot(hq.astype(BF16), wq_ref[...], preferred_element_type=F32)
    q_ref[0] = _head_rms(q, qg_ref[...]).astype(BF16)


def _kvq(x, k_shift, k_scale, m_shift, m_scale, kv_norm_g, mix_norm_g, w_kv, w_q, k_gain, q_gain, rows):
    nb, s, d = x.shape
    xspec = pl.BlockSpec((1, rows, d), lambda b, i: (b, i, 0))
    return pl.pallas_call(
        _kvq_body,
        grid=(nb, s // rows),
        in_specs=[xspec, _mod_spec(k_shift, rows), _mod_spec(k_scale, rows), _mod_spec(m_shift, rows),
                  _mod_spec(m_scale, rows), _const_spec((1, d)), _const_spec((1, d)),
                  _const_spec(w_kv.shape), _const_spec(w_q.shape), _const_spec((1, d)), _const_spec((1, d))],
        out_specs=[xspec] * 5,
        out_shape=[jax.ShapeDtypeStruct((nb, s, d), F32), jax.ShapeDtypeStruct((nb, s, d), F32),
                   jax.ShapeDtypeStruct((nb, s, d), BF16), jax.ShapeDtypeStruct((nb, s, d), BF16),
                   jax.ShapeDtypeStruct((nb, s, d), BF16)],
        compiler_params=_cparams(("arbitrary", "arbitrary")),
        name="kv_q_proj",
    )(x, k_shift, k_scale, m_shift, m_scale, kv_norm_g.reshape(1, d), mix_norm_g.reshape(1, d),
      w_kv, w_q, k_gain, q_gain)


def _softplus(z):
    return jnp.maximum(z, 0.0) + jnp.log1p(jnp.exp(-jnp.abs(z)))


def _split_bf16(x):
    hi = x.astype(BF16)
    lo = (x - hi.astype(F32)).astype(BF16)
    return hi, lo


def _sb_prompt_body(q_ref, k_ref, v_ref, bias_ref, tri_ref, o_ref, *, blk, scale):
    qi = pl.program_id(2)
    q = q_ref[0]
    bias = bias_ref[0]
    row = lax.broadcasted_iota(jnp.int32, (blk, blk), 0)
    col = lax.broadcasted_iota(jnp.int32, (blk, blk), 1)

    def step(j, carry):
        acc, run = carry
        kb = qi - j
        ks = pl.multiple_of(kb * blk, blk)
        kblk = k_ref[0, pl.ds(ks, blk), :]
        vblk = v_ref[0, pl.ds(ks, blk), :]
        z = lax.dot_general(q, kblk, (((1,), (1,)), ((), ())), preferred_element_type=F32) * scale + bias
        mask = (col + kb * blk) < (row + qi * blk)
        sp = _softplus(z)
        lm = jnp.where(mask, -sp, 0.0)
        hi, lo = _split_bf16(lm)
        surv = jnp.dot(jnp.concatenate([hi, lo], axis=1), tri_ref[...], preferred_element_type=F32) + run
        a = jnp.where(mask, jnp.exp(z - sp + surv), 0.0)
        acc = acc + jnp.dot(a.astype(BF16), vblk, preferred_element_type=F32)
        run = run + jnp.sum(lm, axis=1, keepdims=True)
        return acc, run

    acc, _ = lax.fori_loop(0, qi + 1, step,
                           (jnp.zeros((blk, HEAD_DIM), F32), jnp.zeros((blk, 1), F32)))
    o_ref[0] = acc.astype(o_ref.dtype)


def _suffix_matrix(n):
    m = (np.arange(n)[:, None] > np.arange(n)[None, :]).astype(np.float32)
    return jnp.asarray(np.concatenate([m, m], axis=0), dtype=BF16)


def _sb_prompt(q, k, v, logit_bias, blk):
    nb, s, d = q.shape
    bias = jnp.broadcast_to(logit_bias.astype(F32)[:, None, None], (N_HEADS, 1, blk))
    qspec = pl.BlockSpec((1, blk, HEAD_DIM), lambda b, h, i: (b, i, h))
    kvspec = pl.BlockSpec((1, s, HEAD_DIM), lambda b, h, i: (b, 0, h))
    return pl.pallas_call(
        functools.partial(_sb_prompt_body, blk=blk, scale=np.float32(HEAD_DIM ** -0.5)),
        grid=(nb, N_HEADS, s // blk),
        in_specs=[qspec, kvspec, kvspec, pl.BlockSpec((1, 1, blk), lambda b, h, i: (h, 0, 0)),
                  pl.BlockSpec((2 * blk, blk), lambda b, h, i: (0, 0))],
        out_specs=qspec,
        out_shape=jax.ShapeDtypeStruct((nb, s, d), BF16),
        compiler_params=_cparams(("arbitrary", "arbitrary", "arbitrary")),
        name="sb_prompt",
    )(q, k, v, bias, _suffix_matrix(blk))


def _sb_paged_body(pt_ref, *refs, pages_per_step, n_q, scale):
    kp = refs[:pages_per_step]
    vp = refs[pages_per_step:2 * pages_per_step]
    wq_ref, bias_ref, knew_ref, vnew_ref, tri_ref, o_ref, acc_ref, run_ref = refs[2 * pages_per_step:]
    step = pl.program_id(1)
    wq = wq_ref[0]
    bias = bias_ref[...]

    def block(kblk, vblk, mask):
        z = jnp.dot(kblk.astype(BF16), wq, preferred_element_type=F32) * scale + bias
        sp = _softplus(z)
        lm = -sp if mask is None else jnp.where(mask, -sp, 0.0)
        hi, lo = _split_bf16(lm)
        surv = jnp.dot(tri_ref[...], jnp.concatenate([hi, lo], axis=0),
                       preferred_element_type=F32) + run_ref[...]
        a = jnp.exp(z - sp + surv)
        if mask is not None:
            a = jnp.where(mask, a, 0.0)
        acc_ref[...] += jnp.dot(a.T.astype(BF16), vblk.astype(BF16), preferred_element_type=F32)
        run_ref[...] += jnp.sum(lm, axis=0, keepdims=True)

    @pl.when(step == 0)
    def _():
        acc_ref[...] = jnp.zeros_like(acc_ref)
        run_ref[...] = jnp.zeros_like(run_ref)
        shape = (knew_ref.shape[1], wq.shape[1])
        kpos = lax.broadcasted_iota(jnp.int32, shape, 0)
        qpos = lax.broadcasted_iota(jnp.int32, shape, 1) % n_q
        block(knew_ref[0], vnew_ref[0], kpos < qpos)

    for i in range(pages_per_step):
        block(kp[i][0], vp[i][0], None)

    @pl.when(step == pl.num_programs(1) - 1)
    def _():
        o_ref[0] = acc_ref[...]


def _suffix_matrix_t(n):
    m = (np.arange(n)[None, :] > np.arange(n)[:, None]).astype(np.float32)
    return jnp.asarray(np.concatenate([m, m], axis=1), dtype=BF16)


def _sb_paged(q, k_new, v_new, cache_k, cache_v, page_table, logit_bias, pages_per_step):
    nb, n_q, d = q.shape
    n_pool, page = cache_k.shape[:2]
    n_pages = page_table.shape[1]
    ck = cache_k.reshape(n_pool, page, d)
    cv = cache_v.reshape(n_pool, page, d)
    ncol = LANES
    assert N_HEADS * n_q <= ncol and n_q <= page
    qh = q.reshape(nb, n_q, N_HEADS, HEAD_DIM)
    eye = jnp.eye(N_HEADS, dtype=BF16)
    wq = jnp.einsum('bqhd,hg->bhdgq', qh, eye).reshape(nb, d, N_HEADS * n_q)
    wq = jnp.pad(wq, ((0, 0), (0, 0), (0, ncol - N_HEADS * n_q)))
    bias = jnp.pad(jnp.repeat(logit_bias.astype(F32), n_q), (0, ncol - N_HEADS * n_q)).reshape(1, ncol)
    knew = jnp.pad(k_new, ((0, 0), (0, page - n_q), (0, 0)))
    vnew = jnp.pad(v_new, ((0, 0), (0, page - n_q), (0, 0)))
    pt = page_table.reshape(-1).astype(jnp.int32)

    def page_map(i):
        return lambda b, s, pt_ref: (pt_ref[b * n_pages + n_pages - 1 - (s * pages_per_step + i)], 0, 0)

    page_specs = [pl.BlockSpec((1, page, d), page_map(i)) for i in range(pages_per_step)]
    grid_spec = pltpu.PrefetchScalarGridSpec(
        num_scalar_prefetch=1,
        grid=(nb, n_pages // pages_per_step),
        in_specs=page_specs + page_specs + [
            pl.BlockSpec((1, d, ncol), lambda b, s, pt_ref: (b, 0, 0)),
            pl.BlockSpec((1, ncol), lambda b, s, pt_ref: (0, 0)),
            pl.BlockSpec((1, page, d), lambda b, s, pt_ref: (b, 0, 0)),
            pl.BlockSpec((1, page, d), lambda b, s, pt_ref: (b, 0, 0)),
            pl.BlockSpec((page, 2 * page), lambda b, s, pt_ref: (0, 0))],
        out_specs=pl.BlockSpec((1, ncol, d), lambda b, s, pt_ref: (b, 0, 0)),
        scratch_shapes=[pltpu.VMEM((ncol, d), F32), pltpu.VMEM((1, ncol), F32)])
    return pl.pallas_call(
        functools.partial(_sb_paged_body, pages_per_step=pages_per_step, n_q=n_q,
                          scale=np.float32(HEAD_DIM ** -0.5)),
        grid_spec=grid_spec,
        out_shape=jax.ShapeDtypeStruct((nb, ncol, d), F32),
        compiler_params=_cparams(("arbitrary", "arbitrary")),
        name="sb_paged",
    )(pt, *([ck] * pages_per_step), *([cv] * pages_per_step), wq, bias, knew, vnew,
      _suffix_matrix_t(page))


def _oproj_body(o_ref, x_ref, gt_ref, w_ref, y_ref):
    mix = jnp.dot(o_ref[0], w_ref[...], preferred_element_type=F32)
    y_ref[0] = x_ref[0] + gt_ref[0] * mix


def _oproj(o, x, gate, w_o, rows):
    nb, s, d = x.shape
    xspec = pl.BlockSpec((1, rows, d), lambda b, i: (b, i, 0))
    return pl.pallas_call(
        _oproj_body,
        grid=(nb, s // rows),
        in_specs=[xspec, xspec, _mod_spec(gate, rows), _const_spec(w_o.shape)],
        out_specs=xspec,
        out_shape=jax.ShapeDtypeStruct((nb, s, d), F32),
        compiler_params=_cparams(("arbitrary", "arbitrary")),
        name="attn_out_proj",
    )(o, x, gate, w_o)


def _mix_weights_prompt(w_s, b_s, width):
    causal = jnp.tril(jnp.ones((CHUNK, CHUNK), dtype=bool))
    wmix = jnp.where(causal[None], w_s, 0.0).astype(BF16)
    bmix = jnp.repeat(b_s.T, width // GROUPS, axis=1)
    return wmix, bmix


def _mix_weights_sample(w_s, b_s, width, n_seq, n_tok):
    causal = jnp.tril(jnp.ones((n_tok, n_tok), dtype=bool))
    small = jnp.where(causal[None], w_s[:, :n_tok, :n_tok], 0.0)
    eye = jnp.eye(n_seq, dtype=F32)
    blockdiag = jnp.einsum('ab,gts->gatbs', eye, small).reshape(GROUPS, n_seq * n_tok, n_seq * n_tok)
    padn = SAMPLE_ROWS - n_seq * n_tok
    wmix = jnp.pad(blockdiag, ((0, 0), (0, padn), (0, padn))).astype(BF16)
    brow = jnp.tile(b_s.T[:n_tok], (n_seq, 1))
    bmix = jnp.repeat(jnp.pad(brow, ((0, padn), (0, 0))), width // GROUPS, axis=1)
    return wmix, bmix


def kernel(x_prompt, x_sample, cache_k, cache_v, page_table, c_prompt, c_sample, mod_w, mod_b, norm_mix_g, norm_ffn_g, a_w_in, a_b_in, a_v_norm_g, a_w_s, a_b_s, a_w_out, kv_mod_w, kv_mod_b, kv_norm_g, kv_w, k_norm_g, b_w_q, b_q_norm_g, b_logit_bias, b_w_o, peer_w_q, peer_subkeys, peer_u, peer_v):
    nbp, seq, d = x_prompt.shape
    nbs, n_tok, _ = x_sample.shape
    n_samp = nbs * n_tok
    width = a_w_out.shape[1]

    c_all = jnp.concatenate([c_prompt, c_sample], axis=0)
    c_all = jnp.pad(c_all, ((0, BF16_ROWS - c_all.shape[0]), (0, 0)))
    mods = _adaln(c_all, mod_w, mod_b)
    kvmod = _adaln(c_all, kv_mod_w[None], kv_mod_b[None])[0]

    def split_mod(m, n):
        parts = jnp.split(m, n, axis=-1)
        prm = [p[:nbp, None, :] for p in parts]
        smp = [jnp.pad(jnp.repeat(p[nbp:nbp + nbs], n_tok, axis=0), ((0, SAMPLE_ROWS - n_samp), (0, 0)))[None]
               for p in parts]
        return prm, smp

    mod_p, mod_s = zip(*[split_mod(mods[l], 6) for l in range(mods.shape[0])])
    kvmod_p, kvmod_s = split_mod(kvmod, 2)

    w_in = a_w_in[0].astype(BF16)
    w_out = a_w_out[0].astype(BF16)
    wq_t = [jnp.transpose(peer_w_q[l]).astype(BF16) for l in range(2)]
    subk = [peer_subkeys[l].astype(BF16) for l in range(2)]
    u_tab = [peer_u[l].astype(BF16) for l in range(2)]
    vt_tab = [jnp.transpose(peer_v[l]).astype(BF16) for l in range(2)]
    w_kv = kv_w.astype(BF16)
    w_q1 = b_w_q[0].astype(BF16)
    w_o1 = b_w_o[0].astype(BF16)
    k_gain = jnp.tile(k_norm_g, N_HEADS).reshape(1, d)
    q_gain = jnp.tile(b_q_norm_g[0], N_HEADS).reshape(1, d)
    wmix_p, bmix_p = _mix_weights_prompt(a_w_s[0], a_b_s[0], width)
    wmix_s, bmix_s = _mix_weights_sample(a_w_s[0], a_b_s[0], width, nbs, n_tok)

    xs = jnp.pad(x_sample.reshape(1, n_samp, d), ((0, 0), (0, SAMPLE_ROWS - n_samp), (0, 0)))

    def trunk(x, mod, kvm, wmix, bmix, rows, tb, attend):
        sh_m, sc_m, g_m, sh_f, sc_f, g_f = mod[0]
        x1, v_rows = _mixer_a(x, sh_m, sc_m, g_m, norm_mix_g[0], w_in, a_b_in[0], a_v_norm_g[0],
                              wmix, bmix, w_out, rows)
        x2 = _peer(x1, sh_f, sc_f, g_f, norm_ffn_g[0], wq_t[0], subk[0], u_tab[0], vt_tab[0], tb, 512)
        sh_m, sc_m, g_m, sh_f, sc_f, g_f = mod[1]
        k, v, kb, vb, q = _kvq(x2, kvm[0], kvm[1], sh_m, sc_m, kv_norm_g, norm_mix_g[1], w_kv, w_q1,
                               k_gain, q_gain, rows)
        o = attend(q, k, v, kb, vb)
        x3 = _oproj(o, x2, g_m, w_o1, rows)
        y = _peer(x3, sh_f, sc_f, g_f, norm_ffn_g[1], wq_t[1], subk[1], u_tab[1], vt_tab[1], tb, 512)
        return y, k, v, v_rows

    def attend_prompt(q, k, v, kb, vb):
        return _sb_prompt(q, kb, vb, b_logit_bias[0], 256)

    def attend_sample(q, k, v, kb, vb):
        qs = q[0, :n_samp].reshape(nbs, n_tok, d)
        ks = k[0, :n_samp].reshape(nbs, n_tok, d)
        vs = v[0, :n_samp].reshape(nbs, n_tok, d)
        full = _sb_paged(qs, ks, vs, cache_k, cache_v, page_table, b_logit_bias[0], 4)
        full = full[:, :N_HEADS * n_tok].reshape(nbs, N_HEADS, n_tok, N_HEADS, HEAD_DIM)
        idx = jnp.arange(N_HEADS)
        o = full[:, idx, :, idx, :]
        o = jnp.transpose(o, (1, 2, 0, 3)).reshape(1, n_samp, d)
        return jnp.pad(o, ((0, 0), (0, SAMPLE_ROWS - n_samp), (0, 0))).astype(BF16)

    y_p, k_p, v_p, _ = trunk(x_prompt, mod_p, kvmod_p, wmix_p, bmix_p, 256, 512, attend_prompt)
    y_s, k_s, v_s, gv_s = trunk(xs, mod_s, kvmod_s, wmix_s, bmix_s, SAMPLE_ROWS, SAMPLE_ROWS, attend_sample)

    def samp(a, shape):
        return a[0, :n_samp].reshape(shape)

    return (y_p,
            samp(y_s, (nbs, n_tok, d)),
            k_p.reshape(nbp, seq, N_HEADS, HEAD_DIM),
            v_p.reshape(nbp, seq, N_HEADS, HEAD_DIM),
            samp(k_s, (nbs, n_tok, N_HEADS, HEAD_DIM)),
            samp(v_s, (nbs, n_tok, N_HEADS, HEAD_DIM)),
            samp(gv_s, (1, nbs, n_tok, width)))
```

```python
import functools
import math

import jax
import jax.numpy as jnp
import numpy as np
from jax import lax
from jax.experimental import pallas as pl
from jax.experimental.pallas import tpu as pltpu

F32 = jnp.float32
BF16 = jnp.bfloat16
EPS = 1e-6

LANES = 128
SUBLANES = 8
BF16_ROWS = 16
VMEM_LIMIT = 56 * 1024 * 1024

CHUNK = 128
GROUPS = 16
N_HEADS = 16
HEAD_DIM = 128
PEER_HEADS = 8
PEER_NKEYS = 128
PEER_TOPK = 16
SAMPLE_ROWS = 128


def _cparams(sem):
    return pltpu.CompilerParams(dimension_semantics=sem, vmem_limit_bytes=VMEM_LIMIT)


def _const_spec(shape):
    nd = len(shape)
    return pl.BlockSpec(shape, lambda *_: (0,) * nd, pipeline_mode=pl.Buffered(1))


def _rms(x, g):
    ms = jnp.mean(x * x, axis=-1, keepdims=True)
    return x * lax.rsqrt(ms + EPS) * g


def _gelu(x):
    return 0.5 * x * (1.0 + lax.erf(x * np.float32(math.sqrt(0.5))))


def _head_rms(x, g):
    cols = []
    for h in range(x.shape[1] // HEAD_DIM):
        seg = x[:, h * HEAD_DIM:(h + 1) * HEAD_DIM]
        ms = jnp.mean(seg * seg, axis=-1, keepdims=True)
        cols.append(seg * lax.rsqrt(ms + EPS))
    return jnp.concatenate(cols, axis=1) * g


def _adaln_body(c_ref, w_ref, b_ref, o_ref):
    c = c_ref[...]
    s = c / (1.0 + jnp.exp(-c))
    o_ref[0] = jnp.dot(s.astype(BF16), w_ref[0].astype(BF16), preferred_element_type=F32) + b_ref[0]


def _adaln(c, w, b):
    n_l, d, n = w.shape
    m = c.shape[0]
    tn = 1024
    return pl.pallas_call(
        _adaln_body,
        grid=(n_l, n // tn),
        in_specs=[pl.BlockSpec((m, d), lambda l, j: (0, 0)),
                  pl.BlockSpec((1, d, tn), lambda l, j: (l, 0, j)),
                  pl.BlockSpec((1, 1, tn), lambda l, j: (l, 0, j))],
        out_specs=pl.BlockSpec((1, m, tn), lambda l, j: (l, 0, j)),
        out_shape=jax.ShapeDtypeStruct((n_l, m, n), F32),
        compiler_params=_cparams(("arbitrary", "arbitrary")),
        name="adaln",
    )(c, w, b.reshape(n_l, 1, n))


def _mod_spec(mod, rows):
    d = mod.shape[-1]
    if mod.shape[1] == 1:
        return pl.BlockSpec((1, 1, d), lambda b, i: (b, 0, 0))
    return pl.BlockSpec((1, rows, d), lambda b, i: (b, i, 0))


def _mixer_a_body(x_ref, sh_ref, sc_ref, gt_ref, ng_ref, win_ref, bin_ref, vg_ref, wmix_ref,
                  bmix_ref, wout_ref, x1_ref, v_ref, *, chunk):
    x = x_ref[0]
    rows = x.shape[0]
    h = _rms(x, ng_ref[...]) * (1.0 + sc_ref[0]) + sh_ref[0]
    z = jnp.dot(h.astype(BF16), win_ref[...], preferred_element_type=F32) + bin_ref[...]
    z = _gelu(z)
    width = z.shape[1] // 2
    u = z[:, :width]
    v = _rms(z[:, width:], vg_ref[...])
    v_ref[0] = v
    vb = v.astype(BF16)
    gd = width // GROUPS
    blocks = []
    for c in range(rows // chunk):
        cols = [jnp.dot(wmix_ref[g], vb[c * chunk:(c + 1) * chunk, g * gd:(g + 1) * gd],
                        preferred_element_type=F32) for g in range(GROUPS)]
        blocks.append(jnp.concatenate(cols, axis=1) + bmix_ref[...])
    mixed = blocks[0] if len(blocks) == 1 else jnp.concatenate(blocks, axis=0)
    um = (u * mixed).astype(BF16)
    mix = jnp.dot(um, wout_ref[...], preferred_element_type=F32)
    x1_ref[0] = x + gt_ref[0] * mix


def _mixer_a(x, shift, scale, gate, norm_g, w_in, b_in, v_g, wmix, bmix, w_out, rows):
    nb, s, d = x.shape
    width = w_out.shape[0]
    chunk = wmix.shape[1]
    xspec = pl.BlockSpec((1, rows, d), lambda b, i: (b, i, 0))
    return pl.pallas_call(
        functools.partial(_mixer_a_body, chunk=chunk),
        grid=(nb, s // rows),
        in_specs=[xspec, _mod_spec(shift, rows), _mod_spec(scale, rows), _mod_spec(gate, rows),
                  _const_spec((1, d)), _const_spec(w_in.shape), _const_spec((1, 2 * width)),
                  _const_spec((1, width)), _const_spec(wmix.shape), _const_spec(bmix.shape),
                  _const_spec(w_out.shape)],
        out_specs=[xspec, pl.BlockSpec((1, rows, width), lambda b, i: (b, i, 0))],
        out_shape=[jax.ShapeDtypeStruct((nb, s, d), F32), jax.ShapeDtypeStruct((nb, s, width), F32)],
        compiler_params=_cparams(("arbitrary", "arbitrary")),
        name="gmlp_mixer",
    )(x, shift, scale, gate, norm_g.reshape(1, d), w_in, b_in.reshape(1, -1), v_g.reshape(1, -1),
      wmix, bmix, w_out)


def _sort_network(n):
    pairs = []
    p = 1
    while p < n:
        k = p
        while k >= 1:
            for j in range(k % p, n - k, 2 * k):
                for i in range(min(k, n - j - k)):
                    if (i + j) // (2 * p) == (i + j + k) // (2 * p):
                        pairs.append((i + j, i + j + k))
            k //= 2
        p *= 2
    return pairs


_SORT16 = _sort_network(PEER_TOPK)
_BITONIC16 = [(i, i | d) for d in (8, 4, 2, 1) for i in range(PEER_TOPK) if not i & d]


def _apply_network(xs, pairs):
    xs = list(xs)
    for i, j in pairs:
        a, b = xs[i], xs[j]
        xs[i] = jnp.maximum(a, b)
        xs[j] = jnp.minimum(a, b)
    return xs


def _merge_top16(xs, ys):
    zs = [jnp.maximum(xs[r], ys[PEER_TOPK - 1 - r]) for r in range(PEER_TOPK)]
    return _apply_network(zs, _BITONIC16)


def _top16_of_keys(s):
    slabs = [s[SUBLANES * g:SUBLANES * (g + 1), :] for g in range(PEER_NKEYS // SUBLANES)]
    xs = _apply_network(slabs, _SORT16)
    for shift in (4, 2, 1):
        ys = [pltpu.roll(x, shift, axis=0) for x in xs]
        xs = _merge_top16(xs, ys)
    return xs


_CAND = [(r, c) for r in range(PEER_TOPK) for c in range(PEER_TOPK) if (r + 1) * (c + 1) <= PEER_TOPK]


def _route_body(x_ref, sh_ref, sc_ref, ng_ref, wq_ref, sk_ref,
                h2t_ref, r2_ref, e2_ref, c1_ref, e1_ref, s_ref):
    x = x_ref[0]
    h = _rms(x, ng_ref[...]) * (1.0 + sc_ref[0]) + sh_ref[0]
    ht = h.T.astype(BF16)
    h2t_ref[...] = ht
    qt = jnp.dot(wq_ref[...], ht, preferred_element_type=F32).astype(BF16)
    for hp in range(2 * PEER_HEADS):
        s_ref[hp] = jnp.dot(sk_ref[hp % 2], qt[hp * PEER_NKEYS:(hp + 1) * PEER_NKEYS, :],
                            preferred_element_type=F32)
    tb = x.shape[0]
    neg = jnp.full((SUBLANES, LANES), -jnp.inf, F32)
    sub = lax.broadcasted_iota(jnp.int32, (SUBLANES, LANES), 0)

    def chunk(lc, carry):
        lanes = pl.ds(pl.multiple_of(lc * LANES, LANES), LANES)
        top = [[neg] * PEER_TOPK, [neg] * PEER_TOPK]
        for hp in range(2 * PEER_HEADS):
            hh, p = hp // 2, hp % 2
            xs = _top16_of_keys(s_ref[hp, :, lanes])
            top[p] = [jnp.where(sub == hh, xs[r], top[p][r]) for r in range(PEER_TOPK)]
        a, b = top
        cand = {rc: a[rc[0]] + b[rc[1]] for rc in _CAND}
        g1 = [cand[(0, c)] for c in range(PEER_TOPK)]
        rest = [cand[rc] for rc in _CAND if rc[0] > 0]
        rest = rest + [neg] * (3 * PEER_TOPK - len(rest))
        gs = [_apply_network(rest[PEER_TOPK * k:PEER_TOPK * (k + 1)], _SORT16) for k in range(3)]
        m1 = _merge_top16(g1, gs[0])
        m2 = _merge_top16(gs[1], gs[2])
        zs = [jnp.maximum(m1[r], m2[PEER_TOPK - 1 - r]) for r in range(PEER_TOPK)]
        tau = functools.reduce(jnp.minimum, zs)
        top_sum = a[0] + b[0]
        zsum = jnp.zeros((SUBLANES, LANES), F32)
        for rc in _CAND:
            zsum = zsum + jnp.where(cand[rc] >= tau, jnp.exp(cand[rc] - top_sum), 0.0)
        zinv = 1.0 / zsum
        for hh in range(PEER_HEADS):
            def row(v):
                return jnp.broadcast_to(v[hh:hh + 1, :], (SUBLANES, LANES))
            bh = [row(b[c]) for c in range(PEER_TOPK)]
            tau_h, zinv_h, a0_h = row(tau), row(zinv), row(a[0])
            for gp in range(PEER_NKEYS // BF16_ROWS):
                r2s, e2s, c1s, e1s = [], [], [], []
                for g in (2 * gp, 2 * gp + 1):
                    rows = slice(SUBLANES * g, SUBLANES * (g + 1))
                    s1 = s_ref[2 * hh, rows, lanes]
                    s2 = s_ref[2 * hh + 1, rows, lanes]
                    r2 = jnp.zeros((SUBLANES, LANES), F32)
                    c1 = jnp.zeros((SUBLANES, LANES), F32)
                    for c in range(PEER_TOPK):
                        r2 = r2 + jnp.where(bh[c] > s2, 1.0, 0.0)
                        c1 = c1 + jnp.where(s1 + bh[c] >= tau_h, 1.0, 0.0)
                    r2s.append(r2)
                    c1s.append(c1)
                    e2s.append(jnp.exp(s2 - bh[0]))
                    e1s.append(jnp.exp(s1 - a0_h) * zinv_h)
                rows16 = slice(BF16_ROWS * gp, BF16_ROWS * (gp + 1))
                r2_ref[hh, rows16, lanes] = jnp.concatenate(r2s, axis=0).astype(BF16)
                e2_ref[hh, rows16, lanes] = jnp.concatenate(e2s, axis=0).astype(BF16)
                c1_ref[hh, rows16, lanes] = jnp.concatenate(c1s, axis=0)
                e1_ref[hh, rows16, lanes] = jnp.concatenate(e1s, axis=0)
        return carry

    lax.fori_loop(0, tb // LANES, chunk, 0)


def _peer_route(x, shift, scale, norm_g, wq_t, subkeys, tb):
    nb, s, d = x.shape
    t = nb * s
    nblk = s // tb
    tspec3 = pl.BlockSpec((PEER_HEADS, PEER_NKEYS, tb), lambda b, i: (0, 0, b * nblk + i))
    return pl.pallas_call(
        _route_body,
        grid=(nb, nblk),
        in_specs=[pl.BlockSpec((1, tb, d), lambda b, i: (b, i, 0)), _mod_spec(shift, tb), _mod_spec(scale, tb),
                  _const_spec((1, d)), _const_spec(wq_t.shape), _const_spec(subkeys.shape)],
        out_specs=[pl.BlockSpec((d, tb), lambda b, i: (0, b * nblk + i)), tspec3, tspec3, tspec3, tspec3],
        out_shape=[jax.ShapeDtypeStruct((d, t), BF16),
                   jax.ShapeDtypeStruct((PEER_HEADS, PEER_NKEYS, t), BF16),
                   jax.ShapeDtypeStruct((PEER_HEADS, PEER_NKEYS, t), BF16),
                   jax.ShapeDtypeStruct((PEER_HEADS, PEER_NKEYS, t), F32),
                   jax.ShapeDtypeStruct((PEER_HEADS, PEER_NKEYS, t), F32)],
        scratch_shapes=[pltpu.VMEM((2 * PEER_HEADS, PEER_NKEYS, tb), F32)],
        compiler_params=_cparams(("arbitrary", "arbitrary")),
        name="peer_route",
    )(x, shift, scale, norm_g.reshape(1, d), wq_t, subkeys)


def _peer_gate_stage(act_ref, p_ref, r2_ref, e2_ref, c1_ref, e1_ref, key0, n_i):
    tb = act_ref.shape[1]
    for il in range(n_i):
        c1i = c1_ref[key0 + il]
        e1i = e1_ref[key0 + il]
        for lc in range(tb // LANES):
            lanes = slice(lc * LANES, (lc + 1) * LANES)
            gate = [jnp.zeros((BF16_ROWS, LANES), BF16)] * (PEER_NKEYS // BF16_ROWS)
            for hh in range(PEER_HEADS):
                c1b = jnp.broadcast_to(c1i[hh:hh + 1, lanes], (BF16_ROWS, LANES)).astype(BF16)
                e1b = jnp.broadcast_to(e1i[hh:hh + 1, lanes], (BF16_ROWS, LANES)).astype(BF16)
                for rb in range(PEER_NKEYS // BF16_ROWS):
                    rows = slice(rb * BF16_ROWS, (rb + 1) * BF16_ROWS)
                    r2 = r2_ref[hh, rows, lanes]
                    e2 = e2_ref[hh, rows, lanes]
                    gate[rb] = gate[rb] + jnp.where(r2 < c1b, e2, jnp.zeros_like(e2)) * e1b
            for rb in range(PEER_NKEYS // BF16_ROWS):
                rows = slice(il * PEER_NKEYS + rb * BF16_ROWS, il * PEER_NKEYS + (rb + 1) * BF16_ROWS)
                a = act_ref[rows, lanes]
                p_ref[rows, lanes] = (_gelu(a) * gate[rb].astype(F32)).astype(BF16)


def _peer_body(h2t_ref, u_ref, vta_ref, vtb_ref, r2_ref, e2_ref, c1_ref, e1_ref, x_ref, g_ref, y_ref,
               acc_ref, acta_ref, actb_ref, p_ref, *, n_i):
    s = pl.program_id(2)
    last = pl.num_programs(2) - 1
    eb = acta_ref.shape[0]
    routing = (r2_ref, e2_ref, c1_ref, e1_ref)

    @pl.when(s == 0)
    def _():
        acc_ref[...] = jnp.zeros_like(acc_ref)
        actb_ref[...] = jnp.zeros_like(actb_ref)

    acta_ref[...] = jnp.dot(u_ref[:eb, :], h2t_ref[...], preferred_element_type=F32)
    _peer_gate_stage(actb_ref, p_ref, *routing, jnp.maximum(2 * s - 1, 0) * n_i, n_i)
    acc_ref[...] += jnp.dot(vta_ref[...], p_ref[...], preferred_element_type=F32)

    @pl.when(s < last)
    def _():
        actb_ref[...] = jnp.dot(u_ref[eb:, :], h2t_ref[...], preferred_element_type=F32)
        _peer_gate_stage(acta_ref, p_ref, *routing, 2 * s * n_i, n_i)
        acc_ref[...] += jnp.dot(vtb_ref[...], p_ref[...], preferred_element_type=F32)

    @pl.when(s == last)
    def _():
        y_ref[0] = x_ref[0] + g_ref[0] * acc_ref[...].T


def _peer_experts(x, gate, h2t, u_tab, vt_tab, layer, r2, e2, c1, e1, tb, eb):
    nb, s, d = x.shape
    nblk = s // tb
    n_exp = u_tab.shape[1]
    n_i = eb // PEER_NKEYS
    n_pairs = n_exp // (2 * eb)
    one = pl.Buffered(1)
    t3 = pl.BlockSpec((PEER_HEADS, PEER_NKEYS, tb), lambda b, i, e: (0, 0, b * nblk + i), pipeline_mode=one)
    k3 = pl.BlockSpec((PEER_NKEYS, PEER_HEADS, tb), lambda b, i, e: (0, 0, b * nblk + i), pipeline_mode=one)
    if gate.shape[1] == 1:
        gspec = pl.BlockSpec((1, 1, d), lambda b, i, e: (b, 0, 0))
    else:
        gspec = pl.BlockSpec((1, tb, d), lambda b, i, e: (b, i, 0), pipeline_mode=one)
    return pl.pallas_call(
        functools.partial(_peer_body, n_i=n_i),
        grid=(nb, nblk, n_pairs + 1),
        in_specs=[pl.BlockSpec((d, tb), lambda b, i, e: (0, b * nblk + i), pipeline_mode=one),
                  pl.BlockSpec((None, 2 * eb, d), lambda b, i, e: (layer, jnp.minimum(e, n_pairs - 1), 0)),
                  pl.BlockSpec((None, d, eb), lambda b, i, e: (layer, 0, jnp.maximum(2 * e - 1, 0))),
                  pl.BlockSpec((None, d, eb), lambda b, i, e: (layer, 0, jnp.minimum(2 * e, 2 * n_pairs - 1))),
                  t3, t3, k3, k3,
                  pl.BlockSpec((1, tb, d), lambda b, i, e: (b, i, 0), pipeline_mode=one), gspec],
        out_specs=pl.BlockSpec((1, tb, d), lambda b, i, e: (b, i, 0)),
        out_shape=jax.ShapeDtypeStruct((nb, s, d), F32),
        scratch_shapes=[pltpu.VMEM((d, tb), F32), pltpu.VMEM((eb, tb), F32), pltpu.VMEM((eb, tb), F32),
                        pltpu.VMEM((eb, tb), BF16)],
        compiler_params=_cparams(("arbitrary", "arbitrary", "arbitrary")),
        name="peer_experts",
    )(h2t, u_tab, vt_tab, vt_tab, r2, e2, c1, e1, x, gate)


def _peer(x, shift, scale, gate, norm_g, wq_t, subkeys, u_tab, vt_tab, layer, tb, eb):
    h2t, r2, e2, c1, e1 = _peer_route(x, shift, scale, norm_g, wq_t, subkeys, tb)
    c1 = jnp.transpose(c1, (1, 0, 2))
    e1 = jnp.transpose(e1, (1, 0, 2))
    return _peer_experts(x, gate, h2t, u_tab, vt_tab, layer, r2, e2, c1, e1, tb, eb)


def _kvq_body(x_ref, ksh_ref, ksc_ref, msh_ref, msc_ref, kng_ref, mng_ref, wkv_ref, wq_ref,
              kg_ref, qg_ref, k_ref, v_ref, kb_ref, vb_ref, q_ref):
    x = x_ref[0]
    d = x.shape[1]
    ms = jnp.mean(x * x, axis=-1, keepdims=True)
    xn = x * lax.rsqrt(ms + EPS)
    hk = (xn * kng_ref[...]) * (1.0 + ksc_ref[0]) + ksh_ref[0]
    kv = jnp.dot(hk.astype(BF16), wkv_ref[...], preferred_element_type=F32)
    k = _head_rms(kv[:, :d], kg_ref[...])
    v = kv[:, d:]
    k_ref[0] = k
    v_ref[0] = v
    kb_ref[0] = k.astype(BF16)
    vb_ref[0] = v.astype(BF16)
    hq = (xn * mng_ref[...]) * (1.0 + msc_ref[0]) + msh_ref[0]
    q = jnp.dot(hq.astype(BF16), wq_ref[...], preferred_element_type=F32)
    q_ref[0] = _head_rms(q, qg_ref[...]).astype(BF16)


def _kvq(x, k_shift, k_scale, m_shift, m_scale, kv_norm_g, mix_norm_g, w_kv, w_q, k_gain, q_gain, rows):
    nb, s, d = x.shape
    xspec = pl.BlockSpec((1, rows, d), lambda b, i: (b, i, 0))
    return pl.pallas_call(
        _kvq_body,
        grid=(nb, s // rows),
        in_specs=[xspec, _mod_spec(k_shift, rows), _mod_spec(k_scale, rows), _mod_spec(m_shift, rows),
                  _mod_spec(m_scale, rows), _const_spec((1, d)), _const_spec((1, d)),
                  _const_spec(w_kv.shape), _const_spec(w_q.shape), _const_spec((1, d)), _const_spec((1, d))],
        out_specs=[xspec] * 5,
        out_shape=[jax.ShapeDtypeStruct((nb, s, d), F32), jax.ShapeDtypeStruct((nb, s, d), F32),
                   jax.ShapeDtypeStruct((nb, s, d), BF16), jax.ShapeDtypeStruct((nb, s, d), BF16),
                   jax.ShapeDtypeStruct((nb, s, d), BF16)],
        compiler_params=_cparams(("arbitrary", "arbitrary")),
        name="kv_q_proj",
    )(x, k_shift, k_scale, m_shift, m_scale, kv_norm_g.reshape(1, d), mix_norm_g.reshape(1, d),
      w_kv, w_q, k_gain, q_gain)


def _softplus(z):
    return jnp.maximum(z, 0.0) + jnp.log(1.0 + jnp.exp(-jnp.abs(z)))


def _split_bf16(x):
    hi = x.astype(BF16)
    lo = (x - hi.astype(F32)).astype(BF16)
    return hi, lo


def _sb_prompt_body(q_ref, k_ref, v_ref, bias_ref, tri_ref, o_ref, *, blk, heads, scale):
    qi = pl.program_id(2)
    tri = tri_ref[...]
    hsl = [slice(h * HEAD_DIM, (h + 1) * HEAD_DIM) for h in range(heads)]
    qs = [q_ref[0, :, hsl[h]] for h in range(heads)]
    biases = [bias_ref[0, h:h + 1, :] for h in range(heads)]
    hr = range(heads)

    def tiles(kb, carry, mask):
        ks = pl.multiple_of(kb * blk, blk)
        z = [lax.dot_general(qs[h], k_ref[0, pl.ds(ks, blk), hsl[h]], (((1,), (1,)), ((), ())),
                             preferred_element_type=F32) * scale + biases[h] for h in hr]
        sp, lm, surv, pv = [None] * heads, [None] * heads, [None] * heads, [None] * heads
        for h in hr:
            sp[h] = _softplus(z[h])
            lm[h] = -sp[h] if mask is None else jnp.where(mask, -sp[h], 0.0)
            hi, lo = _split_bf16(lm[h])
            surv[h] = jnp.dot(jnp.concatenate([hi, lo], axis=1), tri, preferred_element_type=F32)
        for h in hr:
            a = jnp.exp(z[h] - sp[h] + (surv[h] + carry[h][1]))
            if mask is not None:
                a = jnp.where(mask, a, 0.0)
            pv[h] = jnp.dot(a.astype(BF16), v_ref[0, pl.ds(ks, blk), hsl[h]], preferred_element_type=F32)
        return tuple((carry[h][0] + pv[h], carry[h][1] + jnp.sum(lm[h], axis=1, keepdims=True)) for h in hr)

    row = lax.broadcasted_iota(jnp.int32, (blk, blk), 0)
    col = lax.broadcasted_iota(jnp.int32, (blk, blk), 1)
    init = tuple((jnp.zeros((blk, HEAD_DIM), F32), jnp.zeros((blk, 1), F32)) for _ in range(heads))
    carry = tiles(qi, init, col < row)
    carry = lax.fori_loop(1, qi + 1, lambda j, c: tiles(qi - j, c, None), carry)
    for h in range(heads):
        o_ref[0, :, hsl[h]] = carry[h][0].astype(o_ref.dtype)


def _suffix_matrix(n):
    m = (np.arange(n)[:, None] > np.arange(n)[None, :]).astype(np.float32)
    return jnp.asarray(np.concatenate([m, m], axis=0), dtype=BF16)


def _sb_prompt(q, k, v, logit_bias, blk, heads):
    nb, s, d = q.shape
    bias = jnp.broadcast_to(logit_bias.astype(F32).reshape(N_HEADS // heads, heads, 1),
                            (N_HEADS // heads, heads, blk))
    qspec = pl.BlockSpec((1, blk, heads * HEAD_DIM), lambda b, h, i: (b, i, h))
    kvspec = pl.BlockSpec((1, s, heads * HEAD_DIM), lambda b, h, i: (b, 0, h))
    return pl.pallas_call(
        functools.partial(_sb_prompt_body, blk=blk, heads=heads, scale=np.float32(HEAD_DIM ** -0.5)),
        grid=(nb, N_HEADS // heads, s // blk),
        in_specs=[qspec, kvspec, kvspec, pl.BlockSpec((1, heads, blk), lambda b, h, i: (h, 0, 0)),
                  pl.BlockSpec((2 * blk, blk), lambda b, h, i: (0, 0))],
        out_specs=qspec,
        out_shape=jax.ShapeDtypeStruct((nb, s, d), BF16),
        compiler_params=_cparams(("arbitrary", "arbitrary", "arbitrary")),
        name="sb_prompt",
    )(q, k, v, bias, _suffix_matrix(blk))


def _sb_paged_body(pt_ref, *refs, pages_per_step, n_q, scale):
    kp = refs[:pages_per_step]
    vp = refs[pages_per_step:2 * pages_per_step]
    wq_ref, bias_ref, knew_ref, vnew_ref, tri_ref, o_ref, acc_ref, run_ref = refs[2 * pages_per_step:]
    step = pl.program_id(1)
    wq = wq_ref[0]
    bias = bias_ref[...]

    def heads_to_lanes(ref):
        n_pos = ref.shape[1] // N_HEADS
        return jnp.concatenate(
            [ref[0, pl.ds(h, n_pos, stride=N_HEADS), :].astype(BF16) for h in range(N_HEADS)], axis=1)

    def block(kblk, vblk, mask):
        z = jnp.dot(kblk, wq, preferred_element_type=F32) * scale + bias
        sp = _softplus(z)
        lm = -sp if mask is None else jnp.where(mask, -sp, 0.0)
        hi, lo = _split_bf16(lm)
        surv = jnp.dot(tri_ref[...], jnp.concatenate([hi, lo], axis=0),
                       preferred_element_type=F32) + run_ref[...]
        a = jnp.exp(z - sp + surv)
        if mask is not None:
            a = jnp.where(mask, a, 0.0)
        acc_ref[...] += jnp.dot(a.T.astype(BF16), vblk, preferred_element_type=F32)
        run_ref[...] += jnp.sum(lm, axis=0, keepdims=True)

    @pl.when(step == 0)
    def _():
        acc_ref[...] = jnp.zeros_like(acc_ref)
        run_ref[...] = jnp.zeros_like(run_ref)
        shape = (knew_ref.shape[1], wq.shape[1])
        kpos = lax.broadcasted_iota(jnp.int32, shape, 0)
        qpos = lax.broadcasted_iota(jnp.int32, shape, 1) % n_q
        block(knew_ref[0].astype(BF16), vnew_ref[0].astype(BF16), kpos < qpos)

    for i in range(pages_per_step):
        block(heads_to_lanes(kp[i]), heads_to_lanes(vp[i]), None)

    @pl.when(step == pl.num_programs(1) - 1)
    def _():
        o_ref[0] = acc_ref[...]


def _suffix_matrix_t(n):
    m = (np.arange(n)[None, :] > np.arange(n)[:, None]).astype(np.float32)
    return jnp.asarray(np.concatenate([m, m], axis=1), dtype=BF16)


def _sb_paged(q, k_new, v_new, cache_k, cache_v, page_table, logit_bias, pages_per_step):
    nb, n_q, d = q.shape
    n_pool, page = cache_k.shape[:2]
    n_pages = page_table.shape[1]
    ck = cache_k.reshape(n_pool, page * N_HEADS, HEAD_DIM)
    cv = cache_v.reshape(n_pool, page * N_HEADS, HEAD_DIM)
    ncol = LANES
    assert N_HEADS * n_q <= ncol and n_q <= page
    qh = q.reshape(nb, n_q, N_HEADS, HEAD_DIM)
    eye = jnp.eye(N_HEADS, dtype=BF16)
    wq = jnp.einsum('bqhd,hg->bhdgq', qh, eye).reshape(nb, d, N_HEADS * n_q)
    wq = jnp.pad(wq, ((0, 0), (0, 0), (0, ncol - N_HEADS * n_q)))
    bias = jnp.pad(jnp.repeat(logit_bias.astype(F32), n_q), (0, ncol - N_HEADS * n_q)).reshape(1, ncol)
    knew = jnp.pad(k_new, ((0, 0), (0, page - n_q), (0, 0)))
    vnew = jnp.pad(v_new, ((0, 0), (0, page - n_q), (0, 0)))
    pt = page_table.reshape(-1).astype(jnp.int32)

    def page_map(i):
        return lambda b, s, pt_ref: (pt_ref[b * n_pages + n_pages - 1 - (s * pages_per_step + i)], 0, 0)

    page_specs = [pl.BlockSpec((1, page * N_HEADS, HEAD_DIM), page_map(i)) for i in range(pages_per_step)]
    grid_spec = pltpu.PrefetchScalarGridSpec(
        num_scalar_prefetch=1,
        grid=(nb, n_pages // pages_per_step),
        in_specs=page_specs + page_specs + [
            pl.BlockSpec((1, d, ncol), lambda b, s, pt_ref: (b, 0, 0)),
            pl.BlockSpec((1, ncol), lambda b, s, pt_ref: (0, 0)),
            pl.BlockSpec((1, page, d), lambda b, s, pt_ref: (b, 0, 0)),
            pl.BlockSpec((1, page, d), lambda b, s, pt_ref: (b, 0, 0)),
            pl.BlockSpec((page, 2 * page), lambda b, s, pt_ref: (0, 0))],
        out_specs=pl.BlockSpec((1, ncol, d), lambda b, s, pt_ref: (b, 0, 0)),
        scratch_shapes=[pltpu.VMEM((ncol, d), F32), pltpu.VMEM((1, ncol), F32)])
    return pl.pallas_call(
        functools.partial(_sb_paged_body, pages_per_step=pages_per_step, n_q=n_q,
                          scale=np.float32(HEAD_DIM ** -0.5)),
        grid_spec=grid_spec,
        out_shape=jax.ShapeDtypeStruct((nb, ncol, d), F32),
        compiler_params=_cparams(("arbitrary", "arbitrary")),
        name="sb_paged",
    )(pt, *([ck] * pages_per_step), *([cv] * pages_per_step), wq, bias, knew, vnew,
      _suffix_matrix_t(page))


def _oproj_body(o_ref, x_ref, gt_ref, w_ref, y_ref):
    mix = jnp.dot(o_ref[0], w_ref[...], preferred_element_type=F32)
    y_ref[0] = x_ref[0] + gt_ref[0] * mix


def _oproj(o, x, gate, w_o, rows):
    nb, s, d = x.shape
    xspec = pl.BlockSpec((1, rows, d), lambda b, i: (b, i, 0))
    return pl.pallas_call(
        _oproj_body,
        grid=(nb, s // rows),
        in_specs=[xspec, xspec, _mod_spec(gate, rows), _const_spec(w_o.shape)],
        out_specs=xspec,
        out_shape=jax.ShapeDtypeStruct((nb, s, d), F32),
        compiler_params=_cparams(("arbitrary", "arbitrary")),
        name="attn_out_proj",
    )(o, x, gate, w_o)


def _mix_weights_prompt(w_s, b_s, width):
    causal = jnp.tril(jnp.ones((CHUNK, CHUNK), dtype=bool))
    wmix = jnp.where(causal[None], w_s, 0.0).astype(BF16)
    bmix = jnp.repeat(b_s.T, width // GROUPS, axis=1)
    return wmix, bmix


def _mix_weights_sample(w_s, b_s, width, n_seq, n_tok):
    causal = jnp.tril(jnp.ones((n_tok, n_tok), dtype=bool))
    small = jnp.where(causal[None], w_s[:, :n_tok, :n_tok], 0.0)
    eye = jnp.eye(n_seq, dtype=F32)
    blockdiag = jnp.einsum('ab,gts->gatbs', eye, small).reshape(GROUPS, n_seq * n_tok, n_seq * n_tok)
    padn = SAMPLE_ROWS - n_seq * n_tok
    wmix = jnp.pad(blockdiag, ((0, 0), (0, padn), (0, padn))).astype(BF16)
    brow = jnp.tile(b_s.T[:n_tok], (n_seq, 1))
    bmix = jnp.repeat(jnp.pad(brow, ((0, padn), (0, 0))), width // GROUPS, axis=1)
    return wmix, bmix


def kernel(x_prompt, x_sample, cache_k, cache_v, page_table, c_prompt, c_sample, mod_w, mod_b, norm_mix_g, norm_ffn_g, a_w_in, a_b_in, a_v_norm_g, a_w_s, a_b_s, a_w_out, kv_mod_w, kv_mod_b, kv_norm_g, kv_w, k_norm_g, b_w_q, b_q_norm_g, b_logit_bias, b_w_o, peer_w_q, peer_subkeys, peer_u, peer_v):
    nbp, seq, d = x_prompt.shape
    nbs, n_tok, _ = x_sample.shape
    n_samp = nbs * n_tok
    width = a_w_out.shape[1]

    c_all = jnp.concatenate([c_prompt, c_sample], axis=0)
    c_all = jnp.pad(c_all, ((0, BF16_ROWS - c_all.shape[0]), (0, 0)))
    mods = _adaln(c_all, mod_w, mod_b)
    kvmod = _adaln(c_all, kv_mod_w[None], kv_mod_b[None])[0]

    def split_mod(m, n):
        parts = jnp.split(m, n, axis=-1)
        prm = [p[:nbp, None, :] for p in parts]
        smp = [jnp.pad(jnp.repeat(p[nbp:nbp + nbs], n_tok, axis=0), ((0, SAMPLE_ROWS - n_samp), (0, 0)))[None]
               for p in parts]
        return prm, smp

    mod_p, mod_s = zip(*[split_mod(mods[l], 6) for l in range(mods.shape[0])])
    kvmod_p, kvmod_s = split_mod(kvmod, 2)

    w_in = a_w_in[0].astype(BF16)
    w_out = a_w_out[0].astype(BF16)
    wq_t = [jnp.transpose(peer_w_q[l]).astype(BF16) for l in range(2)]
    subk = [peer_subkeys[l].astype(BF16) for l in range(2)]
    u_tab = peer_u.astype(BF16)
    vt_tab = jnp.transpose(peer_v, (0, 2, 1)).astype(BF16)
    w_kv = kv_w.astype(BF16)
    w_q1 = b_w_q[0].astype(BF16)
    w_o1 = b_w_o[0].astype(BF16)
    k_gain = jnp.tile(k_norm_g, N_HEADS).reshape(1, d)
    q_gain = jnp.tile(b_q_norm_g[0], N_HEADS).reshape(1, d)
    wmix_p, bmix_p = _mix_weights_prompt(a_w_s[0], a_b_s[0], width)
    wmix_s, bmix_s = _mix_weights_sample(a_w_s[0], a_b_s[0], width, nbs, n_tok)

    xs = jnp.pad(x_sample.reshape(1, n_samp, d), ((0, 0), (0, SAMPLE_ROWS - n_samp), (0, 0)))

    def trunk(x, mod, kvm, wmix, bmix, rows, tb, attend):
        sh_m, sc_m, g_m, sh_f, sc_f, g_f = mod[0]
        x1, v_rows = _mixer_a(x, sh_m, sc_m, g_m, norm_mix_g[0], w_in, a_b_in[0], a_v_norm_g[0],
                              wmix, bmix, w_out, rows)
        x2 = _peer(x1, sh_f, sc_f, g_f, norm_ffn_g[0], wq_t[0], subk[0], u_tab, vt_tab, 0, tb, 512)
        sh_m, sc_m, g_m, sh_f, sc_f, g_f = mod[1]
        k, v, kb, vb, q = _kvq(x2, kvm[0], kvm[1], sh_m, sc_m, kv_norm_g, norm_mix_g[1], w_kv, w_q1,
                               k_gain, q_gain, rows)
        o = attend(q, k, v, kb, vb)
        x3 = _oproj(o, x2, g_m, w_o1, rows)
        y = _peer(x3, sh_f, sc_f, g_f, norm_ffn_g[1], wq_t[1], subk[1], u_tab, vt_tab, 1, tb, 512)
        return y, k, v, v_rows

    def attend_prompt(q, k, v, kb, vb):
        return _sb_prompt(q, kb, vb, b_logit_bias[0], 256, 4)

    def attend_sample(q, k, v, kb, vb):
        qs = q[0, :n_samp].reshape(nbs, n_tok, d)
        ks = k[0, :n_samp].reshape(nbs, n_tok, d)
        vs = v[0, :n_samp].reshape(nbs, n_tok, d)
        full = _sb_paged(qs, ks, vs, cache_k, cache_v, page_table, b_logit_bias[0], 4)
        full = full[:, :N_HEADS * n_tok].reshape(nbs, N_HEADS, n_tok, N_HEADS, HEAD_DIM)
        idx = jnp.arange(N_HEADS)
        o = full[:, idx, :, idx, :]
        o = jnp.transpose(o, (1, 2, 0, 3)).reshape(1, n_samp, d)
        return jnp.pad(o, ((0, 0), (0, SAMPLE_ROWS - n_samp), (0, 0))).astype(BF16)

    y_p, k_p, v_p, _ = trunk(x_prompt, mod_p, kvmod_p, wmix_p, bmix_p, 256, 512, attend_prompt)
    y_s, k_s, v_s, gv_s = trunk(xs, mod_s, kvmod_s, wmix_s, bmix_s, SAMPLE_ROWS, SAMPLE_ROWS, attend_sample)

    def samp(a, shape):
        return a[0, :n_samp].reshape(shape)

    return (y_p,
            samp(y_s, (nbs, n_tok, d)),
            k_p.reshape(nbp, seq, N_HEADS, HEAD_DIM),
            v_p.reshape(nbp, seq, N_HEADS, HEAD_DIM),
            samp(k_s, (nbs, n_tok, N_HEADS, HEAD_DIM)),
            samp(v_s, (nbs, n_tok, N_HEADS, HEAD_DIM)),
            samp(gv_s, (1, nbs, n_tok, width)))
```

```python
import functools
import math

import jax
import jax.numpy as jnp
import numpy as np
from jax import lax
from jax.experimental import pallas as pl
from jax.experimental.pallas import tpu as pltpu

F32 = jnp.float32
BF16 = jnp.bfloat16
EPS = 1e-6

LANES = 128
SUBLANES = 8
BF16_ROWS = 16
VMEM_LIMIT = 56 * 1024 * 1024

CHUNK = 128
GROUPS = 16
N_HEADS = 16
HEAD_DIM = 128
PEER_HEADS = 8
PEER_NKEYS = 128
PEER_TOPK = 16
SAMPLE_ROWS = 128


def _cparams(sem):
    return pltpu.CompilerParams(dimension_semantics=sem, vmem_limit_bytes=VMEM_LIMIT)


def _const_spec(shape):
    nd = len(shape)
    return pl.BlockSpec(shape, lambda *_: (0,) * nd, pipeline_mode=pl.Buffered(1))


def _rms(x, g):
    ms = jnp.mean(x * x, axis=-1, keepdims=True)
    return x * lax.rsqrt(ms + EPS) * g


def _gelu(x):
    return 0.5 * x * (1.0 + lax.erf(x * np.float32(math.sqrt(0.5))))


def _head_rms(x, g):
    cols = []
    for h in range(x.shape[1] // HEAD_DIM):
        seg = x[:, h * HEAD_DIM:(h + 1) * HEAD_DIM]
        ms = jnp.mean(seg * seg, axis=-1, keepdims=True)
        cols.append(seg * lax.rsqrt(ms + EPS))
    return jnp.concatenate(cols, axis=1) * g


def _adaln_body(c_ref, w_ref, b_ref, o_ref):
    c = c_ref[...]
    s = c / (1.0 + jnp.exp(-c))
    o_ref[0] = jnp.dot(s.astype(BF16), w_ref[0].astype(BF16), preferred_element_type=F32) + b_ref[0]


def _adaln(c, w, b):
    n_l, d, n = w.shape
    m = c.shape[0]
    tn = 1024
    return pl.pallas_call(
        _adaln_body,
        grid=(n_l, n // tn),
        in_specs=[pl.BlockSpec((m, d), lambda l, j: (0, 0)),
                  pl.BlockSpec((1, d, tn), lambda l, j: (l, 0, j)),
                  pl.BlockSpec((1, 1, tn), lambda l, j: (l, 0, j))],
        out_specs=pl.BlockSpec((1, m, tn), lambda l, j: (l, 0, j)),
        out_shape=jax.ShapeDtypeStruct((n_l, m, n), F32),
        compiler_params=_cparams(("arbitrary", "arbitrary")),
        name="adaln",
    )(c, w, b.reshape(n_l, 1, n))


def _mod_spec(mod, rows):
    d = mod.shape[-1]
    if mod.shape[1] == 1:
        return pl.BlockSpec((1, 1, d), lambda b, i: (b, 0, 0))
    return pl.BlockSpec((1, rows, d), lambda b, i: (b, i, 0))


def _mixer_a_body(x_ref, sh_ref, sc_ref, gt_ref, ng_ref, win_ref, bin_ref, vg_ref, wmix_ref,
                  bmix_ref, wout_ref, x1_ref, v_ref, *, chunk):
    x = x_ref[0]
    rows = x.shape[0]
    h = _rms(x, ng_ref[...]) * (1.0 + sc_ref[0]) + sh_ref[0]
    z = jnp.dot(h.astype(BF16), win_ref[...], preferred_element_type=F32) + bin_ref[...]
    z = _gelu(z)
    width = z.shape[1] // 2
    u = z[:, :width]
    v = _rms(z[:, width:], vg_ref[...])
    v_ref[0] = v
    vb = v.astype(BF16)
    gd = width // GROUPS
    blocks = []
    for c in range(rows // chunk):
        cols = [jnp.dot(wmix_ref[g], vb[c * chunk:(c + 1) * chunk, g * gd:(g + 1) * gd],
                        preferred_element_type=F32) for g in range(GROUPS)]
        blocks.append(jnp.concatenate(cols, axis=1) + bmix_ref[...])
    mixed = blocks[0] if len(blocks) == 1 else jnp.concatenate(blocks, axis=0)
    um = (u * mixed).astype(BF16)
    mix = jnp.dot(um, wout_ref[...], preferred_element_type=F32)
    x1_ref[0] = x + gt_ref[0] * mix


def _mixer_a(x, shift, scale, gate, norm_g, w_in, b_in, v_g, wmix, bmix, w_out, rows):
    nb, s, d = x.shape
    width = w_out.shape[0]
    chunk = wmix.shape[1]
    xspec = pl.BlockSpec((1, rows, d), lambda b, i: (b, i, 0))
    return pl.pallas_call(
        functools.partial(_mixer_a_body, chunk=chunk),
        grid=(nb, s // rows),
        in_specs=[xspec, _mod_spec(shift, rows), _mod_spec(scale, rows), _mod_spec(gate, rows),
                  _const_spec((1, d)), _const_spec(w_in.shape), _const_spec((1, 2 * width)),
                  _const_spec((1, width)), _const_spec(wmix.shape), _const_spec(bmix.shape),
                  _const_spec(w_out.shape)],
        out_specs=[xspec, pl.BlockSpec((1, rows, width), lambda b, i: (b, i, 0))],
        out_shape=[jax.ShapeDtypeStruct((nb, s, d), F32), jax.ShapeDtypeStruct((nb, s, width), F32)],
        compiler_params=_cparams(("arbitrary", "arbitrary")),
        name="gmlp_mixer",
    )(x, shift, scale, gate, norm_g.reshape(1, d), w_in, b_in.reshape(1, -1), v_g.reshape(1, -1),
      wmix, bmix, w_out)


def _sort_network(n):
    pairs = []
    p = 1
    while p < n:
        k = p
        while k >= 1:
            for j in range(k % p, n - k, 2 * k):
                for i in range(min(k, n - j - k)):
                    if (i + j) // (2 * p) == (i + j + k) // (2 * p):
                        pairs.append((i + j, i + j + k))
            k //= 2
        p *= 2
    return pairs


_SORT16 = _sort_network(PEER_TOPK)
_BITONIC16 = [(i, i | d) for d in (8, 4, 2, 1) for i in range(PEER_TOPK) if not i & d]


def _apply_network(xs, pairs):
    xs = list(xs)
    for i, j in pairs:
        a, b = xs[i], xs[j]
        xs[i] = jnp.maximum(a, b)
        xs[j] = jnp.minimum(a, b)
    return xs


def _merge_top16(xs, ys):
    zs = [jnp.maximum(xs[r], ys[PEER_TOPK - 1 - r]) for r in range(PEER_TOPK)]
    return _apply_network(zs, _BITONIC16)


def _top16_of_keys(s):
    slabs = [s[SUBLANES * g:SUBLANES * (g + 1), :] for g in range(PEER_NKEYS // SUBLANES)]
    xs = _apply_network(slabs, _SORT16)
    for shift in (4, 2, 1):
        ys = [pltpu.roll(x, shift, axis=0) for x in xs]
        xs = _merge_top16(xs, ys)
    return xs


_CAND = [(r, c) for r in range(PEER_TOPK) for c in range(PEER_TOPK) if (r + 1) * (c + 1) <= PEER_TOPK]


def _route_body(x_ref, sh_ref, sc_ref, ng_ref, wq_ref, sk_ref,
                h2t_ref, r2_ref, e2_ref, c1_ref, e1_ref, s_ref):
    x = x_ref[0]
    h = _rms(x, ng_ref[...]) * (1.0 + sc_ref[0]) + sh_ref[0]
    ht = h.T.astype(BF16)
    h2t_ref[...] = ht
    qt = jnp.dot(wq_ref[...], ht, preferred_element_type=F32).astype(BF16)
    for hp in range(2 * PEER_HEADS):
        s_ref[hp] = jnp.dot(sk_ref[hp % 2], qt[hp * PEER_NKEYS:(hp + 1) * PEER_NKEYS, :],
                            preferred_element_type=F32)
    tb = x.shape[0]
    neg = jnp.full((SUBLANES, LANES), -jnp.inf, F32)
    sub = lax.broadcasted_iota(jnp.int32, (SUBLANES, LANES), 0)

    def chunk(lc, carry):
        lanes = pl.ds(pl.multiple_of(lc * LANES, LANES), LANES)
        top = [[neg] * PEER_TOPK, [neg] * PEER_TOPK]
        for hp in range(2 * PEER_HEADS):
            hh, p = hp // 2, hp % 2
            xs = _top16_of_keys(s_ref[hp, :, lanes])
            top[p] = [jnp.where(sub == hh, xs[r], top[p][r]) for r in range(PEER_TOPK)]
        a, b = top
        cand = {rc: a[rc[0]] + b[rc[1]] for rc in _CAND}
        g1 = [cand[(0, c)] for c in range(PEER_TOPK)]
        rest = [cand[rc] for rc in _CAND if rc[0] > 0]
        rest = rest + [neg] * (3 * PEER_TOPK - len(rest))
        gs = [_apply_network(rest[PEER_TOPK * k:PEER_TOPK * (k + 1)], _SORT16) for k in range(3)]
        m1 = _merge_top16(g1, gs[0])
        m2 = _merge_top16(gs[1], gs[2])
        zs = [jnp.maximum(m1[r], m2[PEER_TOPK - 1 - r]) for r in range(PEER_TOPK)]
        tau = functools.reduce(jnp.minimum, zs)
        top_sum = a[0] + b[0]
        zsum = jnp.zeros((SUBLANES, LANES), F32)
        for rc in _CAND:
            zsum = zsum + jnp.where(cand[rc] >= tau, jnp.exp(cand[rc] - top_sum), 0.0)
        zinv = 1.0 / zsum
        for hh in range(PEER_HEADS):
            def row(v):
                return jnp.broadcast_to(v[hh:hh + 1, :], (SUBLANES, LANES))
            bh = [row(b[c]) for c in range(PEER_TOPK)]
            tau_h, zinv_h, a0_h = row(tau), row(zinv), row(a[0])
            for gp in range(PEER_NKEYS // BF16_ROWS):
                r2s, e2s, c1s, e1s = [], [], [], []
                for g in (2 * gp, 2 * gp + 1):
                    rows = slice(SUBLANES * g, SUBLANES * (g + 1))
                    s1 = s_ref[2 * hh, rows, lanes]
                    s2 = s_ref[2 * hh + 1, rows, lanes]
                    r2 = jnp.zeros((SUBLANES, LANES), F32)
                    c1 = jnp.zeros((SUBLANES, LANES), F32)
                    for c in range(PEER_TOPK):
                        r2 = r2 + jnp.where(bh[c] > s2, 1.0, 0.0)
                        c1 = c1 + jnp.where(s1 + bh[c] >= tau_h, 1.0, 0.0)
                    r2s.append(r2)
                    c1s.append(c1)
                    e2s.append(jnp.exp(s2 - bh[0]))
                    e1s.append(jnp.exp(s1 - a0_h) * zinv_h)
                rows16 = slice(BF16_ROWS * gp, BF16_ROWS * (gp + 1))
                r2_ref[hh, rows16, lanes] = jnp.concatenate(r2s, axis=0).astype(BF16)
                e2_ref[hh, rows16, lanes] = jnp.concatenate(e2s, axis=0).astype(BF16)
                c1_ref[hh, rows16, lanes] = jnp.concatenate(c1s, axis=0)
                e1_ref[hh, rows16, lanes] = jnp.concatenate(e1s, axis=0)
        return carry

    lax.fori_loop(0, tb // LANES, chunk, 0)


def _peer_route(x, shift, scale, norm_g, wq_t, subkeys, tb):
    nb, s, d = x.shape
    nblk = s // tb
    nt = nb * nblk
    tspec3 = pl.BlockSpec((None, PEER_HEADS, PEER_NKEYS, tb), lambda b, i: (b * nblk + i, 0, 0, 0))
    return pl.pallas_call(
        _route_body,
        grid=(nb, nblk),
        in_specs=[pl.BlockSpec((1, tb, d), lambda b, i: (b, i, 0)), _mod_spec(shift, tb), _mod_spec(scale, tb),
                  _const_spec((1, d)), _const_spec(wq_t.shape), _const_spec(subkeys.shape)],
        out_specs=[pl.BlockSpec((None, d, tb), lambda b, i: (b * nblk + i, 0, 0)), tspec3, tspec3, tspec3, tspec3],
        out_shape=[jax.ShapeDtypeStruct((nt, d, tb), BF16),
                   jax.ShapeDtypeStruct((nt, PEER_HEADS, PEER_NKEYS, tb), BF16),
                   jax.ShapeDtypeStruct((nt, PEER_HEADS, PEER_NKEYS, tb), BF16),
                   jax.ShapeDtypeStruct((nt, PEER_HEADS, PEER_NKEYS, tb), F32),
                   jax.ShapeDtypeStruct((nt, PEER_HEADS, PEER_NKEYS, tb), F32)],
        scratch_shapes=[pltpu.VMEM((2 * PEER_HEADS, PEER_NKEYS, tb), F32)],
        compiler_params=_cparams(("arbitrary", "arbitrary")),
        name="peer_route",
    )(x, shift, scale, norm_g.reshape(1, d), wq_t, subkeys)


def _peer_gate_stage(act_ref, p_ref, r2_ref, e2_ref, c1_ref, e1_ref, key0, n_i):
    tb = act_ref.shape[1]
    for il in range(n_i):
        c1i = c1_ref[key0 + il]
        e1i = e1_ref[key0 + il]
        for lc in range(tb // LANES):
            lanes = slice(lc * LANES, (lc + 1) * LANES)
            gate = [jnp.zeros((BF16_ROWS, LANES), BF16)] * (PEER_NKEYS // BF16_ROWS)
            for hh in range(PEER_HEADS):
                c1b = jnp.broadcast_to(c1i[hh:hh + 1, lanes], (BF16_ROWS, LANES)).astype(BF16)
                e1b = jnp.broadcast_to(e1i[hh:hh + 1, lanes], (BF16_ROWS, LANES)).astype(BF16)
                for rb in range(PEER_NKEYS // BF16_ROWS):
                    rows = slice(rb * BF16_ROWS, (rb + 1) * BF16_ROWS)
                    r2 = r2_ref[hh, rows, lanes]
                    e2 = e2_ref[hh, rows, lanes]
                    gate[rb] = gate[rb] + jnp.where(r2 < c1b, e2, jnp.zeros_like(e2)) * e1b
            for rb in range(PEER_NKEYS // BF16_ROWS):
                rows = slice(il * PEER_NKEYS + rb * BF16_ROWS, il * PEER_NKEYS + (rb + 1) * BF16_ROWS)
                a = act_ref[rows, lanes]
                p_ref[rows, lanes] = (_gelu(a) * gate[rb].astype(F32)).astype(BF16)


def _peer_body(h2t_ref, u_ref, vta_ref, vtb_ref, r2_ref, e2_ref, c1_ref, e1_ref, x_ref, g_ref, y_ref,
               acc_ref, acta_ref, actb_ref, p_ref, *, n_i):
    s = pl.program_id(2)
    last = pl.num_programs(2) - 1
    eb = acta_ref.shape[0]
    routing = (r2_ref, e2_ref, c1_ref, e1_ref)

    @pl.when(s == 0)
    def _():
        acc_ref[...] = jnp.zeros_like(acc_ref)
        actb_ref[...] = jnp.zeros_like(actb_ref)

    acta_ref[...] = jnp.dot(u_ref[:eb, :], h2t_ref[...], preferred_element_type=F32)
    _peer_gate_stage(actb_ref, p_ref, *routing, jnp.maximum(2 * s - 1, 0) * n_i, n_i)
    acc_ref[...] += jnp.dot(vta_ref[...], p_ref[...], preferred_element_type=F32)

    @pl.when(s < last)
    def _():
        actb_ref[...] = jnp.dot(u_ref[eb:, :], h2t_ref[...], preferred_element_type=F32)
        _peer_gate_stage(acta_ref, p_ref, *routing, 2 * s * n_i, n_i)
        acc_ref[...] += jnp.dot(vtb_ref[...], p_ref[...], preferred_element_type=F32)

    @pl.when(s == last)
    def _():
        y_ref[0] = x_ref[0] + g_ref[0] * acc_ref[...].T


def _peer_experts(x, gate, h2t, u_tab, vt_tab, layer, r2, e2, c1, e1, tb, eb):
    nb, s, d = x.shape
    nblk = s // tb
    n_exp = u_tab.shape[1]
    n_i = eb // PEER_NKEYS
    n_pairs = n_exp // (2 * eb)
    assert vt_tab.shape[1:] == (2 * n_pairs, d, eb)
    one = pl.Buffered(1)
    t3 = pl.BlockSpec((None, PEER_HEADS, PEER_NKEYS, tb), lambda b, i, e: (b * nblk + i, 0, 0, 0))
    k3 = pl.BlockSpec((None, PEER_NKEYS, PEER_HEADS, tb), lambda b, i, e: (b * nblk + i, 0, 0, 0))
    if gate.shape[1] == 1:
        gspec = pl.BlockSpec((1, 1, d), lambda b, i, e: (b, 0, 0))
    else:
        gspec = pl.BlockSpec((1, tb, d), lambda b, i, e: (b, i, 0), pipeline_mode=one)
    return pl.pallas_call(
        functools.partial(_peer_body, n_i=n_i),
        grid=(nb, nblk, n_pairs + 1),
        in_specs=[pl.BlockSpec((None, d, tb), lambda b, i, e: (b * nblk + i, 0, 0)),
                  pl.BlockSpec((None, 2 * eb, d), lambda b, i, e: (layer, jnp.minimum(e, n_pairs - 1), 0)),
                  pl.BlockSpec((None, None, d, eb), lambda b, i, e: (layer, jnp.maximum(2 * e - 1, 0), 0, 0)),
                  pl.BlockSpec((None, None, d, eb),
                               lambda b, i, e: (layer, jnp.minimum(2 * e, 2 * n_pairs - 1), 0, 0)),
                  t3, t3, k3, k3,
                  pl.BlockSpec((1, tb, d), lambda b, i, e: (b, i, 0), pipeline_mode=one), gspec],
        out_specs=pl.BlockSpec((1, tb, d), lambda b, i, e: (b, i, 0)),
        out_shape=jax.ShapeDtypeStruct((nb, s, d), F32),
        scratch_shapes=[pltpu.VMEM((d, tb), F32), pltpu.VMEM((eb, tb), F32), pltpu.VMEM((eb, tb), F32),
                        pltpu.VMEM((eb, tb), BF16)],
        compiler_params=_cparams(("arbitrary", "arbitrary", "arbitrary")),
        name="peer_experts",
    )(h2t, u_tab, vt_tab, vt_tab, r2, e2, c1, e1, x, gate)


def _peer(x, shift, scale, gate, norm_g, wq_t, subkeys, u_tab, vt_tab, layer, tb, eb):
    h2t, r2, e2, c1, e1 = _peer_route(x, shift, scale, norm_g, wq_t, subkeys, tb)
    c1 = jnp.transpose(c1, (0, 2, 1, 3))
    e1 = jnp.transpose(e1, (0, 2, 1, 3))
    return _peer_experts(x, gate, h2t, u_tab, vt_tab, layer, r2, e2, c1, e1, tb, eb)


def _kvq_body(x_ref, ksh_ref, ksc_ref, msh_ref, msc_ref, kng_ref, mng_ref, wkv_ref, wq_ref,
              kg_ref, qg_ref, k_ref, v_ref, kb_ref, vb_ref, q_ref):
    x = x_ref[0]
    d = x.shape[1]
    ms = jnp.mean(x * x, axis=-1, keepdims=True)
    xn = x * lax.rsqrt(ms + EPS)
    hk = (xn * kng_ref[...]) * (1.0 + ksc_ref[0]) + ksh_ref[0]
    kv = jnp.dot(hk.astype(BF16), wkv_ref[...], preferred_element_type=F32)
    k = _head_rms(kv[:, :d], kg_ref[...])
    v = kv[:, d:]
    k_ref[0] = k
    v_ref[0] = v
    kb_ref[0] = k.astype(BF16)
    vb_ref[0] = v.astype(BF16)
    hq = (xn * mng_ref[...]) * (1.0 + msc_ref[0]) + msh_ref[0]
    q = jnp.dot(hq.astype(BF16), wq_ref[...], preferred_element_type=F32)
    q_ref[0] = _head_rms(q, qg_ref[...]).astype(BF16)


def _kvq(x, k_shift, k_scale, m_shift, m_scale, kv_norm_g, mix_norm_g, w_kv, w_q, k_gain, q_gain, rows):
    nb, s, d = x.shape
    xspec = pl.BlockSpec((1, rows, d), lambda b, i: (b, i, 0))
    return pl.pallas_call(
        _kvq_body,
        grid=(nb, s // rows),
        in_specs=[xspec, _mod_spec(k_shift, rows), _mod_spec(k_scale, rows), _mod_spec(m_shift, rows),
                  _mod_spec(m_scale, rows), _const_spec((1, d)), _const_spec((1, d)),
                  _const_spec(w_kv.shape), _const_spec(w_q.shape), _const_spec((1, d)), _const_spec((1, d))],
        out_specs=[xspec] * 5,
        out_shape=[jax.ShapeDtypeStruct((nb, s, d), F32), jax.ShapeDtypeStruct((nb, s, d), F32),
                   jax.ShapeDtypeStruct((nb, s, d), BF16), jax.ShapeDtypeStruct((nb, s, d), BF16),
                   jax.ShapeDtypeStruct((nb, s, d), BF16)],
        compiler_params=_cparams(("arbitrary", "arbitrary")),
        name="kv_q_proj",
    )(x, k_shift, k_scale, m_shift, m_scale, kv_norm_g.reshape(1, d), mix_norm_g.reshape(1, d),
      w_kv, w_q, k_gain, q_gain)


LOG2E = np.float32(math.log2(math.e))
SB_SCALE2 = np.float32(HEAD_DIM ** -0.5 * math.log2(math.e))


def _softplus2(z2):
    return jnp.maximum(z2, 0.0) + jnp.log2(1.0 + jnp.exp2(-jnp.abs(z2)))


def _split_bf16(x):
    hi = x.astype(BF16)
    lo = (x - hi.astype(F32)).astype(BF16)
    return hi, lo


def _sb_prompt_body(q_ref, k_ref, v_ref, bias_ref, tri_ref, o_ref, *, blk, heads, scale):
    qi = pl.program_id(2)
    tri = tri_ref[...]
    hsl = [slice(h * HEAD_DIM, (h + 1) * HEAD_DIM) for h in range(heads)]
    qs = [q_ref[0, :, hsl[h]] for h in range(heads)]
    biases = [bias_ref[0, h:h + 1, :] for h in range(heads)]
    hr = range(heads)

    def tiles(kb, carry, mask):
        ks = pl.multiple_of(kb * blk, blk)
        z = [lax.dot_general(qs[h], k_ref[0, pl.ds(ks, blk), hsl[h]], (((1,), (1,)), ((), ())),
                             preferred_element_type=F32) * scale + biases[h] for h in hr]
        sp, lm, surv, pv = [None] * heads, [None] * heads, [None] * heads, [None] * heads
        for h in hr:
            sp[h] = _softplus2(z[h])
            lm[h] = -sp[h] if mask is None else jnp.where(mask, -sp[h], 0.0)
            hi, lo = _split_bf16(lm[h])
            surv[h] = jnp.dot(jnp.concatenate([hi, lo], axis=1), tri, preferred_element_type=F32)
        for h in hr:
            a = jnp.exp2(z[h] - sp[h] + (surv[h] + carry[h][1]))
            if mask is not None:
                a = jnp.where(mask, a, 0.0)
            pv[h] = jnp.dot(a.astype(BF16), v_ref[0, pl.ds(ks, blk), hsl[h]], preferred_element_type=F32)
        return tuple((carry[h][0] + pv[h], carry[h][1] + jnp.sum(lm[h], axis=1, keepdims=True)) for h in hr)

    row = lax.broadcasted_iota(jnp.int32, (blk, blk), 0)
    col = lax.broadcasted_iota(jnp.int32, (blk, blk), 1)
    init = tuple((jnp.zeros((blk, HEAD_DIM), F32), jnp.zeros((blk, 1), F32)) for _ in range(heads))
    carry = tiles(qi, init, col < row)
    carry = lax.fori_loop(1, qi + 1, lambda j, c: tiles(qi - j, c, None), carry)
    for h in range(heads):
        o_ref[0, :, hsl[h]] = carry[h][0].astype(o_ref.dtype)


def _suffix_matrix(n):
    m = (np.arange(n)[:, None] > np.arange(n)[None, :]).astype(np.float32)
    return jnp.asarray(np.concatenate([m, m], axis=0), dtype=BF16)


def _sb_prompt(q, k, v, logit_bias, blk, heads):
    nb, s, d = q.shape
    bias = jnp.broadcast_to((logit_bias.astype(F32) * LOG2E).reshape(N_HEADS // heads, heads, 1),
                            (N_HEADS // heads, heads, blk))
    qspec = pl.BlockSpec((1, blk, heads * HEAD_DIM), lambda b, h, i: (b, i, h))
    kvspec = pl.BlockSpec((1, s, heads * HEAD_DIM), lambda b, h, i: (b, 0, h))
    return pl.pallas_call(
        functools.partial(_sb_prompt_body, blk=blk, heads=heads, scale=SB_SCALE2),
        grid=(nb, N_HEADS // heads, s // blk),
        in_specs=[qspec, kvspec, kvspec, pl.BlockSpec((1, heads, blk), lambda b, h, i: (h, 0, 0)),
                  pl.BlockSpec((2 * blk, blk), lambda b, h, i: (0, 0))],
        out_specs=qspec,
        out_shape=jax.ShapeDtypeStruct((nb, s, d), BF16),
        compiler_params=_cparams(("arbitrary", "arbitrary", "arbitrary")),
        name="sb_prompt",
    )(q, k, v, bias, _suffix_matrix(blk))


def _sb_paged_body(pt_ref, *refs, pages_per_step, n_q, scale):
    kp = refs[:pages_per_step]
    vp = refs[pages_per_step:2 * pages_per_step]
    wq_ref, bias_ref, knew_ref, vnew_ref, tri_ref, o_ref, acc_ref, run_ref = refs[2 * pages_per_step:]
    step = pl.program_id(1)
    wq = wq_ref[0]
    bias = bias_ref[...]

    def heads_to_lanes(ref):
        n_pos = ref.shape[1] // N_HEADS
        return jnp.concatenate(
            [ref[0, pl.ds(h, n_pos, stride=N_HEADS), :].astype(BF16) for h in range(N_HEADS)], axis=1)

    def block(kblk, vblk, mask):
        z = jnp.dot(kblk, wq, preferred_element_type=F32) * scale + bias
        sp = _softplus2(z)
        lm = -sp if mask is None else jnp.where(mask, -sp, 0.0)
        hi, lo = _split_bf16(lm)
        surv = jnp.dot(tri_ref[...], jnp.concatenate([hi, lo], axis=0),
                       preferred_element_type=F32) + run_ref[...]
        a = jnp.exp2(z - sp + surv)
        if mask is not None:
            a = jnp.where(mask, a, 0.0)
        acc_ref[...] += jnp.dot(a.T.astype(BF16), vblk, preferred_element_type=F32)
        run_ref[...] += jnp.sum(lm, axis=0, keepdims=True)

    @pl.when(step == 0)
    def _():
        acc_ref[...] = jnp.zeros_like(acc_ref)
        run_ref[...] = jnp.zeros_like(run_ref)
        shape = (knew_ref.shape[1], wq.shape[1])
        kpos = lax.broadcasted_iota(jnp.int32, shape, 0)
        qpos = lax.broadcasted_iota(jnp.int32, shape, 1) % n_q
        block(knew_ref[0].astype(BF16), vnew_ref[0].astype(BF16), kpos < qpos)

    for i in range(pages_per_step):
        block(heads_to_lanes(kp[i]), heads_to_lanes(vp[i]), None)

    @pl.when(step == pl.num_programs(1) - 1)
    def _():
        o_ref[0] = acc_ref[...]


def _suffix_matrix_t(n):
    m = (np.arange(n)[None, :] > np.arange(n)[:, None]).astype(np.float32)
    return jnp.asarray(np.concatenate([m, m], axis=1), dtype=BF16)


def _sb_paged(q, k_new, v_new, cache_k, cache_v, page_table, logit_bias, pages_per_step):
    nb, n_q, d = q.shape
    n_pool, page = cache_k.shape[:2]
    n_pages = page_table.shape[1]
    ck = cache_k.reshape(n_pool, page * N_HEADS, HEAD_DIM)
    cv = cache_v.reshape(n_pool, page * N_HEADS, HEAD_DIM)
    ncol = LANES
    assert N_HEADS * n_q <= ncol and n_q <= page
    qh = q.reshape(nb, n_q, N_HEADS, HEAD_DIM)
    eye = jnp.eye(N_HEADS, dtype=BF16)
    wq = jnp.einsum('bqhd,hg->bhdgq', qh, eye).reshape(nb, d, N_HEADS * n_q)
    wq = jnp.pad(wq, ((0, 0), (0, 0), (0, ncol - N_HEADS * n_q)))
    bias = jnp.pad(jnp.repeat(logit_bias.astype(F32) * LOG2E, n_q), (0, ncol - N_HEADS * n_q)).reshape(1, ncol)
    knew = jnp.pad(k_new, ((0, 0), (0, page - n_q), (0, 0)))
    vnew = jnp.pad(v_new, ((0, 0), (0, page - n_q), (0, 0)))
    pt = page_table.reshape(-1).astype(jnp.int32)

    def page_map(i):
        return lambda b, s, pt_ref: (pt_ref[b * n_pages + n_pages - 1 - (s * pages_per_step + i)], 0, 0)

    page_specs = [pl.BlockSpec((1, page * N_HEADS, HEAD_DIM), page_map(i)) for i in range(pages_per_step)]
    grid_spec = pltpu.PrefetchScalarGridSpec(
        num_scalar_prefetch=1,
        grid=(nb, n_pages // pages_per_step),
        in_specs=page_specs + page_specs + [
            pl.BlockSpec((1, d, ncol), lambda b, s, pt_ref: (b, 0, 0)),
            pl.BlockSpec((1, ncol), lambda b, s, pt_ref: (0, 0)),
            pl.BlockSpec((1, page, d), lambda b, s, pt_ref: (b, 0, 0)),
            pl.BlockSpec((1, page, d), lambda b, s, pt_ref: (b, 0, 0)),
            pl.BlockSpec((page, 2 * page), lambda b, s, pt_ref: (0, 0))],
        out_specs=pl.BlockSpec((1, ncol, d), lambda b, s, pt_ref: (b, 0, 0)),
        scratch_shapes=[pltpu.VMEM((ncol, d), F32), pltpu.VMEM((1, ncol), F32)])
    return pl.pallas_call(
        functools.partial(_sb_paged_body, pages_per_step=pages_per_step, n_q=n_q,
                          scale=SB_SCALE2),
        grid_spec=grid_spec,
        out_shape=jax.ShapeDtypeStruct((nb, ncol, d), F32),
        compiler_params=_cparams(("arbitrary", "arbitrary")),
        name="sb_paged",
    )(pt, *([ck] * pages_per_step), *([cv] * pages_per_step), wq, bias, knew, vnew,
      _suffix_matrix_t(page))


def _oproj_body(o_ref, x_ref, gt_ref, w_ref, y_ref):
    mix = jnp.dot(o_ref[0], w_ref[...], preferred_element_type=F32)
    y_ref[0] = x_ref[0] + gt_ref[0] * mix


def _oproj(o, x, gate, w_o, rows):
    nb, s, d = x.shape
    xspec = pl.BlockSpec((1, rows, d), lambda b, i: (b, i, 0))
    return pl.pallas_call(
        _oproj_body,
        grid=(nb, s // rows),
        in_specs=[xspec, xspec, _mod_spec(gate, rows), _const_spec(w_o.shape)],
        out_specs=xspec,
        out_shape=jax.ShapeDtypeStruct((nb, s, d), F32),
        compiler_params=_cparams(("arbitrary", "arbitrary")),
        name="attn_out_proj",
    )(o, x, gate, w_o)


def _mix_weights_prompt(w_s, b_s, width):
    causal = jnp.tril(jnp.ones((CHUNK, CHUNK), dtype=bool))
    wmix = jnp.where(causal[None], w_s, 0.0).astype(BF16)
    bmix = jnp.repeat(b_s.T, width // GROUPS, axis=1)
    return wmix, bmix


def _mix_weights_sample(w_s, b_s, width, n_seq, n_tok):
    causal = jnp.tril(jnp.ones((n_tok, n_tok), dtype=bool))
    small = jnp.where(causal[None], w_s[:, :n_tok, :n_tok], 0.0)
    eye = jnp.eye(n_seq, dtype=F32)
    blockdiag = jnp.einsum('ab,gts->gatbs', eye, small).reshape(GROUPS, n_seq * n_tok, n_seq * n_tok)
    padn = SAMPLE_ROWS - n_seq * n_tok
    wmix = jnp.pad(blockdiag, ((0, 0), (0, padn), (0, padn))).astype(BF16)
    brow = jnp.tile(b_s.T[:n_tok], (n_seq, 1))
    bmix = jnp.repeat(jnp.pad(brow, ((0, padn), (0, 0))), width // GROUPS, axis=1)
    return wmix, bmix


def kernel(x_prompt, x_sample, cache_k, cache_v, page_table, c_prompt, c_sample, mod_w, mod_b, norm_mix_g, norm_ffn_g, a_w_in, a_b_in, a_v_norm_g, a_w_s, a_b_s, a_w_out, kv_mod_w, kv_mod_b, kv_norm_g, kv_w, k_norm_g, b_w_q, b_q_norm_g, b_logit_bias, b_w_o, peer_w_q, peer_subkeys, peer_u, peer_v):
    nbp, seq, d = x_prompt.shape
    nbs, n_tok, _ = x_sample.shape
    n_samp = nbs * n_tok
    width = a_w_out.shape[1]

    c_all = jnp.concatenate([c_prompt, c_sample], axis=0)
    c_all = jnp.pad(c_all, ((0, BF16_ROWS - c_all.shape[0]), (0, 0)))
    mods = _adaln(c_all, mod_w, mod_b)
    kvmod = _adaln(c_all, kv_mod_w[None], kv_mod_b[None])[0]

    def split_mod(m, n):
        parts = jnp.split(m, n, axis=-1)
        prm = [p[:nbp, None, :] for p in parts]
        smp = [jnp.pad(jnp.repeat(p[nbp:nbp + nbs], n_tok, axis=0), ((0, SAMPLE_ROWS - n_samp), (0, 0)))[None]
               for p in parts]
        return prm, smp

    mod_p, mod_s = zip(*[split_mod(mods[l], 6) for l in range(mods.shape[0])])
    kvmod_p, kvmod_s = split_mod(kvmod, 2)

    w_in = a_w_in[0].astype(BF16)
    w_out = a_w_out[0].astype(BF16)
    wq_t = [jnp.transpose(peer_w_q[l]).astype(BF16) for l in range(2)]
    subk = [peer_subkeys[l].astype(BF16) for l in range(2)]
    u_tab = peer_u.astype(BF16)
    eb = 512
    n_l, n_exp, _ = peer_v.shape
    vt_tab = jnp.transpose(peer_v.reshape(n_l, n_exp // eb, eb, d), (0, 1, 3, 2)).astype(BF16)
    w_kv = kv_w.astype(BF16)
    w_q1 = b_w_q[0].astype(BF16)
    w_o1 = b_w_o[0].astype(BF16)
    k_gain = jnp.tile(k_norm_g, N_HEADS).reshape(1, d)
    q_gain = jnp.tile(b_q_norm_g[0], N_HEADS).reshape(1, d)
    wmix_p, bmix_p = _mix_weights_prompt(a_w_s[0], a_b_s[0], width)
    wmix_s, bmix_s = _mix_weights_sample(a_w_s[0], a_b_s[0], width, nbs, n_tok)

    xs = jnp.pad(x_sample.reshape(1, n_samp, d), ((0, 0), (0, SAMPLE_ROWS - n_samp), (0, 0)))

    def trunk(x, mod, kvm, wmix, bmix, rows, tb, attend):
        sh_m, sc_m, g_m, sh_f, sc_f, g_f = mod[0]
        x1, v_rows = _mixer_a(x, sh_m, sc_m, g_m, norm_mix_g[0], w_in, a_b_in[0], a_v_norm_g[0],
                              wmix, bmix, w_out, rows)
        x2 = _peer(x1, sh_f, sc_f, g_f, norm_ffn_g[0], wq_t[0], subk[0], u_tab, vt_tab, 0, tb, eb)
        sh_m, sc_m, g_m, sh_f, sc_f, g_f = mod[1]
        k, v, kb, vb, q = _kvq(x2, kvm[0], kvm[1], sh_m, sc_m, kv_norm_g, norm_mix_g[1], w_kv, w_q1,
                               k_gain, q_gain, rows)
        o = attend(q, k, v, kb, vb)
        x3 = _oproj(o, x2, g_m, w_o1, rows)
        y = _peer(x3, sh_f, sc_f, g_f, norm_ffn_g[1], wq_t[1], subk[1], u_tab, vt_tab, 1, tb, eb)
        return y, k, v, v_rows

    def attend_prompt(q, k, v, kb, vb):
        return _sb_prompt(q, kb, vb, b_logit_bias[0], 256, 4)

    def attend_sample(q, k, v, kb, vb):
        qs = q[0, :n_samp].reshape(nbs, n_tok, d)
        ks = k[0, :n_samp].reshape(nbs, n_tok, d)
        vs = v[0, :n_samp].reshape(nbs, n_tok, d)
        full = _sb_paged(qs, ks, vs, cache_k, cache_v, page_table, b_logit_bias[0], 4)
        full = full[:, :N_HEADS * n_tok].reshape(nbs, N_HEADS, n_tok, N_HEADS, HEAD_DIM)
        idx = jnp.arange(N_HEADS)
        o = full[:, idx, :, idx, :]
        o = jnp.transpose(o, (1, 2, 0, 3)).reshape(1, n_samp, d)
        return jnp.pad(o, ((0, 0), (0, SAMPLE_ROWS - n_samp), (0, 0))).astype(BF16)

    y_p, k_p, v_p, _ = trunk(x_prompt, mod_p, kvmod_p, wmix_p, bmix_p, 256, 512, attend_prompt)
    y_s, k_s, v_s, gv_s = trunk(xs, mod_s, kvmod_s, wmix_s, bmix_s, SAMPLE_ROWS, SAMPLE_ROWS, attend_sample)

    def samp(a, shape):
        return a[0, :n_samp].reshape(shape)

    return (y_p,
            samp(y_s, (nbs, n_tok, d)),
            k_p.reshape(nbp, seq, N_HEADS, HEAD_DIM),
            v_p.reshape(nbp, seq, N_HEADS, HEAD_DIM),
            samp(k_s, (nbs, n_tok, N_HEADS, HEAD_DIM)),
            samp(v_s, (nbs, n_tok, N_HEADS, HEAD_DIM)),
            samp(gv_s, (1, nbs, n_tok, width)))
```

```python
import functools
import math

import jax
import jax.numpy as jnp
import numpy as np
from jax import lax
from jax.experimental import pallas as pl
from jax.experimental.pallas import tpu as pltpu

F32 = jnp.float32
BF16 = jnp.bfloat16
EPS = 1e-6

LANES = 128
SUBLANES = 8
BF16_ROWS = 16
VMEM_LIMIT = 56 * 1024 * 1024

CHUNK = 128
GROUPS = 16
N_HEADS = 16
HEAD_DIM = 128
PEER_HEADS = 8
PEER_NKEYS = 128
PEER_TOPK = 16
SAMPLE_ROWS = 128


def _cparams(sem):
    return pltpu.CompilerParams(dimension_semantics=sem, vmem_limit_bytes=VMEM_LIMIT)


def _const_spec(shape):
    nd = len(shape)
    return pl.BlockSpec(shape, lambda *_: (0,) * nd, pipeline_mode=pl.Buffered(1))


def _rms(x, g):
    ms = jnp.mean(x * x, axis=-1, keepdims=True)
    return x * lax.rsqrt(ms + EPS) * g


def _gelu(x):
    return 0.5 * x * (1.0 + lax.erf(x * np.float32(math.sqrt(0.5))))


def _head_rms(x, g):
    cols = []
    for h in range(x.shape[1] // HEAD_DIM):
        seg = x[:, h * HEAD_DIM:(h + 1) * HEAD_DIM]
        ms = jnp.mean(seg * seg, axis=-1, keepdims=True)
        cols.append(seg * lax.rsqrt(ms + EPS))
    return jnp.concatenate(cols, axis=1) * g


def _adaln_body(c_ref, w_ref, b_ref, o_ref):
    c = c_ref[...]
    s = c / (1.0 + jnp.exp(-c))
    o_ref[0] = jnp.dot(s.astype(BF16), w_ref[0].astype(BF16), preferred_element_type=F32) + b_ref[0]


def _adaln(c, w, b):
    n_l, d, n = w.shape
    m = c.shape[0]
    tn = 1024
    return pl.pallas_call(
        _adaln_body,
        grid=(n_l, n // tn),
        in_specs=[pl.BlockSpec((m, d), lambda l, j: (0, 0)),
                  pl.BlockSpec((1, d, tn), lambda l, j: (l, 0, j)),
                  pl.BlockSpec((1, 1, tn), lambda l, j: (l, 0, j))],
        out_specs=pl.BlockSpec((1, m, tn), lambda l, j: (l, 0, j)),
        out_shape=jax.ShapeDtypeStruct((n_l, m, n), F32),
        compiler_params=_cparams(("arbitrary", "arbitrary")),
        name="adaln",
    )(c, w, b.reshape(n_l, 1, n))


def _mod_spec(mod, rows):
    d = mod.shape[-1]
    if mod.shape[1] == 1:
        return pl.BlockSpec((1, 1, d), lambda b, i: (b, 0, 0))
    return pl.BlockSpec((1, rows, d), lambda b, i: (b, i, 0))


def _mixer_a_body(x_ref, sh_ref, sc_ref, gt_ref, ng_ref, win_ref, bin_ref, vg_ref, wmix_ref,
                  bmix_ref, wout_ref, x1_ref, v_ref, *, chunk):
    x = x_ref[0]
    rows = x.shape[0]
    h = _rms(x, ng_ref[...]) * (1.0 + sc_ref[0]) + sh_ref[0]
    z = jnp.dot(h.astype(BF16), win_ref[...], preferred_element_type=F32) + bin_ref[...]
    z = _gelu(z)
    width = z.shape[1] // 2
    u = z[:, :width]
    v = _rms(z[:, width:], vg_ref[...])
    v_ref[0] = v
    vb = v.astype(BF16)
    gd = width // GROUPS
    blocks = []
    for c in range(rows // chunk):
        cols = [jnp.dot(wmix_ref[g], vb[c * chunk:(c + 1) * chunk, g * gd:(g + 1) * gd],
                        preferred_element_type=F32) for g in range(GROUPS)]
        blocks.append(jnp.concatenate(cols, axis=1) + bmix_ref[...])
    mixed = blocks[0] if len(blocks) == 1 else jnp.concatenate(blocks, axis=0)
    um = (u * mixed).astype(BF16)
    mix = jnp.dot(um, wout_ref[...], preferred_element_type=F32)
    x1_ref[0] = x + gt_ref[0] * mix


def _mixer_a(x, shift, scale, gate, norm_g, w_in, b_in, v_g, wmix, bmix, w_out, rows):
    nb, s, d = x.shape
    width = w_out.shape[0]
    chunk = wmix.shape[1]
    xspec = pl.BlockSpec((1, rows, d), lambda b, i: (b, i, 0))
    return pl.pallas_call(
        functools.partial(_mixer_a_body, chunk=chunk),
        grid=(nb, s // rows),
        in_specs=[xspec, _mod_spec(shift, rows), _mod_spec(scale, rows), _mod_spec(gate, rows),
                  _const_spec((1, d)), _const_spec(w_in.shape), _const_spec((1, 2 * width)),
                  _const_spec((1, width)), _const_spec(wmix.shape), _const_spec(bmix.shape),
                  _const_spec(w_out.shape)],
        out_specs=[xspec, pl.BlockSpec((1, rows, width), lambda b, i: (b, i, 0))],
        out_shape=[jax.ShapeDtypeStruct((nb, s, d), F32), jax.ShapeDtypeStruct((nb, s, width), F32)],
        compiler_params=_cparams(("arbitrary", "arbitrary")),
        name="gmlp_mixer",
    )(x, shift, scale, gate, norm_g.reshape(1, d), w_in, b_in.reshape(1, -1), v_g.reshape(1, -1),
      wmix, bmix, w_out)


def _sort_network(n):
    pairs = []
    p = 1
    while p < n:
        k = p
        while k >= 1:
            for j in range(k % p, n - k, 2 * k):
                for i in range(min(k, n - j - k)):
                    if (i + j) // (2 * p) == (i + j + k) // (2 * p):
                        pairs.append((i + j, i + j + k))
            k //= 2
        p *= 2
    return pairs


_SORT16 = _sort_network(PEER_TOPK)
_BITONIC16 = [(i, i | d) for d in (8, 4, 2, 1) for i in range(PEER_TOPK) if not i & d]


def _apply_network(xs, pairs):
    xs = list(xs)
    for i, j in pairs:
        a, b = xs[i], xs[j]
        xs[i] = jnp.maximum(a, b)
        xs[j] = jnp.minimum(a, b)
    return xs


def _merge_top16(xs, ys):
    zs = [jnp.maximum(xs[r], ys[PEER_TOPK - 1 - r]) for r in range(PEER_TOPK)]
    return _apply_network(zs, _BITONIC16)


def _top16_of_keys(s):
    slabs = [s[SUBLANES * g:SUBLANES * (g + 1), :] for g in range(PEER_NKEYS // SUBLANES)]
    xs = _apply_network(slabs, _SORT16)
    for shift in (4, 2, 1):
        ys = [pltpu.roll(x, shift, axis=0) for x in xs]
        xs = _merge_top16(xs, ys)
    return xs


_CAND = [(r, c) for r in range(PEER_TOPK) for c in range(PEER_TOPK) if (r + 1) * (c + 1) <= PEER_TOPK]


def _route_body(x_ref, sh_ref, sc_ref, ng_ref, wq_ref, sk_ref,
                h2t_ref, r2_ref, e2_ref, c1_ref, e1_ref, s_ref):
    x = x_ref[0]
    h = _rms(x, ng_ref[...]) * (1.0 + sc_ref[0]) + sh_ref[0]
    ht = h.T.astype(BF16)
    h2t_ref[...] = ht
    qt = jnp.dot(wq_ref[...], ht, preferred_element_type=F32).astype(BF16)
    for hp in range(2 * PEER_HEADS):
        s_ref[hp] = jnp.dot(sk_ref[hp % 2], qt[hp * PEER_NKEYS:(hp + 1) * PEER_NKEYS, :],
                            preferred_element_type=F32)
    tb = x.shape[0]
    neg = jnp.full((SUBLANES, LANES), -jnp.inf, F32)
    sub = lax.broadcasted_iota(jnp.int32, (SUBLANES, LANES), 0)

    def chunk(lc, carry):
        lanes = pl.ds(pl.multiple_of(lc * LANES, LANES), LANES)
        top = [[neg] * PEER_TOPK, [neg] * PEER_TOPK]
        for hp in range(2 * PEER_HEADS):
            hh, p = hp // 2, hp % 2
            xs = _top16_of_keys(s_ref[hp, :, lanes])
            top[p] = [jnp.where(sub == hh, xs[r], top[p][r]) for r in range(PEER_TOPK)]
        a, b = top
        cand = {rc: a[rc[0]] + b[rc[1]] for rc in _CAND}
        g1 = [cand[(0, c)] for c in range(PEER_TOPK)]
        rest = [cand[rc] for rc in _CAND if rc[0] > 0]
        rest = rest + [neg] * (3 * PEER_TOPK - len(rest))
        gs = [_apply_network(rest[PEER_TOPK * k:PEER_TOPK * (k + 1)], _SORT16) for k in range(3)]
        m1 = _merge_top16(g1, gs[0])
        m2 = _merge_top16(gs[1], gs[2])
        zs = [jnp.maximum(m1[r], m2[PEER_TOPK - 1 - r]) for r in range(PEER_TOPK)]
        tau = functools.reduce(jnp.minimum, zs)
        top_sum = a[0] + b[0]
        zsum = jnp.zeros((SUBLANES, LANES), F32)
        for rc in _CAND:
            zsum = zsum + jnp.where(cand[rc] >= tau, jnp.exp(cand[rc] - top_sum), 0.0)
        zinv = 1.0 / zsum
        for hh in range(PEER_HEADS):
            def row(v):
                return jnp.broadcast_to(v[hh:hh + 1, :], (SUBLANES, LANES))
            bh = [row(b[c]) for c in range(PEER_TOPK)]
            tau_h, zinv_h, a0_h = row(tau), row(zinv), row(a[0])
            for gp in range(PEER_NKEYS // BF16_ROWS):
                r2s, e2s, c1s, e1s = [], [], [], []
                for g in (2 * gp, 2 * gp + 1):
                    rows = slice(SUBLANES * g, SUBLANES * (g + 1))
                    s1 = s_ref[2 * hh, rows, lanes]
                    s2 = s_ref[2 * hh + 1, rows, lanes]
                    r2 = jnp.zeros((SUBLANES, LANES), F32)
                    c1 = jnp.zeros((SUBLANES, LANES), F32)
                    for c in range(PEER_TOPK):
                        r2 = r2 + jnp.where(bh[c] > s2, 1.0, 0.0)
                        c1 = c1 + jnp.where(s1 + bh[c] >= tau_h, 1.0, 0.0)
                    r2s.append(r2)
                    c1s.append(c1)
                    e2s.append(jnp.exp(s2 - bh[0]))
                    e1s.append(jnp.exp(s1 - a0_h) * zinv_h)
                rows16 = slice(BF16_ROWS * gp, BF16_ROWS * (gp + 1))
                r2_ref[hh, rows16, lanes] = jnp.concatenate(r2s, axis=0).astype(BF16)
                e2_ref[hh, rows16, lanes] = jnp.concatenate(e2s, axis=0).astype(BF16)
                c1_ref[hh, rows16, lanes] = jnp.concatenate(c1s, axis=0)
                e1_ref[hh, rows16, lanes] = jnp.concatenate(e1s, axis=0)
        return carry

    lax.fori_loop(0, tb // LANES, chunk, 0)


def _peer_route(x, shift, scale, norm_g, wq_t, subkeys, tb):
    nb, s, d = x.shape
    nblk = s // tb
    nt = nb * nblk
    tspec3 = pl.BlockSpec((None, PEER_HEADS, PEER_NKEYS, tb), lambda b, i: (b * nblk + i, 0, 0, 0))
    return pl.pallas_call(
        _route_body,
        grid=(nb, nblk),
        in_specs=[pl.BlockSpec((1, tb, d), lambda b, i: (b, i, 0)), _mod_spec(shift, tb), _mod_spec(scale, tb),
                  _const_spec((1, d)), _const_spec(wq_t.shape), _const_spec(subkeys.shape)],
        out_specs=[pl.BlockSpec((None, d, tb), lambda b, i: (b * nblk + i, 0, 0)), tspec3, tspec3, tspec3, tspec3],
        out_shape=[jax.ShapeDtypeStruct((nt, d, tb), BF16),
                   jax.ShapeDtypeStruct((nt, PEER_HEADS, PEER_NKEYS, tb), BF16),
                   jax.ShapeDtypeStruct((nt, PEER_HEADS, PEER_NKEYS, tb), BF16),
                   jax.ShapeDtypeStruct((nt, PEER_HEADS, PEER_NKEYS, tb), F32),
                   jax.ShapeDtypeStruct((nt, PEER_HEADS, PEER_NKEYS, tb), F32)],
        scratch_shapes=[pltpu.VMEM((2 * PEER_HEADS, PEER_NKEYS, tb), F32)],
        compiler_params=_cparams(("arbitrary", "arbitrary")),
        name="peer_route",
    )(x, shift, scale, norm_g.reshape(1, d), wq_t, subkeys)


def _peer_gate_stage(act_ref, p_ref, r2_ref, e2_ref, c1_ref, e1_ref, key0, n_i, valid=None):
    tb = act_ref.shape[1]
    for il in range(n_i):
        c1i = c1_ref[key0 + il]
        e1i = e1_ref[key0 + il]
        for lc in range(tb // LANES):
            lanes = slice(lc * LANES, (lc + 1) * LANES)
            gate = [jnp.zeros((BF16_ROWS, LANES), BF16)] * (PEER_NKEYS // BF16_ROWS)
            for hh in range(PEER_HEADS):
                c1b = jnp.broadcast_to(c1i[hh:hh + 1, lanes], (BF16_ROWS, LANES)).astype(BF16)
                e1b = jnp.broadcast_to(e1i[hh:hh + 1, lanes], (BF16_ROWS, LANES)).astype(BF16)
                for rb in range(PEER_NKEYS // BF16_ROWS):
                    rows = slice(rb * BF16_ROWS, (rb + 1) * BF16_ROWS)
                    r2 = r2_ref[hh, rows, lanes]
                    e2 = e2_ref[hh, rows, lanes]
                    gate[rb] = gate[rb] + jnp.where(r2 < c1b, e2, jnp.zeros_like(e2)) * e1b
            for rb in range(PEER_NKEYS // BF16_ROWS):
                rows = slice(il * PEER_NKEYS + rb * BF16_ROWS, il * PEER_NKEYS + (rb + 1) * BF16_ROWS)
                a = act_ref[rows, lanes]
                p = (_gelu(a) * gate[rb].astype(F32)).astype(BF16)
                p_ref[rows, lanes] = p if valid is None else jnp.where(valid, p, jnp.zeros_like(p))


def _peer_body(h2t_ref, u_ref, vta_ref, vtb_ref, r2_ref, e2_ref, c1_ref, e1_ref, x_ref, g_ref, y_ref,
               acc_ref, acta_ref, actb_ref, pa_ref, pb_ref, *, n_i):
    s = pl.program_id(2)
    last = pl.num_programs(2) - 1
    n_blocks = 2 * last
    eb = acta_ref.shape[0]
    routing = (r2_ref, e2_ref, c1_ref, e1_ref)

    @pl.when(s == 0)
    def _():
        acc_ref[...] = jnp.zeros_like(acc_ref)
        actb_ref[...] = jnp.zeros_like(actb_ref)

    acta_ref[...] = jnp.dot(u_ref[:eb, :], h2t_ref[...], preferred_element_type=F32)
    _peer_gate_stage(actb_ref, pb_ref, *routing, jnp.maximum(2 * s - 1, 0) * n_i, n_i)
    acc_ref[...] += jnp.dot(vta_ref[...], pb_ref[...], preferred_element_type=F32)
    actb_ref[...] = jnp.dot(u_ref[eb:, :], h2t_ref[...], preferred_element_type=F32)
    _peer_gate_stage(acta_ref, pa_ref, *routing, jnp.minimum(2 * s, n_blocks - 1) * n_i, n_i, valid=s < last)
    acc_ref[...] += jnp.dot(vtb_ref[...], pa_ref[...], preferred_element_type=F32)

    @pl.when(s == last)
    def _():
        y_ref[0] = x_ref[0] + g_ref[0] * acc_ref[...].T


def _peer_experts(x, gate, h2t, u_tab, vt_tab, layer, r2, e2, c1, e1, tb, eb):
    nb, s, d = x.shape
    nblk = s // tb
    n_exp = u_tab.shape[1]
    n_i = eb // PEER_NKEYS
    n_pairs = n_exp // (2 * eb)
    assert vt_tab.shape[1:] == (2 * n_pairs, d, eb)
    one = pl.Buffered(1)
    t3 = pl.BlockSpec((None, PEER_HEADS, PEER_NKEYS, tb), lambda b, i, e: (b * nblk + i, 0, 0, 0))
    k3 = pl.BlockSpec((None, PEER_NKEYS, PEER_HEADS, tb), lambda b, i, e: (b * nblk + i, 0, 0, 0))
    if gate.shape[1] == 1:
        gspec = pl.BlockSpec((1, 1, d), lambda b, i, e: (b, 0, 0))
    else:
        gspec = pl.BlockSpec((1, tb, d), lambda b, i, e: (b, i, 0), pipeline_mode=one)
    return pl.pallas_call(
        functools.partial(_peer_body, n_i=n_i),
        grid=(nb, nblk, n_pairs + 1),
        in_specs=[pl.BlockSpec((None, d, tb), lambda b, i, e: (b * nblk + i, 0, 0)),
                  pl.BlockSpec((None, 2 * eb, d), lambda b, i, e: (layer, jnp.minimum(e, n_pairs - 1), 0)),
                  pl.BlockSpec((None, None, d, eb), lambda b, i, e: (layer, jnp.maximum(2 * e - 1, 0), 0, 0)),
                  pl.BlockSpec((None, None, d, eb),
                               lambda b, i, e: (layer, jnp.minimum(2 * e, 2 * n_pairs - 1), 0, 0)),
                  t3, t3, k3, k3,
                  pl.BlockSpec((1, tb, d), lambda b, i, e: (b, i, 0), pipeline_mode=one), gspec],
        out_specs=pl.BlockSpec((1, tb, d), lambda b, i, e: (b, i, 0)),
        out_shape=jax.ShapeDtypeStruct((nb, s, d), F32),
        scratch_shapes=[pltpu.VMEM((d, tb), F32), pltpu.VMEM((eb, tb), F32), pltpu.VMEM((eb, tb), F32),
                        pltpu.VMEM((eb, tb), BF16), pltpu.VMEM((eb, tb), BF16)],
        compiler_params=_cparams(("arbitrary", "arbitrary", "arbitrary")),
        name="peer_experts",
    )(h2t, u_tab, vt_tab, vt_tab, r2, e2, c1, e1, x, gate)


def _peer(x, shift, scale, gate, norm_g, wq_t, subkeys, u_tab, vt_tab, layer, tb, eb):
    h2t, r2, e2, c1, e1 = _peer_route(x, shift, scale, norm_g, wq_t, subkeys, tb)
    c1 = jnp.transpose(c1, (0, 2, 1, 3))
    e1 = jnp.transpose(e1, (0, 2, 1, 3))
    return _peer_experts(x, gate, h2t, u_tab, vt_tab, layer, r2, e2, c1, e1, tb, eb)


def _kvq_body(x_ref, ksh_ref, ksc_ref, msh_ref, msc_ref, kng_ref, mng_ref, wkv_ref, wq_ref,
              kg_ref, qg_ref, k_ref, v_ref, kb_ref, vb_ref, q_ref):
    x = x_ref[0]
    d = x.shape[1]
    ms = jnp.mean(x * x, axis=-1, keepdims=True)
    xn = x * lax.rsqrt(ms + EPS)
    hk = (xn * kng_ref[...]) * (1.0 + ksc_ref[0]) + ksh_ref[0]
    kv = jnp.dot(hk.astype(BF16), wkv_ref[...], preferred_element_type=F32)
    k = _head_rms(kv[:, :d], kg_ref[...])
    v = kv[:, d:]
    k_ref[0] = k
    v_ref[0] = v
    kb_ref[0] = k.astype(BF16)
    vb_ref[0] = v.astype(BF16)
    hq = (xn * mng_ref[...]) * (1.0 + msc_ref[0]) + msh_ref[0]
    q = jnp.dot(hq.astype(BF16), wq_ref[...], preferred_element_type=F32)
    q_ref[0] = _head_rms(q, qg_ref[...]).astype(BF16)


def _kvq(x, k_shift, k_scale, m_shift, m_scale, kv_norm_g, mix_norm_g, w_kv, w_q, k_gain, q_gain, rows):
    nb, s, d = x.shape
    xspec = pl.BlockSpec((1, rows, d), lambda b, i: (b, i, 0))
    return pl.pallas_call(
        _kvq_body,
        grid=(nb, s // rows),
        in_specs=[xspec, _mod_spec(k_shift, rows), _mod_spec(k_scale, rows), _mod_spec(m_shift, rows),
                  _mod_spec(m_scale, rows), _const_spec((1, d)), _const_spec((1, d)),
                  _const_spec(w_kv.shape), _const_spec(w_q.shape), _const_spec((1, d)), _const_spec((1, d))],
        out_specs=[xspec] * 5,
        out_shape=[jax.ShapeDtypeStruct((nb, s, d), F32), jax.ShapeDtypeStruct((nb, s, d), F32),
                   jax.ShapeDtypeStruct((nb, s, d), BF16), jax.ShapeDtypeStruct((nb, s, d), BF16),
                   jax.ShapeDtypeStruct((nb, s, d), BF16)],
        compiler_params=_cparams(("arbitrary", "arbitrary")),
        name="kv_q_proj",
    )(x, k_shift, k_scale, m_shift, m_scale, kv_norm_g.reshape(1, d), mix_norm_g.reshape(1, d),
      w_kv, w_q, k_gain, q_gain)


LOG2E = np.float32(math.log2(math.e))
SB_SCALE2 = np.float32(HEAD_DIM ** -0.5 * math.log2(math.e))


def _softplus2(z2):
    return jnp.maximum(z2, 0.0) + jnp.log2(1.0 + jnp.exp2(-jnp.abs(z2)))


def _split_bf16(x):
    hi = x.astype(BF16)
    lo = (x - hi.astype(F32)).astype(BF16)
    return hi, lo


def _sb_prompt_body(q_ref, k_ref, v_ref, bias_ref, tri_ref, o_ref,
                    d_ref, surv_ref, lms_ref, run_ref, acc_ref, *, blk, heads, scale):
    qi = pl.program_id(2)
    hsl = [slice(h * HEAD_DIM, (h + 1) * HEAD_DIM) for h in range(heads)]
    hr = range(heads)

    def logits(kb):
        ks = pl.multiple_of(kb * blk, blk)
        return [lax.dot_general(q_ref[0, :, hsl[h]], k_ref[0, pl.ds(ks, blk), hsl[h]],
                                (((1,), (1,)), ((), ())), preferred_element_type=F32) * scale
                + bias_ref[0, h:h + 1, :] for h in hr]

    def produce(z, mask):
        for h in hr:
            sp = _softplus2(z[h])
            d, lm = z[h] - sp, -sp
            if mask is not None:
                d, lm = jnp.where(mask, d, -jnp.inf), jnp.where(mask, lm, 0.0)
            hi, lo = _split_bf16(lm)
            surv_ref[h] = jnp.dot(jnp.concatenate([hi, lo], axis=1), tri_ref[...],
                                  preferred_element_type=F32)
            d_ref[h] = d
            lms_ref[h] = jnp.sum(lm, axis=1, keepdims=True)

    def consume(kb):
        ks = pl.multiple_of(kb * blk, blk)
        for h in hr:
            a = jnp.exp2(d_ref[h] + surv_ref[h] + run_ref[h])
            acc_ref[h] += jnp.dot(a.astype(BF16), v_ref[0, pl.ds(ks, blk), hsl[h]],
                                  preferred_element_type=F32)
            run_ref[h] += lms_ref[h]

    acc_ref[...] = jnp.zeros_like(acc_ref)
    run_ref[...] = jnp.zeros_like(run_ref)
    row = lax.broadcasted_iota(jnp.int32, (blk, blk), 0)
    col = lax.broadcasted_iota(jnp.int32, (blk, blk), 1)
    produce(logits(qi), col < row)

    def trip(j, carry):
        z = logits(qi - j - 1)
        consume(qi - j)
        produce(z, None)
        return carry

    lax.fori_loop(0, qi, trip, 0)
    consume(0)
    for h in hr:
        o_ref[0, :, hsl[h]] = acc_ref[h].astype(o_ref.dtype)


def _suffix_matrix(n):
    m = (np.arange(n)[:, None] > np.arange(n)[None, :]).astype(np.float32)
    return jnp.asarray(np.concatenate([m, m], axis=0), dtype=BF16)


def _sb_prompt(q, k, v, logit_bias, blk, heads):
    nb, s, d = q.shape
    bias = jnp.broadcast_to((logit_bias.astype(F32) * LOG2E).reshape(N_HEADS // heads, heads, 1),
                            (N_HEADS // heads, heads, blk))
    qspec = pl.BlockSpec((1, blk, heads * HEAD_DIM), lambda b, h, i: (b, i, h))
    kvspec = pl.BlockSpec((1, s, heads * HEAD_DIM), lambda b, h, i: (b, 0, h))
    return pl.pallas_call(
        functools.partial(_sb_prompt_body, blk=blk, heads=heads, scale=SB_SCALE2),
        grid=(nb, N_HEADS // heads, s // blk),
        in_specs=[qspec, kvspec, kvspec, pl.BlockSpec((1, heads, blk), lambda b, h, i: (h, 0, 0)),
                  pl.BlockSpec((2 * blk, blk), lambda b, h, i: (0, 0))],
        out_specs=qspec,
        out_shape=jax.ShapeDtypeStruct((nb, s, d), BF16),
        scratch_shapes=[pltpu.VMEM((heads, blk, blk), F32), pltpu.VMEM((heads, blk, blk), F32),
                        pltpu.VMEM((heads, blk, 1), F32), pltpu.VMEM((heads, blk, 1), F32),
                        pltpu.VMEM((heads, blk, HEAD_DIM), F32)],
        compiler_params=_cparams(("arbitrary", "arbitrary", "arbitrary")),
        name="sb_prompt",
    )(q, k, v, bias, _suffix_matrix(blk))


def _sb_paged_body(pt_ref, *refs, pages_per_step, n_q, scale):
    kp = refs[:pages_per_step]
    vp = refs[pages_per_step:2 * pages_per_step]
    wq_ref, bias_ref, knew_ref, vnew_ref, tri_ref, o_ref, acc_ref, run_ref = refs[2 * pages_per_step:]
    step = pl.program_id(1)
    wq = wq_ref[0]
    bias = bias_ref[...]

    def heads_to_lanes(ref):
        n_pos = ref.shape[1] // N_HEADS
        return jnp.concatenate(
            [ref[0, pl.ds(h, n_pos, stride=N_HEADS), :].astype(BF16) for h in range(N_HEADS)], axis=1)

    def block(kblk, vblk, mask):
        z = jnp.dot(kblk, wq, preferred_element_type=F32) * scale + bias
        sp = _softplus2(z)
        lm = -sp if mask is None else jnp.where(mask, -sp, 0.0)
        hi, lo = _split_bf16(lm)
        surv = jnp.dot(tri_ref[...], jnp.concatenate([hi, lo], axis=0),
                       preferred_element_type=F32) + run_ref[...]
        a = jnp.exp2(z - sp + surv)
        if mask is not None:
            a = jnp.where(mask, a, 0.0)
        acc_ref[...] += jnp.dot(a.T.astype(BF16), vblk, preferred_element_type=F32)
        run_ref[...] += jnp.sum(lm, axis=0, keepdims=True)

    @pl.when(step == 0)
    def _():
        acc_ref[...] = jnp.zeros_like(acc_ref)
        run_ref[...] = jnp.zeros_like(run_ref)
        shape = (knew_ref.shape[1], wq.shape[1])
        kpos = lax.broadcasted_iota(jnp.int32, shape, 0)
        qpos = lax.broadcasted_iota(jnp.int32, shape, 1) % n_q
        block(knew_ref[0].astype(BF16), vnew_ref[0].astype(BF16), kpos < qpos)

    for i in range(pages_per_step):
        block(heads_to_lanes(kp[i]), heads_to_lanes(vp[i]), None)

    @pl.when(step == pl.num_programs(1) - 1)
    def _():
        o_ref[0] = acc_ref[...]


def _suffix_matrix_t(n):
    m = (np.arange(n)[None, :] > np.arange(n)[:, None]).astype(np.float32)
    return jnp.asarray(np.concatenate([m, m], axis=1), dtype=BF16)


def _sb_paged(q, k_new, v_new, cache_k, cache_v, page_table, logit_bias, pages_per_step):
    nb, n_q, d = q.shape
    n_pool, page = cache_k.shape[:2]
    n_pages = page_table.shape[1]
    ck = cache_k.reshape(n_pool, page * N_HEADS, HEAD_DIM)
    cv = cache_v.reshape(n_pool, page * N_HEADS, HEAD_DIM)
    ncol = LANES
    assert N_HEADS * n_q <= ncol and n_q <= page
    qh = q.reshape(nb, n_q, N_HEADS, HEAD_DIM)
    eye = jnp.eye(N_HEADS, dtype=BF16)
    wq = jnp.einsum('bqhd,hg->bhdgq', qh, eye).reshape(nb, d, N_HEADS * n_q)
    wq = jnp.pad(wq, ((0, 0), (0, 0), (0, ncol - N_HEADS * n_q)))
    bias = jnp.pad(jnp.repeat(logit_bias.astype(F32) * LOG2E, n_q), (0, ncol - N_HEADS * n_q)).reshape(1, ncol)
    knew = jnp.pad(k_new, ((0, 0), (0, page - n_q), (0, 0)))
    vnew = jnp.pad(v_new, ((0, 0), (0, page - n_q), (0, 0)))
    pt = page_table.reshape(-1).astype(jnp.int32)

    def page_map(i):
        return lambda b, s, pt_ref: (pt_ref[b * n_pages + n_pages - 1 - (s * pages_per_step + i)], 0, 0)

    page_specs = [pl.BlockSpec((1, page * N_HEADS, HEAD_DIM), page_map(i)) for i in range(pages_per_step)]
    grid_spec = pltpu.PrefetchScalarGridSpec(
        num_scalar_prefetch=1,
        grid=(nb, n_pages // pages_per_step),
        in_specs=page_specs + page_specs + [
            pl.BlockSpec((1, d, ncol), lambda b, s, pt_ref: (b, 0, 0)),
            pl.BlockSpec((1, ncol), lambda b, s, pt_ref: (0, 0)),
            pl.BlockSpec((1, page, d), lambda b, s, pt_ref: (b, 0, 0)),
            pl.BlockSpec((1, page, d), lambda b, s, pt_ref: (b, 0, 0)),
            pl.BlockSpec((page, 2 * page), lambda b, s, pt_ref: (0, 0))],
        out_specs=pl.BlockSpec((1, ncol, d), lambda b, s, pt_ref: (b, 0, 0)),
        scratch_shapes=[pltpu.VMEM((ncol, d), F32), pltpu.VMEM((1, ncol), F32)])
    return pl.pallas_call(
        functools.partial(_sb_paged_body, pages_per_step=pages_per_step, n_q=n_q,
                          scale=SB_SCALE2),
        grid_spec=grid_spec,
        out_shape=jax.ShapeDtypeStruct((nb, ncol, d), F32),
        compiler_params=_cparams(("arbitrary", "arbitrary")),
        name="sb_paged",
    )(pt, *([ck] * pages_per_step), *([cv] * pages_per_step), wq, bias, knew, vnew,
      _suffix_matrix_t(page))


def _oproj_body(o_ref, x_ref, gt_ref, w_ref, y_ref):
    mix = jnp.dot(o_ref[0], w_ref[...], preferred_element_type=F32)
    y_ref[0] = x_ref[0] + gt_ref[0] * mix


def _oproj(o, x, gate, w_o, rows):
    nb, s, d = x.shape
    xspec = pl.BlockSpec((1, rows, d), lambda b, i: (b, i, 0))
    return pl.pallas_call(
        _oproj_body,
        grid=(nb, s // rows),
        in_specs=[xspec, xspec, _mod_spec(gate, rows), _const_spec(w_o.shape)],
        out_specs=xspec,
        out_shape=jax.ShapeDtypeStruct((nb, s, d), F32),
        compiler_params=_cparams(("arbitrary", "arbitrary")),
        name="attn_out_proj",
    )(o, x, gate, w_o)


def _mix_weights_prompt(w_s, b_s, width):
    causal = jnp.tril(jnp.ones((CHUNK, CHUNK), dtype=bool))
    wmix = jnp.where(causal[None], w_s, 0.0).astype(BF16)
    bmix = jnp.repeat(b_s.T, width // GROUPS, axis=1)
    return wmix, bmix


def _mix_weights_sample(w_s, b_s, width, n_seq, n_tok):
    causal = jnp.tril(jnp.ones((n_tok, n_tok), dtype=bool))
    small = jnp.where(causal[None], w_s[:, :n_tok, :n_tok], 0.0)
    eye = jnp.eye(n_seq, dtype=F32)
    blockdiag = jnp.einsum('ab,gts->gatbs', eye, small).reshape(GROUPS, n_seq * n_tok, n_seq * n_tok)
    padn = SAMPLE_ROWS - n_seq * n_tok
    wmix = jnp.pad(blockdiag, ((0, 0), (0, padn), (0, padn))).astype(BF16)
    brow = jnp.tile(b_s.T[:n_tok], (n_seq, 1))
    bmix = jnp.repeat(jnp.pad(brow, ((0, padn), (0, 0))), width // GROUPS, axis=1)
    return wmix, bmix


def kernel(x_prompt, x_sample, cache_k, cache_v, page_table, c_prompt, c_sample, mod_w, mod_b, norm_mix_g, norm_ffn_g, a_w_in, a_b_in, a_v_norm_g, a_w_s, a_b_s, a_w_out, kv_mod_w, kv_mod_b, kv_norm_g, kv_w, k_norm_g, b_w_q, b_q_norm_g, b_logit_bias, b_w_o, peer_w_q, peer_subkeys, peer_u, peer_v):
    nbp, seq, d = x_prompt.shape
    nbs, n_tok, _ = x_sample.shape
    n_samp = nbs * n_tok
    width = a_w_out.shape[1]

    c_all = jnp.concatenate([c_prompt, c_sample], axis=0)
    c_all = jnp.pad(c_all, ((0, BF16_ROWS - c_all.shape[0]), (0, 0)))
    mods = _adaln(c_all, mod_w, mod_b)
    kvmod = _adaln(c_all, kv_mod_w[None], kv_mod_b[None])[0]

    def split_mod(m, n):
        parts = jnp.split(m, n, axis=-1)
        prm = [p[:nbp, None, :] for p in parts]
        smp = [jnp.pad(jnp.repeat(p[nbp:nbp + nbs], n_tok, axis=0), ((0, SAMPLE_ROWS - n_samp), (0, 0)))[None]
               for p in parts]
        return prm, smp

    mod_p, mod_s = zip(*[split_mod(mods[l], 6) for l in range(mods.shape[0])])
    kvmod_p, kvmod_s = split_mod(kvmod, 2)

    w_in = a_w_in[0].astype(BF16)
    w_out = a_w_out[0].astype(BF16)
    wq_t = [jnp.transpose(peer_w_q[l]).astype(BF16) for l in range(2)]
    subk = [peer_subkeys[l].astype(BF16) for l in range(2)]
    u_tab = peer_u.astype(BF16)
    eb = 512
    n_l, n_exp, _ = peer_v.shape
    vt_tab = jnp.transpose(peer_v.reshape(n_l, n_exp // eb, eb, d), (0, 1, 3, 2)).astype(BF16)
    w_kv = kv_w.astype(BF16)
    w_q1 = b_w_q[0].astype(BF16)
    w_o1 = b_w_o[0].astype(BF16)
    k_gain = jnp.tile(k_norm_g, N_HEADS).reshape(1, d)
    q_gain = jnp.tile(b_q_norm_g[0], N_HEADS).reshape(1, d)
    wmix_p, bmix_p = _mix_weights_prompt(a_w_s[0], a_b_s[0], width)
    wmix_s, bmix_s = _mix_weights_sample(a_w_s[0], a_b_s[0], width, nbs, n_tok)

    xs = jnp.pad(x_sample.reshape(1, n_samp, d), ((0, 0), (0, SAMPLE_ROWS - n_samp), (0, 0)))

    def trunk(x, mod, kvm, wmix, bmix, rows, tb, attend):
        sh_m, sc_m, g_m, sh_f, sc_f, g_f = mod[0]
        x1, v_rows = _mixer_a(x, sh_m, sc_m, g_m, norm_mix_g[0], w_in, a_b_in[0], a_v_norm_g[0],
                              wmix, bmix, w_out, rows)
        x2 = _peer(x1, sh_f, sc_f, g_f, norm_ffn_g[0], wq_t[0], subk[0], u_tab, vt_tab, 0, tb, eb)
        sh_m, sc_m, g_m, sh_f, sc_f, g_f = mod[1]
        k, v, kb, vb, q = _kvq(x2, kvm[0], kvm[1], sh_m, sc_m, kv_norm_g, norm_mix_g[1], w_kv, w_q1,
                               k_gain, q_gain, rows)
        o = attend(q, k, v, kb, vb)
        x3 = _oproj(o, x2, g_m, w_o1, rows)
        y = _peer(x3, sh_f, sc_f, g_f, norm_ffn_g[1], wq_t[1], subk[1], u_tab, vt_tab, 1, tb, eb)
        return y, k, v, v_rows

    def attend_prompt(q, k, v, kb, vb):
        return _sb_prompt(q, kb, vb, b_logit_bias[0], 256, 4)

    def attend_sample(q, k, v, kb, vb):
        qs = q[0, :n_samp].reshape(nbs, n_tok, d)
        ks = k[0, :n_samp].reshape(nbs, n_tok, d)
        vs = v[0, :n_samp].reshape(nbs, n_tok, d)
        full = _sb_paged(qs, ks, vs, cache_k, cache_v, page_table, b_logit_bias[0], 4)
        full = full[:, :N_HEADS * n_tok].reshape(nbs, N_HEADS, n_tok, N_HEADS, HEAD_DIM)
        idx = jnp.arange(N_HEADS)
        o = full[:, idx, :, idx, :]
        o = jnp.transpose(o, (1, 2, 0, 3)).reshape(1, n_samp, d)
        return jnp.pad(o, ((0, 0), (0, SAMPLE_ROWS - n_samp), (0, 0))).astype(BF16)

    y_p, k_p, v_p, _ = trunk(x_prompt, mod_p, kvmod_p, wmix_p, bmix_p, 256, 512, attend_prompt)
    y_s, k_s, v_s, gv_s = trunk(xs, mod_s, kvmod_s, wmix_s, bmix_s, SAMPLE_ROWS, SAMPLE_ROWS, attend_sample)

    def samp(a, shape):
        return a[0, :n_samp].reshape(shape)

    return (y_p,
            samp(y_s, (nbs, n_tok, d)),
            k_p.reshape(nbp, seq, N_HEADS, HEAD_DIM),
            v_p.reshape(nbp, seq, N_HEADS, HEAD_DIM),
            samp(k_s, (nbs, n_tok, N_HEADS, HEAD_DIM)),
            samp(v_s, (nbs, n_tok, N_HEADS, HEAD_DIM)),
            samp(gv_s, (1, nbs, n_tok, width)))
```

```python
import functools
import math

import jax
import jax.numpy as jnp
import numpy as np
from jax import lax
from jax.experimental import pallas as pl
from jax.experimental.pallas import tpu as pltpu

F32 = jnp.float32
BF16 = jnp.bfloat16
EPS = 1e-6

LANES = 128
SUBLANES = 8
BF16_ROWS = 16
VMEM_LIMIT = 56 * 1024 * 1024

CHUNK = 128
GROUPS = 16
N_HEADS = 16
HEAD_DIM = 128
PEER_HEADS = 8
PEER_NKEYS = 128
PEER_TOPK = 16
SAMPLE_ROWS = 128


def _cparams(sem):
    return pltpu.CompilerParams(dimension_semantics=sem, vmem_limit_bytes=VMEM_LIMIT)


def _const_spec(shape):
    nd = len(shape)
    return pl.BlockSpec(shape, lambda *_: (0,) * nd, pipeline_mode=pl.Buffered(1))


def _rms(x, g):
    ms = jnp.mean(x * x, axis=-1, keepdims=True)
    return x * lax.rsqrt(ms + EPS) * g


def _gelu(x):
    return 0.5 * x * (1.0 + lax.erf(x * np.float32(math.sqrt(0.5))))


def _head_rms(x, g):
    cols = []
    for h in range(x.shape[1] // HEAD_DIM):
        seg = x[:, h * HEAD_DIM:(h + 1) * HEAD_DIM]
        ms = jnp.mean(seg * seg, axis=-1, keepdims=True)
        cols.append(seg * lax.rsqrt(ms + EPS))
    return jnp.concatenate(cols, axis=1) * g


def _adaln_body(c_ref, w_ref, b_ref, o_ref):
    c = c_ref[...]
    s = c / (1.0 + jnp.exp(-c))
    o_ref[0] = jnp.dot(s.astype(BF16), w_ref[0].astype(BF16), preferred_element_type=F32) + b_ref[0]


def _adaln(c, w, b):
    n_l, d, n = w.shape
    m = c.shape[0]
    tn = 1024
    return pl.pallas_call(
        _adaln_body,
        grid=(n_l, n // tn),
        in_specs=[pl.BlockSpec((m, d), lambda l, j: (0, 0)),
                  pl.BlockSpec((1, d, tn), lambda l, j: (l, 0, j)),
                  pl.BlockSpec((1, 1, tn), lambda l, j: (l, 0, j))],
        out_specs=pl.BlockSpec((1, m, tn), lambda l, j: (l, 0, j)),
        out_shape=jax.ShapeDtypeStruct((n_l, m, n), F32),
        compiler_params=_cparams(("arbitrary", "arbitrary")),
        name="adaln",
    )(c, w, b.reshape(n_l, 1, n))


def _mod_spec(mod, rows):
    d = mod.shape[-1]
    if mod.shape[1] == 1:
        return pl.BlockSpec((1, 1, d), lambda b, i: (b, 0, 0))
    return pl.BlockSpec((1, rows, d), lambda b, i: (b, i, 0))


def _mixer_a_body(x_ref, sh_ref, sc_ref, gt_ref, ng_ref, win_ref, bin_ref, vg_ref, wmix_ref,
                  bmix_ref, wout_ref, x1_ref, v_ref, *, chunk):
    x = x_ref[0]
    rows = x.shape[0]
    h = _rms(x, ng_ref[...]) * (1.0 + sc_ref[0]) + sh_ref[0]
    z = jnp.dot(h.astype(BF16), win_ref[...], preferred_element_type=F32) + bin_ref[...]
    z = _gelu(z)
    width = z.shape[1] // 2
    u = z[:, :width]
    v = _rms(z[:, width:], vg_ref[...])
    v_ref[0] = v
    vb = v.astype(BF16)
    gd = width // GROUPS
    blocks = []
    for c in range(rows // chunk):
        cols = [jnp.dot(wmix_ref[g], vb[c * chunk:(c + 1) * chunk, g * gd:(g + 1) * gd],
                        preferred_element_type=F32) for g in range(GROUPS)]
        blocks.append(jnp.concatenate(cols, axis=1) + bmix_ref[...])
    mixed = blocks[0] if len(blocks) == 1 else jnp.concatenate(blocks, axis=0)
    um = (u * mixed).astype(BF16)
    mix = jnp.dot(um, wout_ref[...], preferred_element_type=F32)
    x1_ref[0] = x + gt_ref[0] * mix


def _mixer_a(x, shift, scale, gate, norm_g, w_in, b_in, v_g, wmix, bmix, w_out, rows):
    nb, s, d = x.shape
    width = w_out.shape[0]
    chunk = wmix.shape[1]
    xspec = pl.BlockSpec((1, rows, d), lambda b, i: (b, i, 0))
    return pl.pallas_call(
        functools.partial(_mixer_a_body, chunk=chunk),
        grid=(nb, s // rows),
        in_specs=[xspec, _mod_spec(shift, rows), _mod_spec(scale, rows), _mod_spec(gate, rows),
                  _const_spec((1, d)), _const_spec(w_in.shape), _const_spec((1, 2 * width)),
                  _const_spec((1, width)), _const_spec(wmix.shape), _const_spec(bmix.shape),
                  _const_spec(w_out.shape)],
        out_specs=[xspec, pl.BlockSpec((1, rows, width), lambda b, i: (b, i, 0))],
        out_shape=[jax.ShapeDtypeStruct((nb, s, d), F32), jax.ShapeDtypeStruct((nb, s, width), F32)],
        compiler_params=_cparams(("arbitrary", "arbitrary")),
        name="gmlp_mixer",
    )(x, shift, scale, gate, norm_g.reshape(1, d), w_in, b_in.reshape(1, -1), v_g.reshape(1, -1),
      wmix, bmix, w_out)


def _sort_network(n):
    pairs = []
    p = 1
    while p < n:
        k = p
        while k >= 1:
            for j in range(k % p, n - k, 2 * k):
                for i in range(min(k, n - j - k)):
                    if (i + j) // (2 * p) == (i + j + k) // (2 * p):
                        pairs.append((i + j, i + j + k))
            k //= 2
        p *= 2
    return pairs


_SORT16 = _sort_network(PEER_TOPK)
_BITONIC16 = [(i, i | d) for d in (8, 4, 2, 1) for i in range(PEER_TOPK) if not i & d]


def _apply_network(xs, pairs):
    xs = list(xs)
    for i, j in pairs:
        a, b = xs[i], xs[j]
        xs[i] = jnp.maximum(a, b)
        xs[j] = jnp.minimum(a, b)
    return xs


def _merge_top16(xs, ys):
    zs = [jnp.maximum(xs[r], ys[PEER_TOPK - 1 - r]) for r in range(PEER_TOPK)]
    return _apply_network(zs, _BITONIC16)


def _top16_of_keys(s):
    slabs = [s[SUBLANES * g:SUBLANES * (g + 1), :] for g in range(PEER_NKEYS // SUBLANES)]
    xs = _apply_network(slabs, _SORT16)
    for shift in (4, 2, 1):
        ys = [pltpu.roll(x, shift, axis=0) for x in xs]
        xs = _merge_top16(xs, ys)
    return xs


_CAND = [(r, c) for r in range(PEER_TOPK) for c in range(PEER_TOPK) if (r + 1) * (c + 1) <= PEER_TOPK]


def _route_body(x_ref, sh_ref, sc_ref, ng_ref, wq_ref, sk_ref,
                h2t_ref, r2_ref, e2_ref, c1_ref, e1_ref, s_ref):
    x = x_ref[0]
    h = _rms(x, ng_ref[...]) * (1.0 + sc_ref[0]) + sh_ref[0]
    ht = h.T.astype(BF16)
    h2t_ref[...] = ht
    qt = jnp.dot(wq_ref[...], ht, preferred_element_type=F32).astype(BF16)
    for hp in range(2 * PEER_HEADS):
        s_ref[hp] = jnp.dot(sk_ref[hp % 2], qt[hp * PEER_NKEYS:(hp + 1) * PEER_NKEYS, :],
                            preferred_element_type=F32)
    tb = x.shape[0]
    neg = jnp.full((SUBLANES, LANES), -jnp.inf, F32)
    sub = lax.broadcasted_iota(jnp.int32, (SUBLANES, LANES), 0)

    def chunk(lc, carry):
        lanes = pl.ds(pl.multiple_of(lc * LANES, LANES), LANES)
        top = [[neg] * PEER_TOPK, [neg] * PEER_TOPK]
        for hp in range(2 * PEER_HEADS):
            hh, p = hp // 2, hp % 2
            xs = _top16_of_keys(s_ref[hp, :, lanes])
            top[p] = [jnp.where(sub == hh, xs[r], top[p][r]) for r in range(PEER_TOPK)]
        a, b = top
        cand = {rc: a[rc[0]] + b[rc[1]] for rc in _CAND}
        g1 = [cand[(0, c)] for c in range(PEER_TOPK)]
        rest = [cand[rc] for rc in _CAND if rc[0] > 0]
        rest = rest + [neg] * (3 * PEER_TOPK - len(rest))
        gs = [_apply_network(rest[PEER_TOPK * k:PEER_TOPK * (k + 1)], _SORT16) for k in range(3)]
        m1 = _merge_top16(g1, gs[0])
        m2 = _merge_top16(gs[1], gs[2])
        zs = [jnp.maximum(m1[r], m2[PEER_TOPK - 1 - r]) for r in range(PEER_TOPK)]
        tau = functools.reduce(jnp.minimum, zs)
        top_sum = a[0] + b[0]
        zsum = jnp.zeros((SUBLANES, LANES), F32)
        for rc in _CAND:
            zsum = zsum + jnp.where(cand[rc] >= tau, jnp.exp(cand[rc] - top_sum), 0.0)
        zinv = 1.0 / zsum
        for hh in range(PEER_HEADS):
            def row(v):
                return jnp.broadcast_to(v[hh:hh + 1, :], (SUBLANES, LANES))
            bh = [row(b[c]) for c in range(PEER_TOPK)]
            tau_h, zinv_h, a0_h = row(tau), row(zinv), row(a[0])
            for gp in range(PEER_NKEYS // BF16_ROWS):
                r2s, e2s, c1s, e1s = [], [], [], []
                for g in (2 * gp, 2 * gp + 1):
                    rows = slice(SUBLANES * g, SUBLANES * (g + 1))
                    s1 = s_ref[2 * hh, rows, lanes]
                    s2 = s_ref[2 * hh + 1, rows, lanes]
                    r2 = jnp.zeros((SUBLANES, LANES), F32)
                    c1 = jnp.zeros((SUBLANES, LANES), F32)
                    for c in range(PEER_TOPK):
                        r2 = r2 + jnp.where(bh[c] > s2, 1.0, 0.0)
                        c1 = c1 + jnp.where(s1 + bh[c] >= tau_h, 1.0, 0.0)
                    r2s.append(r2)
                    c1s.append(c1)
                    e2s.append(jnp.exp(s2 - bh[0]))
                    e1s.append(jnp.exp(s1 - a0_h) * zinv_h)
                rows16 = slice(BF16_ROWS * gp, BF16_ROWS * (gp + 1))
                r2_ref[hh, rows16, lanes] = jnp.concatenate(r2s, axis=0).astype(BF16)
                e2_ref[hh, rows16, lanes] = jnp.concatenate(e2s, axis=0).astype(BF16)
                c1_ref[hh, rows16, lanes] = jnp.concatenate(c1s, axis=0)
                e1_ref[hh, rows16, lanes] = jnp.concatenate(e1s, axis=0)
        return carry

    lax.fori_loop(0, tb // LANES, chunk, 0)


def _peer_route(x, shift, scale, norm_g, wq_t, subkeys, tb):
    nb, s, d = x.shape
    nblk = s // tb
    nt = nb * nblk
    tspec3 = pl.BlockSpec((None, PEER_HEADS, PEER_NKEYS, tb), lambda b, i: (b * nblk + i, 0, 0, 0))
    return pl.pallas_call(
        _route_body,
        grid=(nb, nblk),
        in_specs=[pl.BlockSpec((1, tb, d), lambda b, i: (b, i, 0)), _mod_spec(shift, tb), _mod_spec(scale, tb),
                  _const_spec((1, d)), _const_spec(wq_t.shape), _const_spec(subkeys.shape)],
        out_specs=[pl.BlockSpec((None, d, tb), lambda b, i: (b * nblk + i, 0, 0)), tspec3, tspec3, tspec3, tspec3],
        out_shape=[jax.ShapeDtypeStruct((nt, d, tb), BF16),
                   jax.ShapeDtypeStruct((nt, PEER_HEADS, PEER_NKEYS, tb), BF16),
                   jax.ShapeDtypeStruct((nt, PEER_HEADS, PEER_NKEYS, tb), BF16),
                   jax.ShapeDtypeStruct((nt, PEER_HEADS, PEER_NKEYS, tb), F32),
                   jax.ShapeDtypeStruct((nt, PEER_HEADS, PEER_NKEYS, tb), F32)],
        scratch_shapes=[pltpu.VMEM((2 * PEER_HEADS, PEER_NKEYS, tb), F32)],
        compiler_params=_cparams(("arbitrary", "arbitrary")),
        name="peer_route",
    )(x, shift, scale, norm_g.reshape(1, d), wq_t, subkeys)


def _peer_gate_stage(act_ref, p_ref, r2_ref, e2_ref, c1_ref, e1_ref, key0, n_i, valid=None):
    n_sub, _, tb = act_ref.shape
    for j in range(n_sub):
        for il in range(n_i):
            c1i = c1_ref[j, key0 + il]
            e1i = e1_ref[j, key0 + il]
            for lc in range(tb // LANES):
                lanes = slice(lc * LANES, (lc + 1) * LANES)
                gate = [jnp.zeros((BF16_ROWS, LANES), BF16)] * (PEER_NKEYS // BF16_ROWS)
                for hh in range(PEER_HEADS):
                    c1b = jnp.broadcast_to(c1i[hh:hh + 1, lanes], (BF16_ROWS, LANES)).astype(BF16)
                    e1b = jnp.broadcast_to(e1i[hh:hh + 1, lanes], (BF16_ROWS, LANES)).astype(BF16)
                    for rb in range(PEER_NKEYS // BF16_ROWS):
                        rows = slice(rb * BF16_ROWS, (rb + 1) * BF16_ROWS)
                        r2 = r2_ref[j, hh, rows, lanes]
                        e2 = e2_ref[j, hh, rows, lanes]
                        gate[rb] = gate[rb] + jnp.where(r2 < c1b, e2, jnp.zeros_like(e2)) * e1b
                for rb in range(PEER_NKEYS // BF16_ROWS):
                    rows = slice(il * PEER_NKEYS + rb * BF16_ROWS, il * PEER_NKEYS + (rb + 1) * BF16_ROWS)
                    a = act_ref[j, rows, lanes]
                    p = (_gelu(a) * gate[rb].astype(F32)).astype(BF16)
                    p_ref[j, rows, lanes] = p if valid is None else jnp.where(valid, p, jnp.zeros_like(p))


def _peer_body(h2t_ref, u_ref, vta_ref, vtb_ref, r2_ref, e2_ref, c1_ref, e1_ref, acc_ref,
               acta_ref, actb_ref, pa_ref, pb_ref, *, n_i):
    s = pl.program_id(1)
    last = pl.num_programs(1) - 1
    n_blocks = 2 * last
    n_sub, eb, _ = acta_ref.shape
    routing = (r2_ref, e2_ref, c1_ref, e1_ref)

    @pl.when(s == 0)
    def _():
        acc_ref[...] = jnp.zeros_like(acc_ref)
        actb_ref[...] = jnp.zeros_like(actb_ref)

    for j in range(n_sub):
        acta_ref[j] = jnp.dot(u_ref[:eb, :], h2t_ref[j], preferred_element_type=F32)
    _peer_gate_stage(actb_ref, pb_ref, *routing, jnp.maximum(2 * s - 1, 0) * n_i, n_i)
    for j in range(n_sub):
        acc_ref[j] += jnp.dot(vta_ref[...], pb_ref[j], preferred_element_type=F32)
    for j in range(n_sub):
        actb_ref[j] = jnp.dot(u_ref[eb:, :], h2t_ref[j], preferred_element_type=F32)
    _peer_gate_stage(acta_ref, pa_ref, *routing, jnp.minimum(2 * s, n_blocks - 1) * n_i, n_i, valid=s < last)
    for j in range(n_sub):
        acc_ref[j] += jnp.dot(vtb_ref[...], pa_ref[j], preferred_element_type=F32)


def _peer_experts(h2t, u_tab, vt_tab, layer, r2, e2, c1, e1, n_sub, eb):
    nt, d, tb = h2t.shape
    n_exp = u_tab.shape[1]
    n_i = eb // PEER_NKEYS
    n_pairs = n_exp // (2 * eb)
    assert vt_tab.shape[1:] == (2 * n_pairs, d, eb) and nt % n_sub == 0
    one = pl.Buffered(1)
    t3 = pl.BlockSpec((n_sub, PEER_HEADS, PEER_NKEYS, tb), lambda i, e: (i, 0, 0, 0), pipeline_mode=one)
    k3 = pl.BlockSpec((n_sub, PEER_NKEYS, PEER_HEADS, tb), lambda i, e: (i, 0, 0, 0), pipeline_mode=one)
    tile3 = pl.BlockSpec((n_sub, d, tb), lambda i, e: (i, 0, 0))
    return pl.pallas_call(
        functools.partial(_peer_body, n_i=n_i),
        grid=(nt // n_sub, n_pairs + 1),
        in_specs=[tile3,
                  pl.BlockSpec((None, 2 * eb, d), lambda i, e: (layer, jnp.minimum(e, n_pairs - 1), 0)),
                  pl.BlockSpec((None, None, d, eb), lambda i, e: (layer, jnp.maximum(2 * e - 1, 0), 0, 0)),
                  pl.BlockSpec((None, None, d, eb), lambda i, e: (layer, jnp.minimum(2 * e, 2 * n_pairs - 1), 0, 0)),
                  t3, t3, k3, k3],
        out_specs=tile3,
        out_shape=jax.ShapeDtypeStruct((nt, d, tb), F32),
        scratch_shapes=[pltpu.VMEM((n_sub, eb, tb), F32), pltpu.VMEM((n_sub, eb, tb), F32),
                        pltpu.VMEM((n_sub, eb, tb), BF16), pltpu.VMEM((n_sub, eb, tb), BF16)],
        compiler_params=_cparams(("arbitrary", "arbitrary")),
        name="peer_experts",
    )(h2t, u_tab, vt_tab, vt_tab, r2, e2, c1, e1)


def _peer_finish_body(acc_ref, x_ref, g_ref, y_ref):
    y_ref[0] = x_ref[0] + g_ref[0] * acc_ref[0].T


def _peer_finish(acc_t, x, gate):
    nb, s, d = x.shape
    tb = acc_t.shape[2]
    nblk = s // tb
    xspec = pl.BlockSpec((1, tb, d), lambda b, i: (b, i, 0))
    return pl.pallas_call(
        _peer_finish_body,
        grid=(nb, nblk),
        in_specs=[pl.BlockSpec((1, d, tb), lambda b, i: (b * nblk + i, 0, 0)), xspec, _mod_spec(gate, tb)],
        out_specs=xspec,
        out_shape=jax.ShapeDtypeStruct((nb, s, d), F32),
        compiler_params=_cparams(("arbitrary", "arbitrary")),
        name="peer_finish",
    )(acc_t, x, gate)


def _peer(x, shift, scale, gate, norm_g, wq_t, subkeys, u_tab, vt_tab, layer, tb, n_sub, eb):
    h2t, r2, e2, c1, e1 = _peer_route(x, shift, scale, norm_g, wq_t, subkeys, tb)
    c1 = jnp.transpose(c1, (0, 2, 1, 3))
    e1 = jnp.transpose(e1, (0, 2, 1, 3))
    acc_t = _peer_experts(h2t, u_tab, vt_tab, layer, r2, e2, c1, e1, n_sub, eb)
    return _peer_finish(acc_t, x, gate)


def _kvq_body(x_ref, ksh_ref, ksc_ref, msh_ref, msc_ref, kng_ref, mng_ref, wkv_ref, wq_ref,
              kg_ref, qg_ref, k_ref, v_ref, kb_ref, vb_ref, q_ref):
    x = x_ref[0]
    d = x.shape[1]
    ms = jnp.mean(x * x, axis=-1, keepdims=True)
    xn = x * lax.rsqrt(ms + EPS)
    hk = (xn * kng_ref[...]) * (1.0 + ksc_ref[0]) + ksh_ref[0]
    kv = jnp.dot(hk.astype(BF16), wkv_ref[...], preferred_element_type=F32)
    k = _head_rms(kv[:, :d], kg_ref[...])
    v = kv[:, d:]
    k_ref[0] = k
    v_ref[0] = v
    kb_ref[0] = k.astype(BF16)
    vb_ref[0] = v.astype(BF16)
    hq = (xn * mng_ref[...]) * (1.0 + msc_ref[0]) + msh_ref[0]
    q = jnp.dot(hq.astype(BF16), wq_ref[...], preferred_element_type=F32)
    q_ref[0] = _head_rms(q, qg_ref[...]).astype(BF16)


def _kvq(x, k_shift, k_scale, m_shift, m_scale, kv_norm_g, mix_norm_g, w_kv, w_q, k_gain, q_gain, rows):
    nb, s, d = x.shape
    xspec = pl.BlockSpec((1, rows, d), lambda b, i: (b, i, 0))
    return pl.pallas_call(
        _kvq_body,
        grid=(nb, s // rows),
        in_specs=[xspec, _mod_spec(k_shift, rows), _mod_spec(k_scale, rows), _mod_spec(m_shift, rows),
                  _mod_spec(m_scale, rows), _const_spec((1, d)), _const_spec((1, d)),
                  _const_spec(w_kv.shape), _const_spec(w_q.shape), _const_spec((1, d)), _const_spec((1, d))],
        out_specs=[xspec] * 5,
        out_shape=[jax.ShapeDtypeStruct((nb, s, d), F32), jax.ShapeDtypeStruct((nb, s, d), F32),
                   jax.ShapeDtypeStruct((nb, s, d), BF16), jax.ShapeDtypeStruct((nb, s, d), BF16),
                   jax.ShapeDtypeStruct((nb, s, d), BF16)],
        compiler_params=_cparams(("arbitrary", "arbitrary")),
        name="kv_q_proj",
    )(x, k_shift, k_scale, m_shift, m_scale, kv_norm_g.reshape(1, d), mix_norm_g.reshape(1, d),
      w_kv, w_q, k_gain, q_gain)


LOG2E = np.float32(math.log2(math.e))
SB_SCALE2 = np.float32(HEAD_DIM ** -0.5 * math.log2(math.e))


def _softplus2(z2):
    return jnp.maximum(z2, 0.0) + jnp.log2(1.0 + jnp.exp2(-jnp.abs(z2)))


def _split_bf16(x):
    hi = x.astype(BF16)
    lo = (x - hi.astype(F32)).astype(BF16)
    return hi, lo


def _sb_prompt_body(q_ref, k_ref, v_ref, bias_ref, tri_ref, o_ref,
                    d_ref, surv_ref, lms_ref, run_ref, acc_ref, *, blk, heads, scale):
    qi = pl.program_id(2)
    hsl = [slice(h * HEAD_DIM, (h + 1) * HEAD_DIM) for h in range(heads)]
    hr = range(heads)

    def logits(kb):
        ks = pl.multiple_of(kb * blk, blk)
        return [lax.dot_general(q_ref[0, :, hsl[h]], k_ref[0, pl.ds(ks, blk), hsl[h]],
                                (((1,), (1,)), ((), ())), preferred_element_type=F32) * scale
                + bias_ref[0, h:h + 1, :] for h in hr]

    def produce(z, mask):
        for h in hr:
            sp = _softplus2(z[h])
            d, lm = z[h] - sp, -sp
            if mask is not None:
                d, lm = jnp.where(mask, d, -jnp.inf), jnp.where(mask, lm, 0.0)
            hi, lo = _split_bf16(lm)
            surv_ref[h] = jnp.dot(jnp.concatenate([hi, lo], axis=1), tri_ref[...],
                                  preferred_element_type=F32)
            d_ref[h] = d
            lms_ref[h] = jnp.sum(lm, axis=1, keepdims=True)

    def consume(kb):
        ks = pl.multiple_of(kb * blk, blk)
        for h in hr:
            a = jnp.exp2(d_ref[h] + surv_ref[h] + run_ref[h])
            acc_ref[h] += jnp.dot(a.astype(BF16), v_ref[0, pl.ds(ks, blk), hsl[h]],
                                  preferred_element_type=F32)
            run_ref[h] += lms_ref[h]

    acc_ref[...] = jnp.zeros_like(acc_ref)
    run_ref[...] = jnp.zeros_like(run_ref)
    row = lax.broadcasted_iota(jnp.int32, (blk, blk), 0)
    col = lax.broadcasted_iota(jnp.int32, (blk, blk), 1)
    produce(logits(qi), col < row)

    def trip(j, carry):
        z = logits(qi - j - 1)
        consume(qi - j)
        produce(z, None)
        return carry

    lax.fori_loop(0, qi, trip, 0)
    consume(0)
    for h in hr:
        o_ref[0, :, hsl[h]] = acc_ref[h].astype(o_ref.dtype)


def _suffix_matrix(n):
    m = (np.arange(n)[:, None] > np.arange(n)[None, :]).astype(np.float32)
    return jnp.asarray(np.concatenate([m, m], axis=0), dtype=BF16)


def _sb_prompt(q, k, v, logit_bias, blk, heads):
    nb, s, d = q.shape
    bias = jnp.broadcast_to((logit_bias.astype(F32) * LOG2E).reshape(N_HEADS // heads, heads, 1),
                            (N_HEADS // heads, heads, blk))
    qspec = pl.BlockSpec((1, blk, heads * HEAD_DIM), lambda b, h, i: (b, i, h))
    kvspec = pl.BlockSpec((1, s, heads * HEAD_DIM), lambda b, h, i: (b, 0, h))
    return pl.pallas_call(
        functools.partial(_sb_prompt_body, blk=blk, heads=heads, scale=SB_SCALE2),
        grid=(nb, N_HEADS // heads, s // blk),
        in_specs=[qspec, kvspec, kvspec, pl.BlockSpec((1, heads, blk), lambda b, h, i: (h, 0, 0)),
                  pl.BlockSpec((2 * blk, blk), lambda b, h, i: (0, 0))],
        out_specs=qspec,
        out_shape=jax.ShapeDtypeStruct((nb, s, d), BF16),
        scratch_shapes=[pltpu.VMEM((heads, blk, blk), F32), pltpu.VMEM((heads, blk, blk), F32),
                        pltpu.VMEM((heads, blk, 1), F32), pltpu.VMEM((heads, blk, 1), F32),
                        pltpu.VMEM((heads, blk, HEAD_DIM), F32)],
        compiler_params=_cparams(("arbitrary", "arbitrary", "arbitrary")),
        name="sb_prompt",
    )(q, k, v, bias, _suffix_matrix(blk))


def _sb_paged_body(pt_ref, *refs, pages_per_step, n_q, scale):
    kp = refs[:pages_per_step]
    vp = refs[pages_per_step:2 * pages_per_step]
    wq_ref, bias_ref, knew_ref, vnew_ref, tri_ref, o_ref, acc_ref, run_ref = refs[2 * pages_per_step:]
    step = pl.program_id(1)
    wq = wq_ref[0]
    bias = bias_ref[...]

    def heads_to_lanes(ref):
        n_pos = ref.shape[1] // N_HEADS
        return jnp.concatenate(
            [ref[0, pl.ds(h, n_pos, stride=N_HEADS), :].astype(BF16) for h in range(N_HEADS)], axis=1)

    def block(kblk, vblk, mask):
        z = jnp.dot(kblk, wq, preferred_element_type=F32) * scale + bias
        sp = _softplus2(z)
        lm = -sp if mask is None else jnp.where(mask, -sp, 0.0)
        hi, lo = _split_bf16(lm)
        surv = jnp.dot(tri_ref[...], jnp.concatenate([hi, lo], axis=0),
                       preferred_element_type=F32) + run_ref[...]
        a = jnp.exp2(z - sp + surv)
        if mask is not None:
            a = jnp.where(mask, a, 0.0)
        acc_ref[...] += jnp.dot(a.T.astype(BF16), vblk, preferred_element_type=F32)
        run_ref[...] += jnp.sum(lm, axis=0, keepdims=True)

    @pl.when(step == 0)
    def _():
        acc_ref[...] = jnp.zeros_like(acc_ref)
        run_ref[...] = jnp.zeros_like(run_ref)
        shape = (knew_ref.shape[1], wq.shape[1])
        kpos = lax.broadcasted_iota(jnp.int32, shape, 0)
        qpos = lax.broadcasted_iota(jnp.int32, shape, 1) % n_q
        block(knew_ref[0].astype(BF16), vnew_ref[0].astype(BF16), kpos < qpos)

    for i in range(pages_per_step):
        block(heads_to_lanes(kp[i]), heads_to_lanes(vp[i]), None)

    @pl.when(step == pl.num_programs(1) - 1)
    def _():
        o_ref[0] = acc_ref[...]


def _suffix_matrix_t(n):
    m = (np.arange(n)[None, :] > np.arange(n)[:, None]).astype(np.float32)
    return jnp.asarray(np.concatenate([m, m], axis=1), dtype=BF16)


def _sb_paged(q, k_new, v_new, cache_k, cache_v, page_table, logit_bias, pages_per_step):
    nb, n_q, d = q.shape
    n_pool, page = cache_k.shape[:2]
    n_pages = page_table.shape[1]
    ck = cache_k.reshape(n_pool, page * N_HEADS, HEAD_DIM)
    cv = cache_v.reshape(n_pool, page * N_HEADS, HEAD_DIM)
    ncol = LANES
    assert N_HEADS * n_q <= ncol and n_q <= page
    qh = q.reshape(nb, n_q, N_HEADS, HEAD_DIM)
    eye = jnp.eye(N_HEADS, dtype=BF16)
    wq = jnp.einsum('bqhd,hg->bhdgq', qh, eye).reshape(nb, d, N_HEADS * n_q)
    wq = jnp.pad(wq, ((0, 0), (0, 0), (0, ncol - N_HEADS * n_q)))
    bias = jnp.pad(jnp.repeat(logit_bias.astype(F32) * LOG2E, n_q), (0, ncol - N_HEADS * n_q)).reshape(1, ncol)
    knew = jnp.pad(k_new, ((0, 0), (0, page - n_q), (0, 0)))
    vnew = jnp.pad(v_new, ((0, 0), (0, page - n_q), (0, 0)))
    pt = page_table.reshape(-1).astype(jnp.int32)

    def page_map(i):
        return lambda b, s, pt_ref: (pt_ref[b * n_pages + n_pages - 1 - (s * pages_per_step + i)], 0, 0)

    page_specs = [pl.BlockSpec((1, page * N_HEADS, HEAD_DIM), page_map(i)) for i in range(pages_per_step)]
    grid_spec = pltpu.PrefetchScalarGridSpec(
        num_scalar_prefetch=1,
        grid=(nb, n_pages // pages_per_step),
        in_specs=page_specs + page_specs + [
            pl.BlockSpec((1, d, ncol), lambda b, s, pt_ref: (b, 0, 0)),
            pl.BlockSpec((1, ncol), lambda b, s, pt_ref: (0, 0)),
            pl.BlockSpec((1, page, d), lambda b, s, pt_ref: (b, 0, 0)),
            pl.BlockSpec((1, page, d), lambda b, s, pt_ref: (b, 0, 0)),
            pl.BlockSpec((page, 2 * page), lambda b, s, pt_ref: (0, 0))],
        out_specs=pl.BlockSpec((1, ncol, d), lambda b, s, pt_ref: (b, 0, 0)),
        scratch_shapes=[pltpu.VMEM((ncol, d), F32), pltpu.VMEM((1, ncol), F32)])
    return pl.pallas_call(
        functools.partial(_sb_paged_body, pages_per_step=pages_per_step, n_q=n_q,
                          scale=SB_SCALE2),
        grid_spec=grid_spec,
        out_shape=jax.ShapeDtypeStruct((nb, ncol, d), F32),
        compiler_params=_cparams(("arbitrary", "arbitrary")),
        name="sb_paged",
    )(pt, *([ck] * pages_per_step), *([cv] * pages_per_step), wq, bias, knew, vnew,
      _suffix_matrix_t(page))


def _oproj_body(o_ref, x_ref, gt_ref, w_ref, y_ref):
    mix = jnp.dot(o_ref[0], w_ref[...], preferred_element_type=F32)
    y_ref[0] = x_ref[0] + gt_ref[0] * mix


def _oproj(o, x, gate, w_o, rows):
    nb, s, d = x.shape
    xspec = pl.BlockSpec((1, rows, d), lambda b, i: (b, i, 0))
    return pl.pallas_call(
        _oproj_body,
        grid=(nb, s // rows),
        in_specs=[xspec, xspec, _mod_spec(gate, rows), _const_spec(w_o.shape)],
        out_specs=xspec,
        out_shape=jax.ShapeDtypeStruct((nb, s, d), F32),
        compiler_params=_cparams(("arbitrary", "arbitrary")),
        name="attn_out_proj",
    )(o, x, gate, w_o)


def _mix_weights_prompt(w_s, b_s, width):
    causal = jnp.tril(jnp.ones((CHUNK, CHUNK), dtype=bool))
    wmix = jnp.where(causal[None], w_s, 0.0).astype(BF16)
    bmix = jnp.repeat(b_s.T, width // GROUPS, axis=1)
    return wmix, bmix


def _mix_weights_sample(w_s, b_s, width, n_seq, n_tok):
    causal = jnp.tril(jnp.ones((n_tok, n_tok), dtype=bool))
    small = jnp.where(causal[None], w_s[:, :n_tok, :n_tok], 0.0)
    eye = jnp.eye(n_seq, dtype=F32)
    blockdiag = jnp.einsum('ab,gts->gatbs', eye, small).reshape(GROUPS, n_seq * n_tok, n_seq * n_tok)
    padn = SAMPLE_ROWS - n_seq * n_tok
    wmix = jnp.pad(blockdiag, ((0, 0), (0, padn), (0, padn))).astype(BF16)
    brow = jnp.tile(b_s.T[:n_tok], (n_seq, 1))
    bmix = jnp.repeat(jnp.pad(brow, ((0, padn), (0, 0))), width // GROUPS, axis=1)
    return wmix, bmix


def kernel(x_prompt, x_sample, cache_k, cache_v, page_table, c_prompt, c_sample, mod_w, mod_b, norm_mix_g, norm_ffn_g, a_w_in, a_b_in, a_v_norm_g, a_w_s, a_b_s, a_w_out, kv_mod_w, kv_mod_b, kv_norm_g, kv_w, k_norm_g, b_w_q, b_q_norm_g, b_logit_bias, b_w_o, peer_w_q, peer_subkeys, peer_u, peer_v):
    nbp, seq, d = x_prompt.shape
    nbs, n_tok, _ = x_sample.shape
    n_samp = nbs * n_tok
    width = a_w_out.shape[1]

    c_all = jnp.concatenate([c_prompt, c_sample], axis=0)
    c_all = jnp.pad(c_all, ((0, BF16_ROWS - c_all.shape[0]), (0, 0)))
    mods = _adaln(c_all, mod_w, mod_b)
    kvmod = _adaln(c_all, kv_mod_w[None], kv_mod_b[None])[0]

    def split_mod(m, n):
        parts = jnp.split(m, n, axis=-1)
        prm = [p[:nbp, None, :] for p in parts]
        smp = [jnp.pad(jnp.repeat(p[nbp:nbp + nbs], n_tok, axis=0), ((0, SAMPLE_ROWS - n_samp), (0, 0)))[None]
               for p in parts]
        return prm, smp

    mod_p, mod_s = zip(*[split_mod(mods[l], 6) for l in range(mods.shape[0])])
    kvmod_p, kvmod_s = split_mod(kvmod, 2)

    w_in = a_w_in[0].astype(BF16)
    w_out = a_w_out[0].astype(BF16)
    wq_t = [jnp.transpose(peer_w_q[l]).astype(BF16) for l in range(2)]
    subk = [peer_subkeys[l].astype(BF16) for l in range(2)]
    u_tab = peer_u.astype(BF16)
    eb = 256
    n_l, n_exp, _ = peer_v.shape
    vt_tab = jnp.transpose(peer_v.reshape(n_l, n_exp // eb, eb, d), (0, 1, 3, 2)).astype(BF16)
    w_kv = kv_w.astype(BF16)
    w_q1 = b_w_q[0].astype(BF16)
    w_o1 = b_w_o[0].astype(BF16)
    k_gain = jnp.tile(k_norm_g, N_HEADS).reshape(1, d)
    q_gain = jnp.tile(b_q_norm_g[0], N_HEADS).reshape(1, d)
    wmix_p, bmix_p = _mix_weights_prompt(a_w_s[0], a_b_s[0], width)
    wmix_s, bmix_s = _mix_weights_sample(a_w_s[0], a_b_s[0], width, nbs, n_tok)

    xs = jnp.pad(x_sample.reshape(1, n_samp, d), ((0, 0), (0, SAMPLE_ROWS - n_samp), (0, 0)))

    def trunk(x, mod, kvm, wmix, bmix, rows, tb, n_sub, attend):
        sh_m, sc_m, g_m, sh_f, sc_f, g_f = mod[0]
        x1, v_rows = _mixer_a(x, sh_m, sc_m, g_m, norm_mix_g[0], w_in, a_b_in[0], a_v_norm_g[0],
                              wmix, bmix, w_out, rows)
        x2 = _peer(x1, sh_f, sc_f, g_f, norm_ffn_g[0], wq_t[0], subk[0], u_tab, vt_tab, 0, tb, n_sub, eb)
        sh_m, sc_m, g_m, sh_f, sc_f, g_f = mod[1]
        k, v, kb, vb, q = _kvq(x2, kvm[0], kvm[1], sh_m, sc_m, kv_norm_g, norm_mix_g[1], w_kv, w_q1,
                               k_gain, q_gain, rows)
        o = attend(q, k, v, kb, vb)
        x3 = _oproj(o, x2, g_m, w_o1, rows)
        y = _peer(x3, sh_f, sc_f, g_f, norm_ffn_g[1], wq_t[1], subk[1], u_tab, vt_tab, 1, tb, n_sub, eb)
        return y, k, v, v_rows

    def attend_prompt(q, k, v, kb, vb):
        return _sb_prompt(q, kb, vb, b_logit_bias[0], 256, 4)

    def attend_sample(q, k, v, kb, vb):
        qs = q[0, :n_samp].reshape(nbs, n_tok, d)
        ks = k[0, :n_samp].reshape(nbs, n_tok, d)
        vs = v[0, :n_samp].reshape(nbs, n_tok, d)
        full = _sb_paged(qs, ks, vs, cache_k, cache_v, page_table, b_logit_bias[0], 4)
        full = full[:, :N_HEADS * n_tok].reshape(nbs, N_HEADS, n_tok, N_HEADS, HEAD_DIM)
        idx = jnp.arange(N_HEADS)
        o = full[:, idx, :, idx, :]
        o = jnp.transpose(o, (1, 2, 0, 3)).reshape(1, n_samp, d)
        return jnp.pad(o, ((0, 0), (0, SAMPLE_ROWS - n_samp), (0, 0))).astype(BF16)

    y_p, k_p, v_p, _ = trunk(x_prompt, mod_p, kvmod_p, wmix_p, bmix_p, 256, 512, 2, attend_prompt)
    y_s, k_s, v_s, gv_s = trunk(xs, mod_s, kvmod_s, wmix_s, bmix_s, SAMPLE_ROWS, SAMPLE_ROWS, 1, attend_sample)

    def samp(a, shape):
        return a[0, :n_samp].reshape(shape)

    return (y_p,
            samp(y_s, (nbs, n_tok, d)),
            k_p.reshape(nbp, seq, N_HEADS, HEAD_DIM),
            v_p.reshape(nbp, seq, N_HEADS, HEAD_DIM),
            samp(k_s, (nbs, n_tok, N_HEADS, HEAD_DIM)),
            samp(v_s, (nbs, n_tok, N_HEADS, HEAD_DIM)),
            samp(gv_s, (1, nbs, n_tok, width)))
```

```python
import functools
import math

import jax
import jax.numpy as jnp
import numpy as np
from jax import lax
from jax.experimental import pallas as pl
from jax.experimental.pallas import tpu as pltpu

F32 = jnp.float32
BF16 = jnp.bfloat16
EPS = 1e-6

LANES = 128
SUBLANES = 8
BF16_ROWS = 16
VMEM_LIMIT = 56 * 1024 * 1024

CHUNK = 128
GROUPS = 16
N_HEADS = 16
HEAD_DIM = 128
PEER_HEADS = 8
PEER_NKEYS = 128
PEER_TOPK = 16
SAMPLE_ROWS = 128


def _cparams(sem):
    return pltpu.CompilerParams(dimension_semantics=sem, vmem_limit_bytes=VMEM_LIMIT)


def _const_spec(shape):
    nd = len(shape)
    return pl.BlockSpec(shape, lambda *_: (0,) * nd, pipeline_mode=pl.Buffered(1))


def _rms(x, g):
    ms = jnp.mean(x * x, axis=-1, keepdims=True)
    return x * lax.rsqrt(ms + EPS) * g


def _gelu(x):
    return 0.5 * x * (1.0 + lax.erf(x * np.float32(math.sqrt(0.5))))


def _head_rms(x, g):
    cols = []
    for h in range(x.shape[1] // HEAD_DIM):
        seg = x[:, h * HEAD_DIM:(h + 1) * HEAD_DIM]
        ms = jnp.mean(seg * seg, axis=-1, keepdims=True)
        cols.append(seg * lax.rsqrt(ms + EPS))
    return jnp.concatenate(cols, axis=1) * g


def _adaln_body(c_ref, w_ref, b_ref, o_ref):
    c = c_ref[...]
    s = c / (1.0 + jnp.exp(-c))
    o_ref[0] = jnp.dot(s.astype(BF16), w_ref[0].astype(BF16), preferred_element_type=F32) + b_ref[0]


def _adaln(c, w, b):
    n_l, d, n = w.shape
    m = c.shape[0]
    tn = 1024
    return pl.pallas_call(
        _adaln_body,
        grid=(n_l, n // tn),
        in_specs=[pl.BlockSpec((m, d), lambda l, j: (0, 0)),
                  pl.BlockSpec((1, d, tn), lambda l, j: (l, 0, j)),
                  pl.BlockSpec((1, 1, tn), lambda l, j: (l, 0, j))],
        out_specs=pl.BlockSpec((1, m, tn), lambda l, j: (l, 0, j)),
        out_shape=jax.ShapeDtypeStruct((n_l, m, n), F32),
        compiler_params=_cparams(("arbitrary", "arbitrary")),
        name="adaln",
    )(c, w, b.reshape(n_l, 1, n))


def _mod_spec(mod, rows):
    d = mod.shape[-1]
    if mod.shape[1] == 1:
        return pl.BlockSpec((1, 1, d), lambda b, i: (b, 0, 0))
    return pl.BlockSpec((1, rows, d), lambda b, i: (b, i, 0))


def _mixer_a_body(x_ref, sh_ref, sc_ref, gt_ref, ng_ref, win_ref, bin_ref, vg_ref, wmix_ref,
                  bmix_ref, wout_ref, x1_ref, v_ref, *, chunk):
    x = x_ref[0]
    rows = x.shape[0]
    h = _rms(x, ng_ref[...]) * (1.0 + sc_ref[0]) + sh_ref[0]
    z = jnp.dot(h.astype(BF16), win_ref[...], preferred_element_type=F32) + bin_ref[...]
    z = _gelu(z)
    width = z.shape[1] // 2
    u = z[:, :width]
    v = _rms(z[:, width:], vg_ref[...])
    v_ref[0] = v
    vb = v.astype(BF16)
    gd = width // GROUPS
    blocks = []
    for c in range(rows // chunk):
        cols = [jnp.dot(wmix_ref[g], vb[c * chunk:(c + 1) * chunk, g * gd:(g + 1) * gd],
                        preferred_element_type=F32) for g in range(GROUPS)]
        blocks.append(jnp.concatenate(cols, axis=1) + bmix_ref[...])
    mixed = blocks[0] if len(blocks) == 1 else jnp.concatenate(blocks, axis=0)
    um = (u * mixed).astype(BF16)
    mix = jnp.dot(um, wout_ref[...], preferred_element_type=F32)
    x1_ref[0] = x + gt_ref[0] * mix


def _mixer_a(x, shift, scale, gate, norm_g, w_in, b_in, v_g, wmix, bmix, w_out, rows):
    nb, s, d = x.shape
    width = w_out.shape[0]
    chunk = wmix.shape[1]
    xspec = pl.BlockSpec((1, rows, d), lambda b, i: (b, i, 0))
    return pl.pallas_call(
        functools.partial(_mixer_a_body, chunk=chunk),
        grid=(nb, s // rows),
        in_specs=[xspec, _mod_spec(shift, rows), _mod_spec(scale, rows), _mod_spec(gate, rows),
                  _const_spec((1, d)), _const_spec(w_in.shape), _const_spec((1, 2 * width)),
                  _const_spec((1, width)), _const_spec(wmix.shape), _const_spec(bmix.shape),
                  _const_spec(w_out.shape)],
        out_specs=[xspec, pl.BlockSpec((1, rows, width), lambda b, i: (b, i, 0))],
        out_shape=[jax.ShapeDtypeStruct((nb, s, d), F32), jax.ShapeDtypeStruct((nb, s, width), F32)],
        compiler_params=_cparams(("arbitrary", "arbitrary")),
        name="gmlp_mixer",
    )(x, shift, scale, gate, norm_g.reshape(1, d), w_in, b_in.reshape(1, -1), v_g.reshape(1, -1),
      wmix, bmix, w_out)


def _sort_network(n):
    pairs = []
    p = 1
    while p < n:
        k = p
        while k >= 1:
            for j in range(k % p, n - k, 2 * k):
                for i in range(min(k, n - j - k)):
                    if (i + j) // (2 * p) == (i + j + k) // (2 * p):
                        pairs.append((i + j, i + j + k))
            k //= 2
        p *= 2
    return pairs


_SORT16 = _sort_network(PEER_TOPK)
_BITONIC16 = [(i, i | d) for d in (8, 4, 2, 1) for i in range(PEER_TOPK) if not i & d]


def _apply_network(xs, pairs):
    xs = list(xs)
    for i, j in pairs:
        a, b = xs[i], xs[j]
        xs[i] = jnp.maximum(a, b)
        xs[j] = jnp.minimum(a, b)
    return xs


def _merge_top16(xs, ys):
    zs = [jnp.maximum(xs[r], ys[PEER_TOPK - 1 - r]) for r in range(PEER_TOPK)]
    return _apply_network(zs, _BITONIC16)


def _top16_of_keys(s):
    slabs = [s[SUBLANES * g:SUBLANES * (g + 1), :] for g in range(PEER_NKEYS // SUBLANES)]
    xs = _apply_network(slabs, _SORT16)
    for shift in (4, 2, 1):
        ys = [pltpu.roll(x, shift, axis=0) for x in xs]
        xs = _merge_top16(xs, ys)
    return xs


_CAND = [(r, c) for r in range(PEER_TOPK) for c in range(PEER_TOPK) if (r + 1) * (c + 1) <= PEER_TOPK]


def _route_body(x_ref, sh_ref, sc_ref, ng_ref, wq_ref, sk_ref,
                h2t_ref, r2_ref, e2_ref, c1_ref, e1_ref, s_ref):
    x = x_ref[0]
    h = _rms(x, ng_ref[...]) * (1.0 + sc_ref[0]) + sh_ref[0]
    ht = h.T.astype(BF16)
    h2t_ref[...] = ht
    qt = jnp.dot(wq_ref[...], ht, preferred_element_type=F32).astype(BF16)
    for hp in range(2 * PEER_HEADS):
        s_ref[hp] = jnp.dot(sk_ref[hp % 2], qt[hp * PEER_NKEYS:(hp + 1) * PEER_NKEYS, :],
                            preferred_element_type=F32)
    tb = x.shape[0]
    neg = jnp.full((SUBLANES, LANES), -jnp.inf, F32)
    sub = lax.broadcasted_iota(jnp.int32, (SUBLANES, LANES), 0)

    def chunk(lc, carry):
        lanes = pl.ds(pl.multiple_of(lc * LANES, LANES), LANES)
        top = [[neg] * PEER_TOPK, [neg] * PEER_TOPK]
        for hp in range(2 * PEER_HEADS):
            hh, p = hp // 2, hp % 2
            xs = _top16_of_keys(s_ref[hp, :, lanes])
            top[p] = [jnp.where(sub == hh, xs[r], top[p][r]) for r in range(PEER_TOPK)]
        a, b = top
        cand = {rc: a[rc[0]] + b[rc[1]] for rc in _CAND}
        g1 = [cand[(0, c)] for c in range(PEER_TOPK)]
        rest = [cand[rc] for rc in _CAND if rc[0] > 0]
        rest = rest + [neg] * (3 * PEER_TOPK - len(rest))
        gs = [_apply_network(rest[PEER_TOPK * k:PEER_TOPK * (k + 1)], _SORT16) for k in range(3)]
        m1 = _merge_top16(g1, gs[0])
        m2 = _merge_top16(gs[1], gs[2])
        zs = [jnp.maximum(m1[r], m2[PEER_TOPK - 1 - r]) for r in range(PEER_TOPK)]
        tau = functools.reduce(jnp.minimum, zs)
        top_sum = a[0] + b[0]
        zsum = jnp.zeros((SUBLANES, LANES), F32)
        for rc in _CAND:
            zsum = zsum + jnp.where(cand[rc] >= tau, jnp.exp(cand[rc] - top_sum), 0.0)
        zinv = 1.0 / zsum
        for hh in range(PEER_HEADS):
            def row(v):
                return jnp.broadcast_to(v[hh:hh + 1, :], (SUBLANES, LANES))
            bh = [row(b[c]) for c in range(PEER_TOPK)]
            tau_h, zinv_h, a0_h = row(tau), row(zinv), row(a[0])
            for gp in range(PEER_NKEYS // BF16_ROWS):
                r2s, e2s, c1s, e1s = [], [], [], []
                for g in (2 * gp, 2 * gp + 1):
                    rows = slice(SUBLANES * g, SUBLANES * (g + 1))
                    s1 = s_ref[2 * hh, rows, lanes]
                    s2 = s_ref[2 * hh + 1, rows, lanes]
                    r2 = jnp.zeros((SUBLANES, LANES), F32)
                    c1 = jnp.zeros((SUBLANES, LANES), F32)
                    for c in range(PEER_TOPK):
                        r2 = r2 + jnp.where(bh[c] > s2, 1.0, 0.0)
                        c1 = c1 + jnp.where(s1 + bh[c] >= tau_h, 1.0, 0.0)
                    r2s.append(r2)
                    c1s.append(c1)
                    e2s.append(jnp.exp(s2 - bh[0]))
                    e1s.append(jnp.exp(s1 - a0_h) * zinv_h)
                rows16 = slice(BF16_ROWS * gp, BF16_ROWS * (gp + 1))
                r2_ref[hh, rows16, lanes] = jnp.concatenate(r2s, axis=0).astype(BF16)
                e2_ref[hh, rows16, lanes] = jnp.concatenate(e2s, axis=0).astype(BF16)
                c1_ref[hh, rows16, lanes] = jnp.concatenate(c1s, axis=0)
                e1_ref[hh, rows16, lanes] = jnp.concatenate(e1s, axis=0)
        return carry

    lax.fori_loop(0, tb // LANES, chunk, 0)


def _peer_route(x, shift, scale, norm_g, wq_t, subkeys, tb):
    nb, s, d = x.shape
    nblk = s // tb
    nt = nb * nblk
    tspec3 = pl.BlockSpec((None, PEER_HEADS, PEER_NKEYS, tb), lambda b, i: (b * nblk + i, 0, 0, 0))
    return pl.pallas_call(
        _route_body,
        grid=(nb, nblk),
        in_specs=[pl.BlockSpec((1, tb, d), lambda b, i: (b, i, 0)), _mod_spec(shift, tb), _mod_spec(scale, tb),
                  _const_spec((1, d)), _const_spec(wq_t.shape), _const_spec(subkeys.shape)],
        out_specs=[pl.BlockSpec((None, d, tb), lambda b, i: (b * nblk + i, 0, 0)), tspec3, tspec3, tspec3, tspec3],
        out_shape=[jax.ShapeDtypeStruct((nt, d, tb), BF16),
                   jax.ShapeDtypeStruct((nt, PEER_HEADS, PEER_NKEYS, tb), BF16),
                   jax.ShapeDtypeStruct((nt, PEER_HEADS, PEER_NKEYS, tb), BF16),
                   jax.ShapeDtypeStruct((nt, PEER_HEADS, PEER_NKEYS, tb), F32),
                   jax.ShapeDtypeStruct((nt, PEER_HEADS, PEER_NKEYS, tb), F32)],
        scratch_shapes=[pltpu.VMEM((2 * PEER_HEADS, PEER_NKEYS, tb), F32)],
        compiler_params=_cparams(("arbitrary", "arbitrary")),
        name="peer_route",
    )(x, shift, scale, norm_g.reshape(1, d), wq_t, subkeys)


def _peer_gate_stage(act_ref, p_ref, r2_ref, e2_ref, c1_ref, e1_ref, key0, n_i, valid=None):
    n_sub, _, tb = act_ref.shape
    for j in range(n_sub):
        for il in range(n_i):
            c1i = c1_ref[j, key0 + il]
            e1i = e1_ref[j, key0 + il]
            for lc in range(tb // LANES):
                lanes = slice(lc * LANES, (lc + 1) * LANES)
                gate = [jnp.zeros((BF16_ROWS, LANES), BF16)] * (PEER_NKEYS // BF16_ROWS)
                for hh in range(PEER_HEADS):
                    c1b = jnp.broadcast_to(c1i[hh:hh + 1, lanes], (BF16_ROWS, LANES)).astype(BF16)
                    e1b = jnp.broadcast_to(e1i[hh:hh + 1, lanes], (BF16_ROWS, LANES)).astype(BF16)
                    for rb in range(PEER_NKEYS // BF16_ROWS):
                        rows = slice(rb * BF16_ROWS, (rb + 1) * BF16_ROWS)
                        r2 = r2_ref[j, hh, rows, lanes]
                        e2 = e2_ref[j, hh, rows, lanes]
                        gate[rb] = gate[rb] + jnp.where(r2 < c1b, e2, jnp.zeros_like(e2)) * e1b
                for rb in range(PEER_NKEYS // BF16_ROWS):
                    rows = slice(il * PEER_NKEYS + rb * BF16_ROWS, il * PEER_NKEYS + (rb + 1) * BF16_ROWS)
                    a = act_ref[j, rows, lanes]
                    p = (_gelu(a) * gate[rb].astype(F32)).astype(BF16)
                    p_ref[j, rows, lanes] = p if valid is None else jnp.where(valid, p, jnp.zeros_like(p))


def _peer_body(h2t_ref, u_ref, vta_ref, vtb_ref, r2_ref, e2_ref, c1_ref, e1_ref, acc_ref,
               acta_ref, actb_ref, pa_ref, pb_ref, *, n_i):
    s = pl.program_id(1)
    last = pl.num_programs(1) - 1
    n_blocks = 2 * last
    n_sub, eb, _ = acta_ref.shape
    routing = (r2_ref, e2_ref, c1_ref, e1_ref)

    @pl.when(s == 0)
    def _():
        acc_ref[...] = jnp.zeros_like(acc_ref)
        actb_ref[...] = jnp.zeros_like(actb_ref)

    for j in range(n_sub):
        acta_ref[j] = jnp.dot(u_ref[:eb, :], h2t_ref[j], preferred_element_type=F32)
    _peer_gate_stage(actb_ref, pb_ref, *routing, jnp.maximum(2 * s - 1, 0) * n_i, n_i)
    for j in range(n_sub):
        acc_ref[j] += jnp.dot(vta_ref[...], pb_ref[j], preferred_element_type=F32)
    for j in range(n_sub):
        actb_ref[j] = jnp.dot(u_ref[eb:, :], h2t_ref[j], preferred_element_type=F32)
    _peer_gate_stage(acta_ref, pa_ref, *routing, jnp.minimum(2 * s, n_blocks - 1) * n_i, n_i, valid=s < last)
    for j in range(n_sub):
        acc_ref[j] += jnp.dot(vtb_ref[...], pa_ref[j], preferred_element_type=F32)


def _peer_experts(h2t, u_tab, vt_tab, layer, r2, e2, c1, e1, n_sub, eb):
    nt, d, tb = h2t.shape
    n_exp = u_tab.shape[1]
    n_i = eb // PEER_NKEYS
    n_pairs = n_exp // (2 * eb)
    assert vt_tab.shape[1:] == (2 * n_pairs, d, eb) and nt % n_sub == 0
    one = pl.Buffered(1)
    t3 = pl.BlockSpec((n_sub, PEER_HEADS, PEER_NKEYS, tb), lambda i, e: (i, 0, 0, 0), pipeline_mode=one)
    k3 = pl.BlockSpec((n_sub, PEER_NKEYS, PEER_HEADS, tb), lambda i, e: (i, 0, 0, 0), pipeline_mode=one)
    tile3 = pl.BlockSpec((n_sub, d, tb), lambda i, e: (i, 0, 0))
    return pl.pallas_call(
        functools.partial(_peer_body, n_i=n_i),
        grid=(nt // n_sub, n_pairs + 1),
        in_specs=[tile3,
                  pl.BlockSpec((None, 2 * eb, d), lambda i, e: (layer, jnp.minimum(e, n_pairs - 1), 0)),
                  pl.BlockSpec((None, None, d, eb), lambda i, e: (layer, jnp.maximum(2 * e - 1, 0), 0, 0)),
                  pl.BlockSpec((None, None, d, eb), lambda i, e: (layer, jnp.minimum(2 * e, 2 * n_pairs - 1), 0, 0)),
                  t3, t3, k3, k3],
        out_specs=tile3,
        out_shape=jax.ShapeDtypeStruct((nt, d, tb), F32),
        scratch_shapes=[pltpu.VMEM((n_sub, eb, tb), F32), pltpu.VMEM((n_sub, eb, tb), F32),
                        pltpu.VMEM((n_sub, eb, tb), BF16), pltpu.VMEM((n_sub, eb, tb), BF16)],
        compiler_params=_cparams(("arbitrary", "arbitrary")),
        name="peer_experts",
    )(h2t, u_tab, vt_tab, vt_tab, r2, e2, c1, e1)


def _peer_finish_body(acc_ref, x_ref, g_ref, y_ref):
    y_ref[0] = x_ref[0] + g_ref[0] * acc_ref[0].T


def _peer_finish(acc_t, x, gate):
    nb, s, d = x.shape
    tb = acc_t.shape[2]
    nblk = s // tb
    xspec = pl.BlockSpec((1, tb, d), lambda b, i: (b, i, 0))
    return pl.pallas_call(
        _peer_finish_body,
        grid=(nb, nblk),
        in_specs=[pl.BlockSpec((1, d, tb), lambda b, i: (b * nblk + i, 0, 0)), xspec, _mod_spec(gate, tb)],
        out_specs=xspec,
        out_shape=jax.ShapeDtypeStruct((nb, s, d), F32),
        compiler_params=_cparams(("arbitrary", "arbitrary")),
        name="peer_finish",
    )(acc_t, x, gate)


def _peer(x, shift, scale, gate, norm_g, wq_t, subkeys, u_tab, vt_tab, layer, tb, n_sub, eb):
    h2t, r2, e2, c1, e1 = _peer_route(x, shift, scale, norm_g, wq_t, subkeys, tb)
    c1 = jnp.transpose(c1, (0, 2, 1, 3))
    e1 = jnp.transpose(e1, (0, 2, 1, 3))
    acc_t = _peer_experts(h2t, u_tab, vt_tab, layer, r2, e2, c1, e1, n_sub, eb)
    return _peer_finish(acc_t, x, gate)


def _kvq_body(x_ref, ksh_ref, ksc_ref, msh_ref, msc_ref, kng_ref, mng_ref, wkv_ref, wq_ref,
              kg_ref, qg_ref, k_ref, v_ref, kb_ref, vb_ref, q_ref):
    x = x_ref[0]
    d = x.shape[1]
    ms = jnp.mean(x * x, axis=-1, keepdims=True)
    xn = x * lax.rsqrt(ms + EPS)
    hk = (xn * kng_ref[...]) * (1.0 + ksc_ref[0]) + ksh_ref[0]
    kv = jnp.dot(hk.astype(BF16), wkv_ref[...], preferred_element_type=F32)
    k = _head_rms(kv[:, :d], kg_ref[...])
    v = kv[:, d:]
    k_ref[0] = k
    v_ref[0] = v
    kb_ref[0] = k.astype(BF16)
    vb_ref[0] = v.astype(BF16)
    hq = (xn * mng_ref[...]) * (1.0 + msc_ref[0]) + msh_ref[0]
    q = jnp.dot(hq.astype(BF16), wq_ref[...], preferred_element_type=F32)
    q_ref[0] = _head_rms(q, qg_ref[...]).astype(BF16)


def _kvq(x, k_shift, k_scale, m_shift, m_scale, kv_norm_g, mix_norm_g, w_kv, w_q, k_gain, q_gain, rows):
    nb, s, d = x.shape
    xspec = pl.BlockSpec((1, rows, d), lambda b, i: (b, i, 0))
    return pl.pallas_call(
        _kvq_body,
        grid=(nb, s // rows),
        in_specs=[xspec, _mod_spec(k_shift, rows), _mod_spec(k_scale, rows), _mod_spec(m_shift, rows),
                  _mod_spec(m_scale, rows), _const_spec((1, d)), _const_spec((1, d)),
                  _const_spec(w_kv.shape), _const_spec(w_q.shape), _const_spec((1, d)), _const_spec((1, d))],
        out_specs=[xspec] * 5,
        out_shape=[jax.ShapeDtypeStruct((nb, s, d), F32), jax.ShapeDtypeStruct((nb, s, d), F32),
                   jax.ShapeDtypeStruct((nb, s, d), BF16), jax.ShapeDtypeStruct((nb, s, d), BF16),
                   jax.ShapeDtypeStruct((nb, s, d), BF16)],
        compiler_params=_cparams(("arbitrary", "arbitrary")),
        name="kv_q_proj",
    )(x, k_shift, k_scale, m_shift, m_scale, kv_norm_g.reshape(1, d), mix_norm_g.reshape(1, d),
      w_kv, w_q, k_gain, q_gain)


NEG_LOG2E = np.float32(-math.log2(math.e))
SB_NEG_SCALE2 = np.float32(-(HEAD_DIM ** -0.5) * math.log2(math.e))


def _log2_beta(y):
    neg_abs = lax.bitcast_convert_type(lax.bitcast_convert_type(y, jnp.int32) | jnp.int32(-2 ** 31), F32)
    lm = jnp.minimum(y, 0.0) - jnp.log2(1.0 + jnp.exp2(neg_abs))
    return lm - y, lm


def _split_bf16(x):
    hi = x.astype(BF16)
    lo = (x - hi.astype(F32)).astype(BF16)
    return hi, lo


def _sb_prompt_body(q_ref, k_ref, v_ref, bias_ref, tri_ref, o_ref,
                    d_ref, surv_ref, lms_ref, run_ref, acc_ref, *, blk, heads, scale):
    qi = pl.program_id(2)
    hsl = [slice(h * HEAD_DIM, (h + 1) * HEAD_DIM) for h in range(heads)]
    hr = range(heads)

    def logits(kb):
        ks = pl.multiple_of(kb * blk, blk)
        return [lax.dot_general(q_ref[0, :, hsl[h]], k_ref[0, pl.ds(ks, blk), hsl[h]],
                                (((1,), (1,)), ((), ())), preferred_element_type=F32) * scale
                + bias_ref[0, h:h + 1, :] for h in hr]

    def produce(z, mask):
        for h in hr:
            d, lm = _log2_beta(z[h])
            if mask is not None:
                d, lm = jnp.where(mask, d, -jnp.inf), jnp.where(mask, lm, 0.0)
            hi, lo = _split_bf16(lm)
            surv_ref[h] = jnp.dot(jnp.concatenate([hi, lo], axis=1), tri_ref[...],
                                  preferred_element_type=F32)
            d_ref[h] = d
            lms_ref[h] = jnp.sum(lm, axis=1, keepdims=True)

    def consume(kb):
        ks = pl.multiple_of(kb * blk, blk)
        for h in hr:
            a = jnp.exp2(d_ref[h] + surv_ref[h] + run_ref[h])
            acc_ref[h] += jnp.dot(a.astype(BF16), v_ref[0, pl.ds(ks, blk), hsl[h]],
                                  preferred_element_type=F32)
            run_ref[h] += lms_ref[h]

    acc_ref[...] = jnp.zeros_like(acc_ref)
    run_ref[...] = jnp.zeros_like(run_ref)
    row = lax.broadcasted_iota(jnp.int32, (blk, blk), 0)
    col = lax.broadcasted_iota(jnp.int32, (blk, blk), 1)
    produce(logits(qi), col < row)

    def trip(j, carry):
        z = logits(qi - j - 1)
        consume(qi - j)
        produce(z, None)
        return carry

    lax.fori_loop(0, qi, trip, 0)
    consume(0)
    for h in hr:
        o_ref[0, :, hsl[h]] = acc_ref[h].astype(o_ref.dtype)


def _suffix_matrix(n):
    m = (np.arange(n)[:, None] > np.arange(n)[None, :]).astype(np.float32)
    return jnp.asarray(np.concatenate([m, m], axis=0), dtype=BF16)


def _sb_prompt(q, k, v, logit_bias, blk, heads):
    nb, s, d = q.shape
    bias = jnp.broadcast_to((logit_bias.astype(F32) * NEG_LOG2E).reshape(N_HEADS // heads, heads, 1),
                            (N_HEADS // heads, heads, blk))
    qspec = pl.BlockSpec((1, blk, heads * HEAD_DIM), lambda b, h, i: (b, i, h))
    kvspec = pl.BlockSpec((1, s, heads * HEAD_DIM), lambda b, h, i: (b, 0, h))
    return pl.pallas_call(
        functools.partial(_sb_prompt_body, blk=blk, heads=heads, scale=SB_NEG_SCALE2),
        grid=(nb, N_HEADS // heads, s // blk),
        in_specs=[qspec, kvspec, kvspec, pl.BlockSpec((1, heads, blk), lambda b, h, i: (h, 0, 0)),
                  pl.BlockSpec((2 * blk, blk), lambda b, h, i: (0, 0))],
        out_specs=qspec,
        out_shape=jax.ShapeDtypeStruct((nb, s, d), BF16),
        scratch_shapes=[pltpu.VMEM((heads, blk, blk), F32), pltpu.VMEM((heads, blk, blk), F32),
                        pltpu.VMEM((heads, blk, 1), F32), pltpu.VMEM((heads, blk, 1), F32),
                        pltpu.VMEM((heads, blk, HEAD_DIM), F32)],
        compiler_params=_cparams(("arbitrary", "arbitrary", "arbitrary")),
        name="sb_prompt",
    )(q, k, v, bias, _suffix_matrix(blk))


def _sb_paged_body(pt_ref, *refs, pages_per_step, n_q, scale):
    kp = refs[:pages_per_step]
    vp = refs[pages_per_step:2 * pages_per_step]
    wq_ref, bias_ref, knew_ref, vnew_ref, tri_ref, o_ref, acc_ref, run_ref = refs[2 * pages_per_step:]
    step = pl.program_id(1)
    wq = wq_ref[0]
    bias = bias_ref[...]

    def heads_to_lanes(ref):
        n_pos = ref.shape[1] // N_HEADS
        return jnp.concatenate(
            [ref[0, pl.ds(h, n_pos, stride=N_HEADS), :].astype(BF16) for h in range(N_HEADS)], axis=1)

    def block(kblk, vblk, mask):
        d, lm = _log2_beta(jnp.dot(kblk, wq, preferred_element_type=F32) * scale + bias)
        if mask is not None:
            lm = jnp.where(mask, lm, 0.0)
        hi, lo = _split_bf16(lm)
        surv = jnp.dot(tri_ref[...], jnp.concatenate([hi, lo], axis=0),
                       preferred_element_type=F32) + run_ref[...]
        a = jnp.exp2(d + surv)
        if mask is not None:
            a = jnp.where(mask, a, 0.0)
        acc_ref[...] += jnp.dot(a.T.astype(BF16), vblk, preferred_element_type=F32)
        run_ref[...] += jnp.sum(lm, axis=0, keepdims=True)

    @pl.when(step == 0)
    def _():
        acc_ref[...] = jnp.zeros_like(acc_ref)
        run_ref[...] = jnp.zeros_like(run_ref)
        shape = (knew_ref.shape[1], wq.shape[1])
        kpos = lax.broadcasted_iota(jnp.int32, shape, 0)
        qpos = lax.broadcasted_iota(jnp.int32, shape, 1) % n_q
        block(knew_ref[0].astype(BF16), vnew_ref[0].astype(BF16), kpos < qpos)

    ds, survs, tots = [], [], []
    for i in range(pages_per_step):
        d, lm = _log2_beta(jnp.dot(heads_to_lanes(kp[i]), wq, preferred_element_type=F32) * scale + bias)
        hi, lo = _split_bf16(lm)
        ds.append(d)
        survs.append(jnp.dot(tri_ref[...], jnp.concatenate([hi, lo], axis=0), preferred_element_type=F32))
        tots.append(jnp.sum(lm, axis=0, keepdims=True))
    run = run_ref[...]
    weights = []
    for i in range(pages_per_step):
        weights.append(jnp.exp2(ds[i] + survs[i] + run).T.astype(BF16))
        run = run + tots[i]
    run_ref[...] = run
    acc_ref[...] += jnp.dot(jnp.concatenate(weights, axis=1),
                            jnp.concatenate([heads_to_lanes(vp[i]) for i in range(pages_per_step)], axis=0),
                            preferred_element_type=F32)

    @pl.when(step == pl.num_programs(1) - 1)
    def _():
        o_ref[0] = acc_ref[...]


def _suffix_matrix_t(n):
    m = (np.arange(n)[None, :] > np.arange(n)[:, None]).astype(np.float32)
    return jnp.asarray(np.concatenate([m, m], axis=1), dtype=BF16)


def _sb_paged(q, k_new, v_new, cache_k, cache_v, page_table, logit_bias, pages_per_step):
    nb, n_q, d = q.shape
    n_pool, page = cache_k.shape[:2]
    n_pages = page_table.shape[1]
    ck = cache_k.reshape(n_pool, page * N_HEADS, HEAD_DIM)
    cv = cache_v.reshape(n_pool, page * N_HEADS, HEAD_DIM)
    ncol = LANES
    assert N_HEADS * n_q <= ncol and n_q <= page
    qh = q.reshape(nb, n_q, N_HEADS, HEAD_DIM)
    eye = jnp.eye(N_HEADS, dtype=BF16)
    wq = jnp.einsum('bqhd,hg->bhdgq', qh, eye).reshape(nb, d, N_HEADS * n_q)
    wq = jnp.pad(wq, ((0, 0), (0, 0), (0, ncol - N_HEADS * n_q)))
    bias = jnp.pad(jnp.repeat(logit_bias.astype(F32) * NEG_LOG2E, n_q), (0, ncol - N_HEADS * n_q)).reshape(1, ncol)
    knew = jnp.pad(k_new, ((0, 0), (0, page - n_q), (0, 0)))
    vnew = jnp.pad(v_new, ((0, 0), (0, page - n_q), (0, 0)))
    pt = page_table.reshape(-1).astype(jnp.int32)

    def page_map(i):
        return lambda b, s, pt_ref: (pt_ref[b * n_pages + n_pages - 1 - (s * pages_per_step + i)], 0, 0)

    page_specs = [pl.BlockSpec((1, page * N_HEADS, HEAD_DIM), page_map(i)) for i in range(pages_per_step)]
    grid_spec = pltpu.PrefetchScalarGridSpec(
        num_scalar_prefetch=1,
        grid=(nb, n_pages // pages_per_step),
        in_specs=page_specs + page_specs + [
            pl.BlockSpec((1, d, ncol), lambda b, s, pt_ref: (b, 0, 0)),
            pl.BlockSpec((1, ncol), lambda b, s, pt_ref: (0, 0)),
            pl.BlockSpec((1, page, d), lambda b, s, pt_ref: (b, 0, 0)),
            pl.BlockSpec((1, page, d), lambda b, s, pt_ref: (b, 0, 0)),
            pl.BlockSpec((page, 2 * page), lambda b, s, pt_ref: (0, 0))],
        out_specs=pl.BlockSpec((1, ncol, d), lambda b, s, pt_ref: (b, 0, 0)),
        scratch_shapes=[pltpu.VMEM((ncol, d), F32), pltpu.VMEM((1, ncol), F32)])
    return pl.pallas_call(
        functools.partial(_sb_paged_body, pages_per_step=pages_per_step, n_q=n_q,
                          scale=SB_NEG_SCALE2),
        grid_spec=grid_spec,
        out_shape=jax.ShapeDtypeStruct((nb, ncol, d), F32),
        compiler_params=_cparams(("arbitrary", "arbitrary")),
        name="sb_paged",
    )(pt, *([ck] * pages_per_step), *([cv] * pages_per_step), wq, bias, knew, vnew,
      _suffix_matrix_t(page))


def _oproj_body(o_ref, x_ref, gt_ref, w_ref, y_ref):
    mix = jnp.dot(o_ref[0], w_ref[...], preferred_element_type=F32)
    y_ref[0] = x_ref[0] + gt_ref[0] * mix


def _oproj(o, x, gate, w_o, rows):
    nb, s, d = x.shape
    xspec = pl.BlockSpec((1, rows, d), lambda b, i: (b, i, 0))
    return pl.pallas_call(
        _oproj_body,
        grid=(nb, s // rows),
        in_specs=[xspec, xspec, _mod_spec(gate, rows), _const_spec(w_o.shape)],
        out_specs=xspec,
        out_shape=jax.ShapeDtypeStruct((nb, s, d), F32),
        compiler_params=_cparams(("arbitrary", "arbitrary")),
        name="attn_out_proj",
    )(o, x, gate, w_o)


def _mix_weights_prompt(w_s, b_s, width):
    causal = jnp.tril(jnp.ones((CHUNK, CHUNK), dtype=bool))
    wmix = jnp.where(causal[None], w_s, 0.0).astype(BF16)
    bmix = jnp.repeat(b_s.T, width // GROUPS, axis=1)
    return wmix, bmix


def _mix_weights_sample(w_s, b_s, width, n_seq, n_tok):
    causal = jnp.tril(jnp.ones((n_tok, n_tok), dtype=bool))
    small = jnp.where(causal[None], w_s[:, :n_tok, :n_tok], 0.0)
    eye = jnp.eye(n_seq, dtype=F32)
    blockdiag = jnp.einsum('ab,gts->gatbs', eye, small).reshape(GROUPS, n_seq * n_tok, n_seq * n_tok)
    padn = SAMPLE_ROWS - n_seq * n_tok
    wmix = jnp.pad(blockdiag, ((0, 0), (0, padn), (0, padn))).astype(BF16)
    brow = jnp.tile(b_s.T[:n_tok], (n_seq, 1))
    bmix = jnp.repeat(jnp.pad(brow, ((0, padn), (0, 0))), width // GROUPS, axis=1)
    return wmix, bmix


def kernel(x_prompt, x_sample, cache_k, cache_v, page_table, c_prompt, c_sample, mod_w, mod_b, norm_mix_g, norm_ffn_g, a_w_in, a_b_in, a_v_norm_g, a_w_s, a_b_s, a_w_out, kv_mod_w, kv_mod_b, kv_norm_g, kv_w, k_norm_g, b_w_q, b_q_norm_g, b_logit_bias, b_w_o, peer_w_q, peer_subkeys, peer_u, peer_v):
    nbp, seq, d = x_prompt.shape
    nbs, n_tok, _ = x_sample.shape
    n_samp = nbs * n_tok
    width = a_w_out.shape[1]

    c_all = jnp.concatenate([c_prompt, c_sample], axis=0)
    c_all = jnp.pad(c_all, ((0, BF16_ROWS - c_all.shape[0]), (0, 0)))
    mods = _adaln(c_all, mod_w, mod_b)
    kvmod = _adaln(c_all, kv_mod_w[None], kv_mod_b[None])[0]

    def split_mod(m, n):
        parts = jnp.split(m, n, axis=-1)
        prm = [p[:nbp, None, :] for p in parts]
        smp = [jnp.pad(jnp.repeat(p[nbp:nbp + nbs], n_tok, axis=0), ((0, SAMPLE_ROWS - n_samp), (0, 0)))[None]
               for p in parts]
        return prm, smp

    mod_p, mod_s = zip(*[split_mod(mods[l], 6) for l in range(mods.shape[0])])
    kvmod_p, kvmod_s = split_mod(kvmod, 2)

    w_in = a_w_in[0].astype(BF16)
    w_out = a_w_out[0].astype(BF16)
    wq_t = [jnp.transpose(peer_w_q[l]).astype(BF16) for l in range(2)]
    subk = [peer_subkeys[l].astype(BF16) for l in range(2)]
    u_tab = peer_u.astype(BF16)
    eb = 256
    n_l, n_exp, _ = peer_v.shape
    vt_tab = jnp.transpose(peer_v.reshape(n_l, n_exp // eb, eb, d), (0, 1, 3, 2)).astype(BF16)
    w_kv = kv_w.astype(BF16)
    w_q1 = b_w_q[0].astype(BF16)
    w_o1 = b_w_o[0].astype(BF16)
    k_gain = jnp.tile(k_norm_g, N_HEADS).reshape(1, d)
    q_gain = jnp.tile(b_q_norm_g[0], N_HEADS).reshape(1, d)
    wmix_p, bmix_p = _mix_weights_prompt(a_w_s[0], a_b_s[0], width)
    wmix_s, bmix_s = _mix_weights_sample(a_w_s[0], a_b_s[0], width, nbs, n_tok)

    xs = jnp.pad(x_sample.reshape(1, n_samp, d), ((0, 0), (0, SAMPLE_ROWS - n_samp), (0, 0)))

    def trunk(x, mod, kvm, wmix, bmix, rows, tb, n_sub, attend):
        sh_m, sc_m, g_m, sh_f, sc_f, g_f = mod[0]
        x1, v_rows = _mixer_a(x, sh_m, sc_m, g_m, norm_mix_g[0], w_in, a_b_in[0], a_v_norm_g[0],
                              wmix, bmix, w_out, rows)
        x2 = _peer(x1, sh_f, sc_f, g_f, norm_ffn_g[0], wq_t[0], subk[0], u_tab, vt_tab, 0, tb, n_sub, eb)
        sh_m, sc_m, g_m, sh_f, sc_f, g_f = mod[1]
        k, v, kb, vb, q = _kvq(x2, kvm[0], kvm[1], sh_m, sc_m, kv_norm_g, norm_mix_g[1], w_kv, w_q1,
                               k_gain, q_gain, rows)
        o = attend(q, k, v, kb, vb)
        x3 = _oproj(o, x2, g_m, w_o1, rows)
        y = _peer(x3, sh_f, sc_f, g_f, norm_ffn_g[1], wq_t[1], subk[1], u_tab, vt_tab, 1, tb, n_sub, eb)
        return y, k, v, v_rows

    def attend_prompt(q, k, v, kb, vb):
        return _sb_prompt(q, kb, vb, b_logit_bias[0], 256, 4)

    def attend_sample(q, k, v, kb, vb):
        qs = q[0, :n_samp].reshape(nbs, n_tok, d)
        ks = k[0, :n_samp].reshape(nbs, n_tok, d)
        vs = v[0, :n_samp].reshape(nbs, n_tok, d)
        full = _sb_paged(qs, ks, vs, cache_k, cache_v, page_table, b_logit_bias[0], 4)
        full = full[:, :N_HEADS * n_tok].reshape(nbs, N_HEADS, n_tok, N_HEADS, HEAD_DIM)
        idx = jnp.arange(N_HEADS)
        o = full[:, idx, :, idx, :]
        o = jnp.transpose(o, (1, 2, 0, 3)).reshape(1, n_samp, d)
        return jnp.pad(o, ((0, 0), (0, SAMPLE_ROWS - n_samp), (0, 0))).astype(BF16)

    y_p, k_p, v_p, _ = trunk(x_prompt, mod_p, kvmod_p, wmix_p, bmix_p, 256, 512, 2, attend_prompt)
    y_s, k_s, v_s, gv_s = trunk(xs, mod_s, kvmod_s, wmix_s, bmix_s, SAMPLE_ROWS, SAMPLE_ROWS, 1, attend_sample)

    def samp(a, shape):
        return a[0, :n_samp].reshape(shape)

    return (y_p,
            samp(y_s, (nbs, n_tok, d)),
            k_p.reshape(nbp, seq, N_HEADS, HEAD_DIM),
            v_p.reshape(nbp, seq, N_HEADS, HEAD_DIM),
            samp(k_s, (nbs, n_tok, N_HEADS, HEAD_DIM)),
            samp(v_s, (nbs, n_tok, N_HEADS, HEAD_DIM)),
            samp(gv_s, (1, nbs, n_tok, width)))
```

```python
import functools
import math

import jax
import jax.numpy as jnp
import numpy as np
from jax import lax
from jax.experimental import pallas as pl
from jax.experimental.pallas import tpu as pltpu

F32 = jnp.float32
BF16 = jnp.bfloat16
EPS = 1e-6

LANES = 128
SUBLANES = 8
BF16_ROWS = 16
VMEM_LIMIT = 56 * 1024 * 1024

CHUNK = 128
GROUPS = 16
N_HEADS = 16
HEAD_DIM = 128
PEER_HEADS = 8
PEER_NKEYS = 128
PEER_TOPK = 16
SAMPLE_ROWS = 128


def _cparams(sem):
    return pltpu.CompilerParams(dimension_semantics=sem, vmem_limit_bytes=VMEM_LIMIT)


def _const_spec(shape):
    nd = len(shape)
    return pl.BlockSpec(shape, lambda *_: (0,) * nd, pipeline_mode=pl.Buffered(1))


def _rms(x, g):
    ms = jnp.mean(x * x, axis=-1, keepdims=True)
    return x * lax.rsqrt(ms + EPS) * g


def _gelu(x):
    return 0.5 * x * (1.0 + lax.erf(x * np.float32(math.sqrt(0.5))))


def _head_rms(x, g):
    cols = []
    for h in range(x.shape[1] // HEAD_DIM):
        seg = x[:, h * HEAD_DIM:(h + 1) * HEAD_DIM]
        ms = jnp.mean(seg * seg, axis=-1, keepdims=True)
        cols.append(seg * lax.rsqrt(ms + EPS))
    return jnp.concatenate(cols, axis=1) * g


def _adaln_body(c_ref, w_ref, b_ref, o_ref):
    c = c_ref[...]
    s = c / (1.0 + jnp.exp(-c))
    o_ref[0] = jnp.dot(s.astype(BF16), w_ref[0].astype(BF16), preferred_element_type=F32) + b_ref[0]


def _adaln(c, w, b):
    n_l, d, n = w.shape
    m = c.shape[0]
    tn = 1024
    return pl.pallas_call(
        _adaln_body,
        grid=(n_l, n // tn),
        in_specs=[pl.BlockSpec((m, d), lambda l, j: (0, 0)),
                  pl.BlockSpec((1, d, tn), lambda l, j: (l, 0, j)),
                  pl.BlockSpec((1, 1, tn), lambda l, j: (l, 0, j))],
        out_specs=pl.BlockSpec((1, m, tn), lambda l, j: (l, 0, j)),
        out_shape=jax.ShapeDtypeStruct((n_l, m, n), F32),
        compiler_params=_cparams(("arbitrary", "arbitrary")),
        name="adaln",
    )(c, w, b.reshape(n_l, 1, n))


def _mod_spec(mod, rows):
    d = mod.shape[-1]
    if mod.shape[1] == 1:
        return pl.BlockSpec((1, 1, d), lambda b, i: (b, 0, 0))
    return pl.BlockSpec((1, rows, d), lambda b, i: (b, i, 0))


def _mixer_a_body(x_ref, sh_ref, sc_ref, gt_ref, ng_ref, win_ref, bin_ref, vg_ref, wmix_ref,
                  bmix_ref, wout_ref, x1_ref, v_ref, *, chunk):
    x = x_ref[0]
    rows = x.shape[0]
    h = _rms(x, ng_ref[...]) * (1.0 + sc_ref[0]) + sh_ref[0]
    z = jnp.dot(h.astype(BF16), win_ref[...], preferred_element_type=F32) + bin_ref[...]
    z = _gelu(z)
    width = z.shape[1] // 2
    u = z[:, :width]
    v = _rms(z[:, width:], vg_ref[...])
    v_ref[0] = v
    vb = v.astype(BF16)
    gd = width // GROUPS
    blocks = []
    for c in range(rows // chunk):
        cols = [jnp.dot(wmix_ref[g], vb[c * chunk:(c + 1) * chunk, g * gd:(g + 1) * gd],
                        preferred_element_type=F32) for g in range(GROUPS)]
        blocks.append(jnp.concatenate(cols, axis=1) + bmix_ref[...])
    mixed = blocks[0] if len(blocks) == 1 else jnp.concatenate(blocks, axis=0)
    um = (u * mixed).astype(BF16)
    mix = jnp.dot(um, wout_ref[...], preferred_element_type=F32)
    x1_ref[0] = x + gt_ref[0] * mix


def _mixer_a(x, shift, scale, gate, norm_g, w_in, b_in, v_g, wmix, bmix, w_out, rows):
    nb, s, d = x.shape
    width = w_out.shape[0]
    chunk = wmix.shape[1]
    xspec = pl.BlockSpec((1, rows, d), lambda b, i: (b, i, 0))
    return pl.pallas_call(
        functools.partial(_mixer_a_body, chunk=chunk),
        grid=(nb, s // rows),
        in_specs=[xspec, _mod_spec(shift, rows), _mod_spec(scale, rows), _mod_spec(gate, rows),
                  _const_spec((1, d)), _const_spec(w_in.shape), _const_spec((1, 2 * width)),
                  _const_spec((1, width)), _const_spec(wmix.shape), _const_spec(bmix.shape),
                  _const_spec(w_out.shape)],
        out_specs=[xspec, pl.BlockSpec((1, rows, width), lambda b, i: (b, i, 0))],
        out_shape=[jax.ShapeDtypeStruct((nb, s, d), F32), jax.ShapeDtypeStruct((nb, s, width), F32)],
        compiler_params=_cparams(("arbitrary", "arbitrary")),
        name="gmlp_mixer",
    )(x, shift, scale, gate, norm_g.reshape(1, d), w_in, b_in.reshape(1, -1), v_g.reshape(1, -1),
      wmix, bmix, w_out)


def _sort_network(n):
    pairs = []
    p = 1
    while p < n:
        k = p
        while k >= 1:
            for j in range(k % p, n - k, 2 * k):
                for i in range(min(k, n - j - k)):
                    if (i + j) // (2 * p) == (i + j + k) // (2 * p):
                        pairs.append((i + j, i + j + k))
            k //= 2
        p *= 2
    return pairs


_SORT16 = _sort_network(PEER_TOPK)
_BITONIC16 = [(i, i | d) for d in (8, 4, 2, 1) for i in range(PEER_TOPK) if not i & d]


def _apply_network(xs, pairs):
    xs = list(xs)
    for i, j in pairs:
        a, b = xs[i], xs[j]
        xs[i] = jnp.maximum(a, b)
        xs[j] = jnp.minimum(a, b)
    return xs


def _merge_top16(xs, ys):
    zs = [jnp.maximum(xs[r], ys[PEER_TOPK - 1 - r]) for r in range(PEER_TOPK)]
    return _apply_network(zs, _BITONIC16)


def _top16_of_keys(s):
    slabs = [s[SUBLANES * g:SUBLANES * (g + 1), :] for g in range(PEER_NKEYS // SUBLANES)]
    xs = _apply_network(slabs, _SORT16)
    for shift in (4, 2, 1):
        ys = [pltpu.roll(x, shift, axis=0) for x in xs]
        xs = _merge_top16(xs, ys)
    return xs


_CAND = [(r, c) for r in range(PEER_TOPK) for c in range(PEER_TOPK) if (r + 1) * (c + 1) <= PEER_TOPK]


def _route_body(x_ref, sh_ref, sc_ref, ng_ref, wq_ref, sk_ref,
                h2t_ref, r2_ref, e2_ref, c1_ref, e1_ref, s_ref):
    x = x_ref[0]
    h = _rms(x, ng_ref[...]) * (1.0 + sc_ref[0]) + sh_ref[0]
    ht = h.T.astype(BF16)
    h2t_ref[...] = ht
    qt = jnp.dot(wq_ref[...], ht, preferred_element_type=F32).astype(BF16)
    for hp in range(2 * PEER_HEADS):
        s_ref[hp] = jnp.dot(sk_ref[hp % 2], qt[hp * PEER_NKEYS:(hp + 1) * PEER_NKEYS, :],
                            preferred_element_type=F32)
    tb = x.shape[0]
    neg = jnp.full((SUBLANES, LANES), -jnp.inf, F32)
    sub = lax.broadcasted_iota(jnp.int32, (SUBLANES, LANES), 0)

    def chunk(lc, carry):
        lanes = pl.ds(pl.multiple_of(lc * LANES, LANES), LANES)
        top = [[neg] * PEER_TOPK, [neg] * PEER_TOPK]
        for hp in range(2 * PEER_HEADS):
            hh, p = hp // 2, hp % 2
            xs = _top16_of_keys(s_ref[hp, :, lanes])
            top[p] = [jnp.where(sub == hh, xs[r], top[p][r]) for r in range(PEER_TOPK)]
        a, b = top
        cand = {rc: a[rc[0]] + b[rc[1]] for rc in _CAND}
        g1 = [cand[(0, c)] for c in range(PEER_TOPK)]
        rest = [cand[rc] for rc in _CAND if rc[0] > 0]
        rest = rest + [neg] * (3 * PEER_TOPK - len(rest))
        gs = [_apply_network(rest[PEER_TOPK * k:PEER_TOPK * (k + 1)], _SORT16) for k in range(3)]
        m1 = _merge_top16(g1, gs[0])
        m2 = _merge_top16(gs[1], gs[2])
        zs = [jnp.maximum(m1[r], m2[PEER_TOPK - 1 - r]) for r in range(PEER_TOPK)]
        tau = functools.reduce(jnp.minimum, zs)
        top_sum = a[0] + b[0]
        zsum = jnp.zeros((SUBLANES, LANES), F32)
        for rc in _CAND:
            zsum = zsum + jnp.where(cand[rc] >= tau, jnp.exp(cand[rc] - top_sum), 0.0)
        zinv = 1.0 / zsum
        for hh in range(PEER_HEADS):
            def row(v):
                return jnp.broadcast_to(v[hh:hh + 1, :], (SUBLANES, LANES))
            bh = [row(b[c]) for c in range(PEER_TOPK)]
            tau_h, zinv_h, a0_h = row(tau), row(zinv), row(a[0])
            for gp in range(PEER_NKEYS // BF16_ROWS):
                r2s, e2s, c1s, e1s = [], [], [], []
                for g in (2 * gp, 2 * gp + 1):
                    rows = slice(SUBLANES * g, SUBLANES * (g + 1))
                    s1 = s_ref[2 * hh, rows, lanes]
                    s2 = s_ref[2 * hh + 1, rows, lanes]
                    r2 = jnp.zeros((SUBLANES, LANES), F32)
                    c1 = jnp.zeros((SUBLANES, LANES), F32)
                    for c in range(PEER_TOPK):
                        r2 = r2 + jnp.where(bh[c] > s2, 1.0, 0.0)
                        c1 = c1 + jnp.where(s1 + bh[c] >= tau_h, 1.0, 0.0)
                    r2s.append(r2)
                    c1s.append(c1)
                    e2s.append(jnp.exp(s2 - bh[0]))
                    e1s.append(jnp.exp(s1 - a0_h) * zinv_h)
                rows16 = slice(BF16_ROWS * gp, BF16_ROWS * (gp + 1))
                r2_ref[hh, rows16, lanes] = jnp.concatenate(r2s, axis=0).astype(BF16)
                e2_ref[hh, rows16, lanes] = jnp.concatenate(e2s, axis=0).astype(BF16)
                c1_ref[hh, rows16, lanes] = jnp.concatenate(c1s, axis=0)
                e1_ref[hh, rows16, lanes] = jnp.concatenate(e1s, axis=0)
        return carry

    lax.fori_loop(0, tb // LANES, chunk, 0)


def _peer_route(x, shift, scale, norm_g, wq_t, subkeys, tb):
    nb, s, d = x.shape
    nblk = s // tb
    nt = nb * nblk
    tspec3 = pl.BlockSpec((None, PEER_HEADS, PEER_NKEYS, tb), lambda b, i: (b * nblk + i, 0, 0, 0))
    return pl.pallas_call(
        _route_body,
        grid=(nb, nblk),
        in_specs=[pl.BlockSpec((1, tb, d), lambda b, i: (b, i, 0)), _mod_spec(shift, tb), _mod_spec(scale, tb),
                  _const_spec((1, d)), _const_spec(wq_t.shape), _const_spec(subkeys.shape)],
        out_specs=[pl.BlockSpec((None, d, tb), lambda b, i: (b * nblk + i, 0, 0)), tspec3, tspec3, tspec3, tspec3],
        out_shape=[jax.ShapeDtypeStruct((nt, d, tb), BF16),
                   jax.ShapeDtypeStruct((nt, PEER_HEADS, PEER_NKEYS, tb), BF16),
                   jax.ShapeDtypeStruct((nt, PEER_HEADS, PEER_NKEYS, tb), BF16),
                   jax.ShapeDtypeStruct((nt, PEER_HEADS, PEER_NKEYS, tb), F32),
                   jax.ShapeDtypeStruct((nt, PEER_HEADS, PEER_NKEYS, tb), F32)],
        scratch_shapes=[pltpu.VMEM((2 * PEER_HEADS, PEER_NKEYS, tb), F32)],
        compiler_params=_cparams(("arbitrary", "arbitrary")),
        name="peer_route",
    )(x, shift, scale, norm_g.reshape(1, d), wq_t, subkeys)


def _peer_gate_stage(act_ref, p_ref, r2_ref, e2_ref, c1_ref, e1_ref, key0, n_i, valid=None):
    n_sub, _, tb = act_ref.shape
    for j in range(n_sub):
        for il in range(n_i):
            c1i = c1_ref[j, key0 + il]
            e1i = e1_ref[j, key0 + il]
            for lc in range(tb // LANES):
                lanes = slice(lc * LANES, (lc + 1) * LANES)
                gate = [jnp.zeros((BF16_ROWS, LANES), BF16)] * (PEER_NKEYS // BF16_ROWS)
                for hh in range(PEER_HEADS):
                    c1b = jnp.broadcast_to(c1i[hh:hh + 1, lanes], (BF16_ROWS, LANES)).astype(BF16)
                    e1b = jnp.broadcast_to(e1i[hh:hh + 1, lanes], (BF16_ROWS, LANES)).astype(BF16)
                    for rb in range(PEER_NKEYS // BF16_ROWS):
                        rows = slice(rb * BF16_ROWS, (rb + 1) * BF16_ROWS)
                        r2 = r2_ref[j, hh, rows, lanes]
                        e2 = e2_ref[j, hh, rows, lanes]
                        gate[rb] = gate[rb] + jnp.where(r2 < c1b, e2, jnp.zeros_like(e2)) * e1b
                for rb in range(PEER_NKEYS // BF16_ROWS):
                    rows = slice(il * PEER_NKEYS + rb * BF16_ROWS, il * PEER_NKEYS + (rb + 1) * BF16_ROWS)
                    a = act_ref[j, rows, lanes]
                    p = (_gelu(a) * gate[rb].astype(F32)).astype(BF16)
                    p_ref[j, rows, lanes] = p if valid is None else jnp.where(valid, p, jnp.zeros_like(p))


def _peer_body(h2t_ref, u_ref, vta_ref, vtb_ref, r2_ref, e2_ref, c1_ref, e1_ref, acc_ref,
               acta_ref, actb_ref, pa_ref, pb_ref, *, n_i):
    s = pl.program_id(1)
    last = pl.num_programs(1) - 1
    n_blocks = 2 * last
    n_sub, eb, _ = acta_ref.shape
    routing = (r2_ref, e2_ref, c1_ref, e1_ref)

    @pl.when(s == 0)
    def _():
        acc_ref[...] = jnp.zeros_like(acc_ref)
        actb_ref[...] = jnp.zeros_like(actb_ref)

    for j in range(n_sub):
        acta_ref[j] = jnp.dot(u_ref[:eb, :], h2t_ref[j], preferred_element_type=F32)
    _peer_gate_stage(actb_ref, pb_ref, *routing, jnp.maximum(2 * s - 1, 0) * n_i, n_i)
    for j in range(n_sub):
        acc_ref[j] += jnp.dot(vta_ref[...], pb_ref[j], preferred_element_type=F32)
    for j in range(n_sub):
        actb_ref[j] = jnp.dot(u_ref[eb:, :], h2t_ref[j], preferred_element_type=F32)
    _peer_gate_stage(acta_ref, pa_ref, *routing, jnp.minimum(2 * s, n_blocks - 1) * n_i, n_i, valid=s < last)
    for j in range(n_sub):
        acc_ref[j] += jnp.dot(vtb_ref[...], pa_ref[j], preferred_element_type=F32)


def _peer_experts(h2t, u_tab, vt_tab, layer, r2, e2, c1, e1, n_sub, eb):
    nt, d, tb = h2t.shape
    n_exp = u_tab.shape[1]
    n_i = eb // PEER_NKEYS
    n_pairs = n_exp // (2 * eb)
    assert vt_tab.shape[1:] == (2 * n_pairs, d, eb) and nt % n_sub == 0
    one = pl.Buffered(1)
    t3 = pl.BlockSpec((n_sub, PEER_HEADS, PEER_NKEYS, tb), lambda i, e: (i, 0, 0, 0), pipeline_mode=one)
    k3 = pl.BlockSpec((n_sub, PEER_NKEYS, PEER_HEADS, tb), lambda i, e: (i, 0, 0, 0), pipeline_mode=one)
    tile3 = pl.BlockSpec((n_sub, d, tb), lambda i, e: (i, 0, 0))
    return pl.pallas_call(
        functools.partial(_peer_body, n_i=n_i),
        grid=(nt // n_sub, n_pairs + 1),
        in_specs=[tile3,
                  pl.BlockSpec((None, 2 * eb, d), lambda i, e: (layer, jnp.minimum(e, n_pairs - 1), 0)),
                  pl.BlockSpec((None, None, d, eb), lambda i, e: (layer, jnp.maximum(2 * e - 1, 0), 0, 0)),
                  pl.BlockSpec((None, None, d, eb), lambda i, e: (layer, jnp.minimum(2 * e, 2 * n_pairs - 1), 0, 0)),
                  t3, t3, k3, k3],
        out_specs=tile3,
        out_shape=jax.ShapeDtypeStruct((nt, d, tb), F32),
        scratch_shapes=[pltpu.VMEM((n_sub, eb, tb), F32), pltpu.VMEM((n_sub, eb, tb), F32),
                        pltpu.VMEM((n_sub, eb, tb), BF16), pltpu.VMEM((n_sub, eb, tb), BF16)],
        compiler_params=_cparams(("arbitrary", "arbitrary")),
        name="peer_experts",
    )(h2t, u_tab, vt_tab, vt_tab, r2, e2, c1, e1)


def _peer_finish_body(acc_ref, x_ref, g_ref, y_ref):
    y_ref[0] = x_ref[0] + g_ref[0] * acc_ref[0].T


def _peer_finish(acc_t, x, gate):
    nb, s, d = x.shape
    tb = acc_t.shape[2]
    nblk = s // tb
    xspec = pl.BlockSpec((1, tb, d), lambda b, i: (b, i, 0))
    return pl.pallas_call(
        _peer_finish_body,
        grid=(nb, nblk),
        in_specs=[pl.BlockSpec((1, d, tb), lambda b, i: (b * nblk + i, 0, 0)), xspec, _mod_spec(gate, tb)],
        out_specs=xspec,
        out_shape=jax.ShapeDtypeStruct((nb, s, d), F32),
        compiler_params=_cparams(("arbitrary", "arbitrary")),
        name="peer_finish",
    )(acc_t, x, gate)


def _peer(x, shift, scale, gate, norm_g, wq_t, subkeys, u_tab, vt_tab, layer, tb, n_sub, eb):
    h2t, r2, e2, c1, e1 = _peer_route(x, shift, scale, norm_g, wq_t, subkeys, tb)
    c1 = jnp.transpose(c1, (0, 2, 1, 3))
    e1 = jnp.transpose(e1, (0, 2, 1, 3))
    acc_t = _peer_experts(h2t, u_tab, vt_tab, layer, r2, e2, c1, e1, n_sub, eb)
    return _peer_finish(acc_t, x, gate)


def _kvq_body(x_ref, ksh_ref, ksc_ref, msh_ref, msc_ref, kng_ref, mng_ref, wkv_ref, wq_ref,
              kg_ref, qg_ref, k_ref, v_ref, kb_ref, vb_ref, q_ref):
    x = x_ref[0]
    d = x.shape[1]
    ms = jnp.mean(x * x, axis=-1, keepdims=True)
    xn = x * lax.rsqrt(ms + EPS)
    hk = (xn * kng_ref[...]) * (1.0 + ksc_ref[0]) + ksh_ref[0]
    kv = jnp.dot(hk.astype(BF16), wkv_ref[...], preferred_element_type=F32)
    k = _head_rms(kv[:, :d], kg_ref[...])
    v = kv[:, d:]
    k_ref[0] = k
    v_ref[0] = v
    kb_ref[0] = k.astype(BF16)
    vb_ref[0] = v.astype(BF16)
    hq = (xn * mng_ref[...]) * (1.0 + msc_ref[0]) + msh_ref[0]
    q = jnp.dot(hq.astype(BF16), wq_ref[...], preferred_element_type=F32)
    q_ref[0] = _head_rms(q, qg_ref[...]).astype(BF16)


def _kvq(x, k_shift, k_scale, m_shift, m_scale, kv_norm_g, mix_norm_g, w_kv, w_q, k_gain, q_gain, rows):
    nb, s, d = x.shape
    xspec = pl.BlockSpec((1, rows, d), lambda b, i: (b, i, 0))
    return pl.pallas_call(
        _kvq_body,
        grid=(nb, s // rows),
        in_specs=[xspec, _mod_spec(k_shift, rows), _mod_spec(k_scale, rows), _mod_spec(m_shift, rows),
                  _mod_spec(m_scale, rows), _const_spec((1, d)), _const_spec((1, d)),
                  _const_spec(w_kv.shape), _const_spec(w_q.shape), _const_spec((1, d)), _const_spec((1, d))],
        out_specs=[xspec] * 5,
        out_shape=[jax.ShapeDtypeStruct((nb, s, d), F32), jax.ShapeDtypeStruct((nb, s, d), F32),
                   jax.ShapeDtypeStruct((nb, s, d), BF16), jax.ShapeDtypeStruct((nb, s, d), BF16),
                   jax.ShapeDtypeStruct((nb, s, d), BF16)],
        compiler_params=_cparams(("arbitrary", "arbitrary")),
        name="kv_q_proj",
    )(x, k_shift, k_scale, m_shift, m_scale, kv_norm_g.reshape(1, d), mix_norm_g.reshape(1, d),
      w_kv, w_q, k_gain, q_gain)


NEG_LOG2E = np.float32(-math.log2(math.e))
SB_NEG_SCALE2 = np.float32(-(HEAD_DIM ** -0.5) * math.log2(math.e))


def _log2_beta(y):
    neg_abs = lax.bitcast_convert_type(lax.bitcast_convert_type(y, jnp.int32) | jnp.int32(-2 ** 31), F32)
    lm = jnp.minimum(y, 0.0) - jnp.log2(1.0 + jnp.exp2(neg_abs))
    return lm - y, lm


def _split_bf16(x):
    hi = x.astype(BF16)
    lo = (x - hi.astype(F32)).astype(BF16)
    return hi, lo


def _sb_prompt_body(q_ref, k_ref, v_ref, bias_ref, tri_ref, o_ref,
                    d_ref, surv_ref, lms_ref, run_ref, acc_ref, *, blk, heads, scale):
    qi = pl.program_id(2)
    hsl = [slice(h * HEAD_DIM, (h + 1) * HEAD_DIM) for h in range(heads)]
    hr = range(heads)

    def logits(kb):
        ks = pl.multiple_of(kb * blk, blk)
        return [lax.dot_general(q_ref[0, :, hsl[h]], k_ref[0, pl.ds(ks, blk), hsl[h]],
                                (((1,), (1,)), ((), ())), preferred_element_type=F32) * scale
                + bias_ref[0, h:h + 1, :] for h in hr]

    def produce(z, mask):
        for h in hr:
            d, lm = _log2_beta(z[h])
            if mask is not None:
                d, lm = jnp.where(mask, d, -jnp.inf), jnp.where(mask, lm, 0.0)
            hi, lo = _split_bf16(lm)
            surv_ref[h] = jnp.dot(jnp.concatenate([hi, lo], axis=1), tri_ref[...],
                                  preferred_element_type=F32)
            d_ref[h] = d
            lms_ref[h] = jnp.sum(lm, axis=1, keepdims=True)

    def consume(kb):
        ks = pl.multiple_of(kb * blk, blk)
        for h in hr:
            a = jnp.exp2(d_ref[h] + surv_ref[h] + run_ref[h])
            acc_ref[h] += jnp.dot(a.astype(BF16), v_ref[0, pl.ds(ks, blk), hsl[h]],
                                  preferred_element_type=F32)
            run_ref[h] += lms_ref[h]

    acc_ref[...] = jnp.zeros_like(acc_ref)
    run_ref[...] = jnp.zeros_like(run_ref)
    row = lax.broadcasted_iota(jnp.int32, (blk, blk), 0)
    col = lax.broadcasted_iota(jnp.int32, (blk, blk), 1)
    produce(logits(qi), col < row)

    def trip(j, carry):
        z = logits(qi - j - 1)
        consume(qi - j)
        produce(z, None)
        return carry

    lax.fori_loop(0, qi, trip, 0)
    consume(0)
    for h in hr:
        o_ref[0, :, hsl[h]] = acc_ref[h].astype(o_ref.dtype)


def _suffix_matrix(n):
    m = (np.arange(n)[:, None] > np.arange(n)[None, :]).astype(np.float32)
    return jnp.asarray(np.concatenate([m, m], axis=0), dtype=BF16)


def _sb_prompt(q, k, v, logit_bias, blk, heads):
    nb, s, d = q.shape
    bias = jnp.broadcast_to((logit_bias.astype(F32) * NEG_LOG2E).reshape(N_HEADS // heads, heads, 1),
                            (N_HEADS // heads, heads, blk))
    qspec = pl.BlockSpec((1, blk, heads * HEAD_DIM), lambda b, h, i: (b, i, h))
    kvspec = pl.BlockSpec((1, s, heads * HEAD_DIM), lambda b, h, i: (b, 0, h))
    return pl.pallas_call(
        functools.partial(_sb_prompt_body, blk=blk, heads=heads, scale=SB_NEG_SCALE2),
        grid=(nb, N_HEADS // heads, s // blk),
        in_specs=[qspec, kvspec, kvspec, pl.BlockSpec((1, heads, blk), lambda b, h, i: (h, 0, 0)),
                  pl.BlockSpec((2 * blk, blk), lambda b, h, i: (0, 0))],
        out_specs=qspec,
        out_shape=jax.ShapeDtypeStruct((nb, s, d), BF16),
        scratch_shapes=[pltpu.VMEM((heads, blk, blk), F32), pltpu.VMEM((heads, blk, blk), F32),
                        pltpu.VMEM((heads, blk, 1), F32), pltpu.VMEM((heads, blk, 1), F32),
                        pltpu.VMEM((heads, blk, HEAD_DIM), F32)],
        compiler_params=_cparams(("arbitrary", "arbitrary", "arbitrary")),
        name="sb_prompt",
    )(q, k, v, bias, _suffix_matrix(blk))


def _sb_paged_body(pt_ref, *refs, pages_per_step, n_q, scale):
    kp = refs[:pages_per_step]
    vp = refs[pages_per_step:2 * pages_per_step]
    wq_ref, bias_ref, knew_ref, vnew_ref, tri_ref, o_ref, acc_ref, run_ref = refs[2 * pages_per_step:]
    step = pl.program_id(1)
    wq = wq_ref[0]
    bias = bias_ref[...]

    def heads_to_lanes(ref):
        n_pos = ref.shape[1] // N_HEADS
        return jnp.concatenate(
            [ref[0, pl.ds(h, n_pos, stride=N_HEADS), :].astype(BF16) for h in range(N_HEADS)], axis=1)

    def block(kblk, vblk, mask):
        d, lm = _log2_beta(jnp.dot(kblk, wq, preferred_element_type=F32) * scale + bias)
        if mask is not None:
            lm = jnp.where(mask, lm, 0.0)
        hi, lo = _split_bf16(lm)
        surv = jnp.dot(tri_ref[...], jnp.concatenate([hi, lo], axis=0),
                       preferred_element_type=F32) + run_ref[...]
        a = jnp.exp2(d + surv)
        if mask is not None:
            a = jnp.where(mask, a, 0.0)
        acc_ref[...] += jnp.dot(a.T.astype(BF16), vblk, preferred_element_type=F32)
        run_ref[...] += jnp.sum(lm, axis=0, keepdims=True)

    @pl.when(step == 0)
    def _():
        acc_ref[...] = jnp.zeros_like(acc_ref)
        run_ref[...] = jnp.zeros_like(run_ref)
        shape = (knew_ref.shape[1], wq.shape[1])
        kpos = lax.broadcasted_iota(jnp.int32, shape, 0)
        qpos = lax.broadcasted_iota(jnp.int32, shape, 1) % n_q
        block(knew_ref[0].astype(BF16), vnew_ref[0].astype(BF16), kpos < qpos)

    ds, survs, tots = [], [], []
    for i in range(pages_per_step):
        d, lm = _log2_beta(jnp.dot(heads_to_lanes(kp[i]), wq, preferred_element_type=F32) * scale + bias)
        hi, lo = _split_bf16(lm)
        ds.append(d)
        survs.append(jnp.dot(tri_ref[...], jnp.concatenate([hi, lo], axis=0), preferred_element_type=F32))
        tots.append(jnp.sum(lm, axis=0, keepdims=True))
    run = run_ref[...]
    weights = []
    for i in range(pages_per_step):
        weights.append(jnp.exp2(ds[i] + survs[i] + run).T.astype(BF16))
        run = run + tots[i]
    run_ref[...] = run
    acc_ref[...] += jnp.dot(jnp.concatenate(weights, axis=1),
                            jnp.concatenate([heads_to_lanes(vp[i]) for i in range(pages_per_step)], axis=0),
                            preferred_element_type=F32)

    @pl.when(step == pl.num_programs(1) - 1)
    def _():
        o_ref[0] = acc_ref[...]


def _suffix_matrix_t(n):
    m = (np.arange(n)[None, :] > np.arange(n)[:, None]).astype(np.float32)
    return jnp.asarray(np.concatenate([m, m], axis=1), dtype=BF16)


def _sb_paged(q, k_new, v_new, cache_k, cache_v, page_table, logit_bias, pages_per_step):
    nb, n_q, d = q.shape
    n_pool, page = cache_k.shape[:2]
    n_pages = page_table.shape[1]
    ck = cache_k.reshape(n_pool, page * N_HEADS, HEAD_DIM)
    cv = cache_v.reshape(n_pool, page * N_HEADS, HEAD_DIM)
    ncol = LANES
    assert N_HEADS * n_q <= ncol and n_q <= page
    qh = q.reshape(nb, n_q, N_HEADS, HEAD_DIM)
    eye = jnp.eye(N_HEADS, dtype=BF16)
    wq = jnp.einsum('bqhd,hg->bhdgq', qh, eye).reshape(nb, d, N_HEADS * n_q)
    wq = jnp.pad(wq, ((0, 0), (0, 0), (0, ncol - N_HEADS * n_q)))
    bias = jnp.pad(jnp.repeat(logit_bias.astype(F32) * NEG_LOG2E, n_q), (0, ncol - N_HEADS * n_q)).reshape(1, ncol)
    knew = jnp.pad(k_new, ((0, 0), (0, page - n_q), (0, 0)))
    vnew = jnp.pad(v_new, ((0, 0), (0, page - n_q), (0, 0)))
    pt = page_table.reshape(-1).astype(jnp.int32)

    def page_map(i):
        return lambda b, s, pt_ref: (pt_ref[b * n_pages + n_pages - 1 - (s * pages_per_step + i)], 0, 0)

    page_specs = [pl.BlockSpec((1, page * N_HEADS, HEAD_DIM), page_map(i)) for i in range(pages_per_step)]
    grid_spec = pltpu.PrefetchScalarGridSpec(
        num_scalar_prefetch=1,
        grid=(nb, n_pages // pages_per_step),
        in_specs=page_specs + page_specs + [
            pl.BlockSpec((1, d, ncol), lambda b, s, pt_ref: (b, 0, 0)),
            pl.BlockSpec((1, ncol), lambda b, s, pt_ref: (0, 0)),
            pl.BlockSpec((1, page, d), lambda b, s, pt_ref: (b, 0, 0)),
            pl.BlockSpec((1, page, d), lambda b, s, pt_ref: (b, 0, 0)),
            pl.BlockSpec((page, 2 * page), lambda b, s, pt_ref: (0, 0))],
        out_specs=pl.BlockSpec((1, ncol, d), lambda b, s, pt_ref: (b, 0, 0)),
        scratch_shapes=[pltpu.VMEM((ncol, d), F32), pltpu.VMEM((1, ncol), F32)])
    return pl.pallas_call(
        functools.partial(_sb_paged_body, pages_per_step=pages_per_step, n_q=n_q,
                          scale=SB_NEG_SCALE2),
        grid_spec=grid_spec,
        out_shape=jax.ShapeDtypeStruct((nb, ncol, d), F32),
        compiler_params=_cparams(("arbitrary", "arbitrary")),
        name="sb_paged",
    )(pt, *([ck] * pages_per_step), *([cv] * pages_per_step), wq, bias, knew, vnew,
      _suffix_matrix_t(page))


def _oproj_body(o_ref, x_ref, gt_ref, w_ref, y_ref):
    mix = jnp.dot(o_ref[0], w_ref[...], preferred_element_type=F32)
    y_ref[0] = x_ref[0] + gt_ref[0] * mix


def _oproj(o, x, gate, w_o, rows):
    nb, s, d = x.shape
    xspec = pl.BlockSpec((1, rows, d), lambda b, i: (b, i, 0))
    return pl.pallas_call(
        _oproj_body,
        grid=(nb, s // rows),
        in_specs=[xspec, xspec, _mod_spec(gate, rows), _const_spec(w_o.shape)],
        out_specs=xspec,
        out_shape=jax.ShapeDtypeStruct((nb, s, d), F32),
        compiler_params=_cparams(("arbitrary", "arbitrary")),
        name="attn_out_proj",
    )(o, x, gate, w_o)


def _mix_weights_prompt(w_s, b_s, width):
    causal = jnp.tril(jnp.ones((CHUNK, CHUNK), dtype=bool))
    wmix = jnp.where(causal[None], w_s, 0.0).astype(BF16)
    bmix = jnp.repeat(b_s.T, width // GROUPS, axis=1)
    return wmix, bmix


def _mix_weights_sample(w_s, b_s, width, n_seq, n_tok):
    causal = jnp.tril(jnp.ones((n_tok, n_tok), dtype=bool))
    small = jnp.where(causal[None], w_s[:, :n_tok, :n_tok], 0.0)
    eye = jnp.eye(n_seq, dtype=F32)
    blockdiag = jnp.einsum('ab,gts->gatbs', eye, small).reshape(GROUPS, n_seq * n_tok, n_seq * n_tok)
    padn = SAMPLE_ROWS - n_seq * n_tok
    wmix = jnp.pad(blockdiag, ((0, 0), (0, padn), (0, padn))).astype(BF16)
    brow = jnp.tile(b_s.T[:n_tok], (n_seq, 1))
    bmix = jnp.repeat(jnp.pad(brow, ((0, padn), (0, 0))), width // GROUPS, axis=1)
    return wmix, bmix


def kernel(x_prompt, x_sample, cache_k, cache_v, page_table, c_prompt, c_sample, mod_w, mod_b, norm_mix_g, norm_ffn_g, a_w_in, a_b_in, a_v_norm_g, a_w_s, a_b_s, a_w_out, kv_mod_w, kv_mod_b, kv_norm_g, kv_w, k_norm_g, b_w_q, b_q_norm_g, b_logit_bias, b_w_o, peer_w_q, peer_subkeys, peer_u, peer_v):
    nbp, seq, d = x_prompt.shape
    nbs, n_tok, _ = x_sample.shape
    n_samp = nbs * n_tok
    width = a_w_out.shape[1]

    c_all = jnp.concatenate([c_prompt, c_sample], axis=0)
    c_all = jnp.pad(c_all, ((0, BF16_ROWS - c_all.shape[0]), (0, 0)))
    mods = _adaln(c_all, mod_w, mod_b)
    kvmod = _adaln(c_all, kv_mod_w[None], kv_mod_b[None])[0]

    def split_mod(m, n):
        parts = jnp.split(m, n, axis=-1)
        prm = [p[:nbp, None, :] for p in parts]
        smp = [jnp.pad(jnp.repeat(p[nbp:nbp + nbs], n_tok, axis=0), ((0, SAMPLE_ROWS - n_samp), (0, 0)))[None]
               for p in parts]
        return prm, smp

    mod_p, mod_s = zip(*[split_mod(mods[l], 6) for l in range(mods.shape[0])])
    kvmod_p, kvmod_s = split_mod(kvmod, 2)

    w_in = a_w_in[0].astype(BF16)
    w_out = a_w_out[0].astype(BF16)
    wq_t = [jnp.transpose(peer_w_q[l]).astype(BF16) for l in range(2)]
    subk = [peer_subkeys[l].astype(BF16) for l in range(2)]
    u_tab = peer_u.astype(BF16)
    eb = 256
    n_l, n_exp, _ = peer_v.shape
    vt_tab = jnp.transpose(peer_v.reshape(n_l, n_exp // eb, eb, d), (0, 1, 3, 2)).astype(BF16)
    w_kv = kv_w.astype(BF16)
    w_q1 = b_w_q[0].astype(BF16)
    w_o1 = b_w_o[0].astype(BF16)
    k_gain = jnp.tile(k_norm_g, N_HEADS).reshape(1, d)
    q_gain = jnp.tile(b_q_norm_g[0], N_HEADS).reshape(1, d)
    wmix_p, bmix_p = _mix_weights_prompt(a_w_s[0], a_b_s[0], width)
    wmix_s, bmix_s = _mix_weights_sample(a_w_s[0], a_b_s[0], width, nbs, n_tok)

    xs = jnp.pad(x_sample.reshape(1, n_samp, d), ((0, 0), (0, SAMPLE_ROWS - n_samp), (0, 0)))

    def trunk(x, mod, kvm, wmix, bmix, rows, tb, n_sub, attend):
        sh_m, sc_m, g_m, sh_f, sc_f, g_f = mod[0]
        x1, v_rows = _mixer_a(x, sh_m, sc_m, g_m, norm_mix_g[0], w_in, a_b_in[0], a_v_norm_g[0],
                              wmix, bmix, w_out, rows)
        x2 = _peer(x1, sh_f, sc_f, g_f, norm_ffn_g[0], wq_t[0], subk[0], u_tab, vt_tab, 0, tb, n_sub, eb)
        sh_m, sc_m, g_m, sh_f, sc_f, g_f = mod[1]
        k, v, kb, vb, q = _kvq(x2, kvm[0], kvm[1], sh_m, sc_m, kv_norm_g, norm_mix_g[1], w_kv, w_q1,
                               k_gain, q_gain, rows)
        o = attend(q, k, v, kb, vb)
        x3 = _oproj(o, x2, g_m, w_o1, rows)
        y = _peer(x3, sh_f, sc_f, g_f, norm_ffn_g[1], wq_t[1], subk[1], u_tab, vt_tab, 1, tb, n_sub, eb)
        return y, k, v, v_rows

    def attend_prompt(q, k, v, kb, vb):
        return _sb_prompt(q, kb, vb, b_logit_bias[0], 256, 8)

    def attend_sample(q, k, v, kb, vb):
        qs = q[0, :n_samp].reshape(nbs, n_tok, d)
        ks = k[0, :n_samp].reshape(nbs, n_tok, d)
        vs = v[0, :n_samp].reshape(nbs, n_tok, d)
        full = _sb_paged(qs, ks, vs, cache_k, cache_v, page_table, b_logit_bias[0], 8)
        full = full[:, :N_HEADS * n_tok].reshape(nbs, N_HEADS, n_tok, N_HEADS, HEAD_DIM)
        idx = jnp.arange(N_HEADS)
        o = full[:, idx, :, idx, :]
        o = jnp.transpose(o, (1, 2, 0, 3)).reshape(1, n_samp, d)
        return jnp.pad(o, ((0, 0), (0, SAMPLE_ROWS - n_samp), (0, 0))).astype(BF16)

    y_p, k_p, v_p, _ = trunk(x_prompt, mod_p, kvmod_p, wmix_p, bmix_p, 256, 512, 2, attend_prompt)
    y_s, k_s, v_s, gv_s = trunk(xs, mod_s, kvmod_s, wmix_s, bmix_s, SAMPLE_ROWS, SAMPLE_ROWS, 1, attend_sample)

    def samp(a, shape):
        return a[0, :n_samp].reshape(shape)

    return (y_p,
            samp(y_s, (nbs, n_tok, d)),
            k_p.reshape(nbp, seq, N_HEADS, HEAD_DIM),
            v_p.reshape(nbp, seq, N_HEADS, HEAD_DIM),
            samp(k_s, (nbs, n_tok, N_HEADS, HEAD_DIM)),
            samp(v_s, (nbs, n_tok, N_HEADS, HEAD_DIM)),
            samp(gv_s, (1, nbs, n_tok, width)))
```

```python
import functools
import math

import jax
import jax.numpy as jnp
import numpy as np
from jax import lax
from jax.experimental import pallas as pl
from jax.experimental.pallas import tpu as pltpu

F32 = jnp.float32
BF16 = jnp.bfloat16
EPS = 1e-6

LANES = 128
SUBLANES = 8
BF16_ROWS = 16
VMEM_LIMIT = 56 * 1024 * 1024

CHUNK = 128
GROUPS = 16
N_HEADS = 16
HEAD_DIM = 128
PEER_HEADS = 8
PEER_NKEYS = 128
PEER_TOPK = 16
SAMPLE_ROWS = 128


def _cparams(sem):
    return pltpu.CompilerParams(dimension_semantics=sem, vmem_limit_bytes=VMEM_LIMIT)


def _const_spec(shape):
    nd = len(shape)
    return pl.BlockSpec(shape, lambda *_: (0,) * nd, pipeline_mode=pl.Buffered(1))


def _rms(x, g):
    ms = jnp.mean(x * x, axis=-1, keepdims=True)
    return x * lax.rsqrt(ms + EPS) * g


def _gelu(x):
    return 0.5 * x * (1.0 + lax.erf(x * np.float32(math.sqrt(0.5))))


def _head_rms(x, g):
    cols = []
    for h in range(x.shape[1] // HEAD_DIM):
        seg = x[:, h * HEAD_DIM:(h + 1) * HEAD_DIM]
        ms = jnp.mean(seg * seg, axis=-1, keepdims=True)
        cols.append(seg * lax.rsqrt(ms + EPS))
    return jnp.concatenate(cols, axis=1) * g


def _adaln_body(c_ref, w_ref, b_ref, o_ref):
    c = c_ref[...]
    s = c / (1.0 + jnp.exp(-c))
    o_ref[0] = jnp.dot(s.astype(BF16), w_ref[0].astype(BF16), preferred_element_type=F32) + b_ref[0]


def _adaln(c, w, b):
    n_l, d, n = w.shape
    m = c.shape[0]
    tn = 1024
    return pl.pallas_call(
        _adaln_body,
        grid=(n_l, n // tn),
        in_specs=[pl.BlockSpec((m, d), lambda l, j: (0, 0)),
                  pl.BlockSpec((1, d, tn), lambda l, j: (l, 0, j)),
                  pl.BlockSpec((1, 1, tn), lambda l, j: (l, 0, j))],
        out_specs=pl.BlockSpec((1, m, tn), lambda l, j: (l, 0, j)),
        out_shape=jax.ShapeDtypeStruct((n_l, m, n), F32),
        compiler_params=_cparams(("arbitrary", "arbitrary")),
        name="adaln",
    )(c, w, b.reshape(n_l, 1, n))


def _mod_spec(mod, rows):
    d = mod.shape[-1]
    if mod.shape[1] == 1:
        return pl.BlockSpec((1, 1, d), lambda b, i: (b, 0, 0))
    return pl.BlockSpec((1, rows, d), lambda b, i: (b, i, 0))


def _mixer_a_body(x_ref, sh_ref, sc_ref, gt_ref, ng_ref, win_ref, bin_ref, vg_ref, wmix_ref,
                  bmix_ref, wout_ref, x1_ref, v_ref, *, chunk):
    x = x_ref[0]
    rows = x.shape[0]
    h = _rms(x, ng_ref[...]) * (1.0 + sc_ref[0]) + sh_ref[0]
    z = jnp.dot(h.astype(BF16), win_ref[...], preferred_element_type=F32) + bin_ref[...]
    z = _gelu(z)
    width = z.shape[1] // 2
    u = z[:, :width]
    v = _rms(z[:, width:], vg_ref[...])
    v_ref[0] = v
    vb = v.astype(BF16)
    gd = width // GROUPS
    blocks = []
    for c in range(rows // chunk):
        cols = [jnp.dot(wmix_ref[g], vb[c * chunk:(c + 1) * chunk, g * gd:(g + 1) * gd],
                        preferred_element_type=F32) for g in range(GROUPS)]
        blocks.append(jnp.concatenate(cols, axis=1) + bmix_ref[...])
    mixed = blocks[0] if len(blocks) == 1 else jnp.concatenate(blocks, axis=0)
    um = (u * mixed).astype(BF16)
    mix = jnp.dot(um, wout_ref[...], preferred_element_type=F32)
    x1_ref[0] = x + gt_ref[0] * mix


def _mixer_a(x, shift, scale, gate, norm_g, w_in, b_in, v_g, wmix, bmix, w_out, rows):
    nb, s, d = x.shape
    width = w_out.shape[0]
    chunk = wmix.shape[1]
    xspec = pl.BlockSpec((1, rows, d), lambda b, i: (b, i, 0))
    return pl.pallas_call(
        functools.partial(_mixer_a_body, chunk=chunk),
        grid=(nb, s // rows),
        in_specs=[xspec, _mod_spec(shift, rows), _mod_spec(scale, rows), _mod_spec(gate, rows),
                  _const_spec((1, d)), _const_spec(w_in.shape), _const_spec((1, 2 * width)),
                  _const_spec((1, width)), _const_spec(wmix.shape), _const_spec(bmix.shape),
                  _const_spec(w_out.shape)],
        out_specs=[xspec, pl.BlockSpec((1, rows, width), lambda b, i: (b, i, 0))],
        out_shape=[jax.ShapeDtypeStruct((nb, s, d), F32), jax.ShapeDtypeStruct((nb, s, width), F32)],
        compiler_params=_cparams(("arbitrary", "arbitrary")),
        name="gmlp_mixer",
    )(x, shift, scale, gate, norm_g.reshape(1, d), w_in, b_in.reshape(1, -1), v_g.reshape(1, -1),
      wmix, bmix, w_out)


def _sort_network(n):
    pairs = []
    p = 1
    while p < n:
        k = p
        while k >= 1:
            for j in range(k % p, n - k, 2 * k):
                for i in range(min(k, n - j - k)):
                    if (i + j) // (2 * p) == (i + j + k) // (2 * p):
                        pairs.append((i + j, i + j + k))
            k //= 2
        p *= 2
    return pairs


_SORT16 = _sort_network(PEER_TOPK)
_BITONIC16 = [(i, i | d) for d in (8, 4, 2, 1) for i in range(PEER_TOPK) if not i & d]


def _apply_network(xs, pairs):
    xs = list(xs)
    for i, j in pairs:
        a, b = xs[i], xs[j]
        xs[i] = jnp.maximum(a, b)
        xs[j] = jnp.minimum(a, b)
    return xs


def _merge_top16(xs, ys):
    zs = [jnp.maximum(xs[r], ys[PEER_TOPK - 1 - r]) for r in range(PEER_TOPK)]
    return _apply_network(zs, _BITONIC16)


def _top16_of_keys(s):
    slabs = [s[SUBLANES * g:SUBLANES * (g + 1), :] for g in range(PEER_NKEYS // SUBLANES)]
    xs = _apply_network(slabs, _SORT16)
    for shift in (4, 2, 1):
        ys = [pltpu.roll(x, shift, axis=0) for x in xs]
        xs = _merge_top16(xs, ys)
    return xs


_CAND = [(r, c) for r in range(PEER_TOPK) for c in range(PEER_TOPK) if (r + 1) * (c + 1) <= PEER_TOPK]


def _route_body(x_ref, sh_ref, sc_ref, ng_ref, wq_ref, sk_ref,
                h2t_ref, r2_ref, e2_ref, c1_ref, e1_ref, s_ref):
    x = x_ref[0]
    h = _rms(x, ng_ref[...]) * (1.0 + sc_ref[0]) + sh_ref[0]
    ht = h.T.astype(BF16)
    h2t_ref[...] = ht
    qt = jnp.dot(wq_ref[...], ht, preferred_element_type=F32).astype(BF16)
    for hp in range(2 * PEER_HEADS):
        s_ref[hp] = jnp.dot(sk_ref[hp % 2], qt[hp * PEER_NKEYS:(hp + 1) * PEER_NKEYS, :],
                            preferred_element_type=F32)
    tb = x.shape[0]
    neg = jnp.full((SUBLANES, LANES), -jnp.inf, F32)
    sub = lax.broadcasted_iota(jnp.int32, (SUBLANES, LANES), 0)

    def chunk(lc, carry):
        lanes = pl.ds(pl.multiple_of(lc * LANES, LANES), LANES)
        top = [[neg] * PEER_TOPK, [neg] * PEER_TOPK]
        for hp in range(2 * PEER_HEADS):
            hh, p = hp // 2, hp % 2
            xs = _top16_of_keys(s_ref[hp, :, lanes])
            top[p] = [jnp.where(sub == hh, xs[r], top[p][r]) for r in range(PEER_TOPK)]
        a, b = top
        cand = {rc: a[rc[0]] + b[rc[1]] for rc in _CAND}
        g1 = [cand[(0, c)] for c in range(PEER_TOPK)]
        rest = [cand[rc] for rc in _CAND if rc[0] > 0]
        rest = rest + [neg] * (3 * PEER_TOPK - len(rest))
        gs = [_apply_network(rest[PEER_TOPK * k:PEER_TOPK * (k + 1)], _SORT16) for k in range(3)]
        m1 = _merge_top16(g1, gs[0])
        m2 = _merge_top16(gs[1], gs[2])
        zs = [jnp.maximum(m1[r], m2[PEER_TOPK - 1 - r]) for r in range(PEER_TOPK)]
        tau = functools.reduce(jnp.minimum, zs)
        top_sum = a[0] + b[0]
        zsum = jnp.zeros((SUBLANES, LANES), F32)
        for rc in _CAND:
            zsum = zsum + jnp.where(cand[rc] >= tau, jnp.exp(cand[rc] - top_sum), 0.0)
        zinv = 1.0 / zsum
        for hh in range(PEER_HEADS):
            def row(v):
                return jnp.broadcast_to(v[hh:hh + 1, :], (SUBLANES, LANES))
            bh = [row(b[c]) for c in range(PEER_TOPK)]
            tau_h, zinv_h, a0_h = row(tau), row(zinv), row(a[0])
            for gp in range(PEER_NKEYS // BF16_ROWS):
                r2s, e2s, c1s, e1s = [], [], [], []
                for g in (2 * gp, 2 * gp + 1):
                    rows = slice(SUBLANES * g, SUBLANES * (g + 1))
                    s1 = s_ref[2 * hh, rows, lanes]
                    s2 = s_ref[2 * hh + 1, rows, lanes]
                    r2 = jnp.zeros((SUBLANES, LANES), F32)
                    c1 = jnp.zeros((SUBLANES, LANES), F32)
                    for c in range(PEER_TOPK):
                        r2 = r2 + jnp.where(bh[c] > s2, 1.0, 0.0)
                        c1 = c1 + jnp.where(s1 + bh[c] >= tau_h, 1.0, 0.0)
                    r2s.append(r2)
                    c1s.append(c1)
                    e2s.append(jnp.exp(s2 - bh[0]))
                    e1s.append(jnp.exp(s1 - a0_h) * zinv_h)
                rows16 = slice(BF16_ROWS * gp, BF16_ROWS * (gp + 1))
                r2_ref[hh, rows16, lanes] = jnp.concatenate(r2s, axis=0).astype(BF16)
                e2_ref[hh, rows16, lanes] = jnp.concatenate(e2s, axis=0).astype(BF16)
                key_rows = pl.ds(BF16_ROWS * gp * PEER_HEADS + hh, BF16_ROWS, stride=PEER_HEADS)
                c1_ref[lc, key_rows, :] = jnp.concatenate(c1s, axis=0)
                e1_ref[lc, key_rows, :] = jnp.concatenate(e1s, axis=0)
        return carry

    lax.fori_loop(0, tb // LANES, chunk, 0)


def _peer_route(x, shift, scale, norm_g, wq_t, subkeys, tb):
    nb, s, d = x.shape
    nblk = s // tb
    nt = nb * nblk
    tspec3 = pl.BlockSpec((None, PEER_HEADS, PEER_NKEYS, tb), lambda b, i: (b * nblk + i, 0, 0, 0))
    kh_shape = (tb // LANES, PEER_NKEYS * PEER_HEADS, LANES)
    kspec = pl.BlockSpec((None,) + kh_shape, lambda b, i: (b * nblk + i, 0, 0, 0))
    return pl.pallas_call(
        _route_body,
        grid=(nb, nblk),
        in_specs=[pl.BlockSpec((1, tb, d), lambda b, i: (b, i, 0)), _mod_spec(shift, tb), _mod_spec(scale, tb),
                  _const_spec((1, d)), _const_spec(wq_t.shape), _const_spec(subkeys.shape)],
        out_specs=[pl.BlockSpec((None, d, tb), lambda b, i: (b * nblk + i, 0, 0)), tspec3, tspec3, kspec, kspec],
        out_shape=[jax.ShapeDtypeStruct((nt, d, tb), BF16),
                   jax.ShapeDtypeStruct((nt, PEER_HEADS, PEER_NKEYS, tb), BF16),
                   jax.ShapeDtypeStruct((nt, PEER_HEADS, PEER_NKEYS, tb), BF16),
                   jax.ShapeDtypeStruct((nt,) + kh_shape, F32),
                   jax.ShapeDtypeStruct((nt,) + kh_shape, F32)],
        scratch_shapes=[pltpu.VMEM((2 * PEER_HEADS, PEER_NKEYS, tb), F32)],
        compiler_params=_cparams(("arbitrary", "arbitrary")),
        name="peer_route",
    )(x, shift, scale, norm_g.reshape(1, d), wq_t, subkeys)


def _peer_gate_stage(act_ref, p_ref, r2_ref, e2_ref, c1_ref, e1_ref, key0, n_i, valid=None):
    n_sub, _, tb = act_ref.shape
    for j in range(n_sub):
        for il in range(n_i):
            key_rows = pl.ds(pl.multiple_of((key0 + il) * PEER_HEADS, PEER_HEADS), PEER_HEADS)
            for lc in range(tb // LANES):
                lanes = slice(lc * LANES, (lc + 1) * LANES)
                c1i = c1_ref[j, lc, key_rows, :]
                e1i = e1_ref[j, lc, key_rows, :]
                gate = [jnp.zeros((BF16_ROWS, LANES), BF16)] * (PEER_NKEYS // BF16_ROWS)
                for hh in range(PEER_HEADS):
                    c1b = jnp.broadcast_to(c1i[hh:hh + 1, :], (BF16_ROWS, LANES)).astype(BF16)
                    e1b = jnp.broadcast_to(e1i[hh:hh + 1, :], (BF16_ROWS, LANES)).astype(BF16)
                    for rb in range(PEER_NKEYS // BF16_ROWS):
                        rows = slice(rb * BF16_ROWS, (rb + 1) * BF16_ROWS)
                        r2 = r2_ref[j, hh, rows, lanes]
                        e2 = e2_ref[j, hh, rows, lanes]
                        gate[rb] = gate[rb] + jnp.where(r2 < c1b, e2, jnp.zeros_like(e2)) * e1b
                for rb in range(PEER_NKEYS // BF16_ROWS):
                    rows = slice(il * PEER_NKEYS + rb * BF16_ROWS, il * PEER_NKEYS + (rb + 1) * BF16_ROWS)
                    a = act_ref[j, rows, lanes]
                    p = (_gelu(a) * gate[rb].astype(F32)).astype(BF16)
                    p_ref[j, rows, lanes] = p if valid is None else jnp.where(valid, p, jnp.zeros_like(p))


def _peer_body(h2t_ref, u_ref, vta_ref, vtb_ref, r2_ref, e2_ref, c1_ref, e1_ref, acc_ref,
               acta_ref, actb_ref, pa_ref, pb_ref, *, n_i):
    s = pl.program_id(1)
    last = pl.num_programs(1) - 1
    n_blocks = 2 * last
    n_sub, eb, _ = acta_ref.shape
    routing = (r2_ref, e2_ref, c1_ref, e1_ref)

    @pl.when(s == 0)
    def _():
        acc_ref[...] = jnp.zeros_like(acc_ref)
        actb_ref[...] = jnp.zeros_like(actb_ref)

    for j in range(n_sub):
        acta_ref[j] = jnp.dot(u_ref[:eb, :], h2t_ref[j], preferred_element_type=F32)
    _peer_gate_stage(actb_ref, pb_ref, *routing, jnp.maximum(2 * s - 1, 0) * n_i, n_i)
    for j in range(n_sub):
        acc_ref[j] += jnp.dot(vta_ref[...], pb_ref[j], preferred_element_type=F32)
    for j in range(n_sub):
        actb_ref[j] = jnp.dot(u_ref[eb:, :], h2t_ref[j], preferred_element_type=F32)
    _peer_gate_stage(acta_ref, pa_ref, *routing, jnp.minimum(2 * s, n_blocks - 1) * n_i, n_i, valid=s < last)
    for j in range(n_sub):
        acc_ref[j] += jnp.dot(vtb_ref[...], pa_ref[j], preferred_element_type=F32)


def _peer_experts(h2t, u_tab, vt_tab, layer, r2, e2, c1, e1, n_sub, eb):
    nt, d, tb = h2t.shape
    n_exp = u_tab.shape[1]
    n_i = eb // PEER_NKEYS
    n_pairs = n_exp // (2 * eb)
    assert vt_tab.shape[1:] == (2 * n_pairs, d, eb) and nt % n_sub == 0
    one = pl.Buffered(1)
    t3 = pl.BlockSpec((n_sub, PEER_HEADS, PEER_NKEYS, tb), lambda i, e: (i, 0, 0, 0), pipeline_mode=one)
    k3 = pl.BlockSpec((n_sub, tb // LANES, PEER_NKEYS * PEER_HEADS, LANES), lambda i, e: (i, 0, 0, 0),
                      pipeline_mode=one)
    tile3 = pl.BlockSpec((n_sub, d, tb), lambda i, e: (i, 0, 0))
    return pl.pallas_call(
        functools.partial(_peer_body, n_i=n_i),
        grid=(nt // n_sub, n_pairs + 1),
        in_specs=[tile3,
                  pl.BlockSpec((None, 2 * eb, d), lambda i, e: (layer, jnp.minimum(e, n_pairs - 1), 0)),
                  pl.BlockSpec((None, None, d, eb), lambda i, e: (layer, jnp.maximum(2 * e - 1, 0), 0, 0)),
                  pl.BlockSpec((None, None, d, eb), lambda i, e: (layer, jnp.minimum(2 * e, 2 * n_pairs - 1), 0, 0)),
                  t3, t3, k3, k3],
        out_specs=tile3,
        out_shape=jax.ShapeDtypeStruct((nt, d, tb), F32),
        scratch_shapes=[pltpu.VMEM((n_sub, eb, tb), F32), pltpu.VMEM((n_sub, eb, tb), F32),
                        pltpu.VMEM((n_sub, eb, tb), BF16), pltpu.VMEM((n_sub, eb, tb), BF16)],
        compiler_params=_cparams(("arbitrary", "arbitrary")),
        name="peer_experts",
    )(h2t, u_tab, vt_tab, vt_tab, r2, e2, c1, e1)


def _peer_finish_body(acc_ref, x_ref, g_ref, y_ref):
    y_ref[0] = x_ref[0] + g_ref[0] * acc_ref[0].T


def _peer_finish(acc_t, x, gate):
    nb, s, d = x.shape
    tb = acc_t.shape[2]
    nblk = s // tb
    xspec = pl.BlockSpec((1, tb, d), lambda b, i: (b, i, 0))
    return pl.pallas_call(
        _peer_finish_body,
        grid=(nb, nblk),
        in_specs=[pl.BlockSpec((1, d, tb), lambda b, i: (b * nblk + i, 0, 0)), xspec, _mod_spec(gate, tb)],
        out_specs=xspec,
        out_shape=jax.ShapeDtypeStruct((nb, s, d), F32),
        compiler_params=_cparams(("arbitrary", "arbitrary")),
        name="peer_finish",
    )(acc_t, x, gate)


def _peer(x, shift, scale, gate, norm_g, wq_t, subkeys, u_tab, vt_tab, layer, tb, n_sub, eb):
    h2t, r2, e2, c1, e1 = _peer_route(x, shift, scale, norm_g, wq_t, subkeys, tb)
    acc_t = _peer_experts(h2t, u_tab, vt_tab, layer, r2, e2, c1, e1, n_sub, eb)
    return _peer_finish(acc_t, x, gate)


def _kvq_body(x_ref, ksh_ref, ksc_ref, msh_ref, msc_ref, kng_ref, mng_ref, wkv_ref, wq_ref,
              kg_ref, qg_ref, k_ref, v_ref, kb_ref, vb_ref, q_ref):
    x = x_ref[0]
    d = x.shape[1]
    ms = jnp.mean(x * x, axis=-1, keepdims=True)
    xn = x * lax.rsqrt(ms + EPS)
    hk = (xn * kng_ref[...]) * (1.0 + ksc_ref[0]) + ksh_ref[0]
    kv = jnp.dot(hk.astype(BF16), wkv_ref[...], preferred_element_type=F32)
    k = _head_rms(kv[:, :d], kg_ref[...])
    v = kv[:, d:]
    k_ref[0] = k
    v_ref[0] = v
    kb_ref[0] = k.astype(BF16)
    vb_ref[0] = v.astype(BF16)
    hq = (xn * mng_ref[...]) * (1.0 + msc_ref[0]) + msh_ref[0]
    q = jnp.dot(hq.astype(BF16), wq_ref[...], preferred_element_type=F32)
    q_ref[0] = _head_rms(q, qg_ref[...]).astype(BF16)


def _kvq(x, k_shift, k_scale, m_shift, m_scale, kv_norm_g, mix_norm_g, w_kv, w_q, k_gain, q_gain, rows):
    nb, s, d = x.shape
    xspec = pl.BlockSpec((1, rows, d), lambda b, i: (b, i, 0))
    return pl.pallas_call(
        _kvq_body,
        grid=(nb, s // rows),
        in_specs=[xspec, _mod_spec(k_shift, rows), _mod_spec(k_scale, rows), _mod_spec(m_shift, rows),
                  _mod_spec(m_scale, rows), _const_spec((1, d)), _const_spec((1, d)),
                  _const_spec(w_kv.shape), _const_spec(w_q.shape), _const_spec((1, d)), _const_spec((1, d))],
        out_specs=[xspec] * 5,
        out_shape=[jax.ShapeDtypeStruct((nb, s, d), F32), jax.ShapeDtypeStruct((nb, s, d), F32),
                   jax.ShapeDtypeStruct((nb, s, d), BF16), jax.ShapeDtypeStruct((nb, s, d), BF16),
                   jax.ShapeDtypeStruct((nb, s, d), BF16)],
        compiler_params=_cparams(("arbitrary", "arbitrary")),
        name="kv_q_proj",
    )(x, k_shift, k_scale, m_shift, m_scale, kv_norm_g.reshape(1, d), mix_norm_g.reshape(1, d),
      w_kv, w_q, k_gain, q_gain)


NEG_LOG2E = np.float32(-math.log2(math.e))
SB_NEG_SCALE2 = np.float32(-(HEAD_DIM ** -0.5) * math.log2(math.e))


def _log2_beta(y):
    neg_abs = lax.bitcast_convert_type(lax.bitcast_convert_type(y, jnp.int32) | jnp.int32(-2 ** 31), F32)
    lm = jnp.minimum(y, 0.0) - jnp.log2(1.0 + jnp.exp2(neg_abs))
    return lm - y, lm


def _split_bf16(x):
    hi = x.astype(BF16)
    lo = (x - hi.astype(F32)).astype(BF16)
    return hi, lo


def _sb_prompt_body(q_ref, k_ref, v_ref, bias_ref, tri_ref, o_ref,
                    d_ref, surv_ref, lms_ref, run_ref, acc_ref, *, blk, heads, scale):
    qi = pl.program_id(2)
    hsl = [slice(h * HEAD_DIM, (h + 1) * HEAD_DIM) for h in range(heads)]
    hr = range(heads)

    def logits(kb):
        ks = pl.multiple_of(kb * blk, blk)
        return [lax.dot_general(q_ref[0, :, hsl[h]], k_ref[0, pl.ds(ks, blk), hsl[h]],
                                (((1,), (1,)), ((), ())), preferred_element_type=F32) * scale
                + bias_ref[0, h:h + 1, :] for h in hr]

    def produce(z, mask):
        for h in hr:
            d, lm = _log2_beta(z[h])
            if mask is not None:
                d, lm = jnp.where(mask, d, -jnp.inf), jnp.where(mask, lm, 0.0)
            hi, lo = _split_bf16(lm)
            surv_ref[h] = jnp.dot(jnp.concatenate([hi, lo], axis=1), tri_ref[...],
                                  preferred_element_type=F32)
            d_ref[h] = d
            lms_ref[h] = jnp.sum(lm, axis=1, keepdims=True)

    def consume(kb):
        ks = pl.multiple_of(kb * blk, blk)
        for h in hr:
            a = jnp.exp2(d_ref[h] + surv_ref[h] + run_ref[h])
            acc_ref[h] += jnp.dot(a.astype(BF16), v_ref[0, pl.ds(ks, blk), hsl[h]],
                                  preferred_element_type=F32)
            run_ref[h] += lms_ref[h]

    acc_ref[...] = jnp.zeros_like(acc_ref)
    run_ref[...] = jnp.zeros_like(run_ref)
    row = lax.broadcasted_iota(jnp.int32, (blk, blk), 0)
    col = lax.broadcasted_iota(jnp.int32, (blk, blk), 1)
    produce(logits(qi), col < row)

    def trip(j, carry):
        z = logits(qi - j - 1)
        consume(qi - j)
        produce(z, None)
        return carry

    lax.fori_loop(0, qi, trip, 0)
    consume(0)
    for h in hr:
        o_ref[0, :, hsl[h]] = acc_ref[h].astype(o_ref.dtype)


def _suffix_matrix(n):
    m = (np.arange(n)[:, None] > np.arange(n)[None, :]).astype(np.float32)
    return jnp.asarray(np.concatenate([m, m], axis=0), dtype=BF16)


def _sb_prompt(q, k, v, logit_bias, blk, heads):
    nb, s, d = q.shape
    bias = jnp.broadcast_to((logit_bias.astype(F32) * NEG_LOG2E).reshape(N_HEADS // heads, heads, 1),
                            (N_HEADS // heads, heads, blk))
    qspec = pl.BlockSpec((1, blk, heads * HEAD_DIM), lambda b, h, i: (b, i, h))
    kvspec = pl.BlockSpec((1, s, heads * HEAD_DIM), lambda b, h, i: (b, 0, h))
    return pl.pallas_call(
        functools.partial(_sb_prompt_body, blk=blk, heads=heads, scale=SB_NEG_SCALE2),
        grid=(nb, N_HEADS // heads, s // blk),
        in_specs=[qspec, kvspec, kvspec, pl.BlockSpec((1, heads, blk), lambda b, h, i: (h, 0, 0)),
                  pl.BlockSpec((2 * blk, blk), lambda b, h, i: (0, 0))],
        out_specs=qspec,
        out_shape=jax.ShapeDtypeStruct((nb, s, d), BF16),
        scratch_shapes=[pltpu.VMEM((heads, blk, blk), F32), pltpu.VMEM((heads, blk, blk), F32),
                        pltpu.VMEM((heads, blk, 1), F32), pltpu.VMEM((heads, blk, 1), F32),
                        pltpu.VMEM((heads, blk, HEAD_DIM), F32)],
        compiler_params=_cparams(("arbitrary", "arbitrary", "arbitrary")),
        name="sb_prompt",
    )(q, k, v, bias, _suffix_matrix(blk))


def _sb_paged_body(pt_ref, *refs, pages_per_step, n_q, scale):
    kp = refs[:pages_per_step]
    vp = refs[pages_per_step:2 * pages_per_step]
    wq_ref, bias_ref, knew_ref, vnew_ref, tri_ref, o_ref, acc_ref, run_ref = refs[2 * pages_per_step:]
    step = pl.program_id(1)
    wq = wq_ref[0]
    bias = bias_ref[...]

    def heads_to_lanes(ref):
        n_pos = ref.shape[1] // N_HEADS
        return jnp.concatenate(
            [ref[0, pl.ds(h, n_pos, stride=N_HEADS), :].astype(BF16) for h in range(N_HEADS)], axis=1)

    def block(kblk, vblk, mask):
        d, lm = _log2_beta(jnp.dot(kblk, wq, preferred_element_type=F32) * scale + bias)
        if mask is not None:
            lm = jnp.where(mask, lm, 0.0)
        hi, lo = _split_bf16(lm)
        surv = jnp.dot(tri_ref[...], jnp.concatenate([hi, lo], axis=0),
                       preferred_element_type=F32) + run_ref[...]
        a = jnp.exp2(d + surv)
        if mask is not None:
            a = jnp.where(mask, a, 0.0)
        acc_ref[...] += jnp.dot(a.T.astype(BF16), vblk, preferred_element_type=F32)
        run_ref[...] += jnp.sum(lm, axis=0, keepdims=True)

    @pl.when(step == 0)
    def _():
        acc_ref[...] = jnp.zeros_like(acc_ref)
        run_ref[...] = jnp.zeros_like(run_ref)
        shape = (knew_ref.shape[1], wq.shape[1])
        kpos = lax.broadcasted_iota(jnp.int32, shape, 0)
        qpos = lax.broadcasted_iota(jnp.int32, shape, 1) % n_q
        block(knew_ref[0].astype(BF16), vnew_ref[0].astype(BF16), kpos < qpos)

    ds, survs, tots = [], [], []
    for i in range(pages_per_step):
        d, lm = _log2_beta(jnp.dot(heads_to_lanes(kp[i]), wq, preferred_element_type=F32) * scale + bias)
        hi, lo = _split_bf16(lm)
        ds.append(d)
        survs.append(jnp.dot(tri_ref[...], jnp.concatenate([hi, lo], axis=0), preferred_element_type=F32))
        tots.append(jnp.sum(lm, axis=0, keepdims=True))
    run = run_ref[...]
    weights = []
    for i in range(pages_per_step):
        weights.append(jnp.exp2(ds[i] + survs[i] + run).T.astype(BF16))
        run = run + tots[i]
    run_ref[...] = run
    acc_ref[...] += jnp.dot(jnp.concatenate(weights, axis=1),
                            jnp.concatenate([heads_to_lanes(vp[i]) for i in range(pages_per_step)], axis=0),
                            preferred_element_type=F32)

    @pl.when(step == pl.num_programs(1) - 1)
    def _():
        o_ref[0] = acc_ref[...]


def _suffix_matrix_t(n):
    m = (np.arange(n)[None, :] > np.arange(n)[:, None]).astype(np.float32)
    return jnp.asarray(np.concatenate([m, m], axis=1), dtype=BF16)


def _sb_paged(q, k_new, v_new, cache_k, cache_v, page_table, logit_bias, pages_per_step):
    nb, n_q, d = q.shape
    n_pool, page = cache_k.shape[:2]
    n_pages = page_table.shape[1]
    ck = cache_k.reshape(n_pool, page * N_HEADS, HEAD_DIM)
    cv = cache_v.reshape(n_pool, page * N_HEADS, HEAD_DIM)
    ncol = LANES
    assert N_HEADS * n_q <= ncol and n_q <= page
    qh = q.reshape(nb, n_q, N_HEADS, HEAD_DIM)
    eye = jnp.eye(N_HEADS, dtype=BF16)
    wq = jnp.einsum('bqhd,hg->bhdgq', qh, eye).reshape(nb, d, N_HEADS * n_q)
    wq = jnp.pad(wq, ((0, 0), (0, 0), (0, ncol - N_HEADS * n_q)))
    bias = jnp.pad(jnp.repeat(logit_bias.astype(F32) * NEG_LOG2E, n_q), (0, ncol - N_HEADS * n_q)).reshape(1, ncol)
    knew = jnp.pad(k_new, ((0, 0), (0, page - n_q), (0, 0)))
    vnew = jnp.pad(v_new, ((0, 0), (0, page - n_q), (0, 0)))
    pt = page_table.reshape(-1).astype(jnp.int32)

    def page_map(i):
        return lambda b, s, pt_ref: (pt_ref[b * n_pages + n_pages - 1 - (s * pages_per_step + i)], 0, 0)

    page_specs = [pl.BlockSpec((1, page * N_HEADS, HEAD_DIM), page_map(i)) for i in range(pages_per_step)]
    grid_spec = pltpu.PrefetchScalarGridSpec(
        num_scalar_prefetch=1,
        grid=(nb, n_pages // pages_per_step),
        in_specs=page_specs + page_specs + [
            pl.BlockSpec((1, d, ncol), lambda b, s, pt_ref: (b, 0, 0)),
            pl.BlockSpec((1, ncol), lambda b, s, pt_ref: (0, 0)),
            pl.BlockSpec((1, page, d), lambda b, s, pt_ref: (b, 0, 0)),
            pl.BlockSpec((1, page, d), lambda b, s, pt_ref: (b, 0, 0)),
            pl.BlockSpec((page, 2 * page), lambda b, s, pt_ref: (0, 0))],
        out_specs=pl.BlockSpec((1, ncol, d), lambda b, s, pt_ref: (b, 0, 0)),
        scratch_shapes=[pltpu.VMEM((ncol, d), F32), pltpu.VMEM((1, ncol), F32)])
    return pl.pallas_call(
        functools.partial(_sb_paged_body, pages_per_step=pages_per_step, n_q=n_q,
                          scale=SB_NEG_SCALE2),
        grid_spec=grid_spec,
        out_shape=jax.ShapeDtypeStruct((nb, ncol, d), F32),
        compiler_params=_cparams(("arbitrary", "arbitrary")),
        name="sb_paged",
    )(pt, *([ck] * pages_per_step), *([cv] * pages_per_step), wq, bias, knew, vnew,
      _suffix_matrix_t(page))


def _oproj_body(o_ref, x_ref, gt_ref, w_ref, y_ref):
    mix = jnp.dot(o_ref[0], w_ref[...], preferred_element_type=F32)
    y_ref[0] = x_ref[0] + gt_ref[0] * mix


def _oproj(o, x, gate, w_o, rows):
    nb, s, d = x.shape
    xspec = pl.BlockSpec((1, rows, d), lambda b, i: (b, i, 0))
    return pl.pallas_call(
        _oproj_body,
        grid=(nb, s // rows),
        in_specs=[xspec, xspec, _mod_spec(gate, rows), _const_spec(w_o.shape)],
        out_specs=xspec,
        out_shape=jax.ShapeDtypeStruct((nb, s, d), F32),
        compiler_params=_cparams(("arbitrary", "arbitrary")),
        name="attn_out_proj",
    )(o, x, gate, w_o)


def _mix_weights_prompt(w_s, b_s, width):
    causal = jnp.tril(jnp.ones((CHUNK, CHUNK), dtype=bool))
    wmix = jnp.where(causal[None], w_s, 0.0).astype(BF16)
    bmix = jnp.repeat(b_s.T, width // GROUPS, axis=1)
    return wmix, bmix


def _mix_weights_sample(w_s, b_s, width, n_seq, n_tok):
    causal = jnp.tril(jnp.ones((n_tok, n_tok), dtype=bool))
    small = jnp.where(causal[None], w_s[:, :n_tok, :n_tok], 0.0)
    eye = jnp.eye(n_seq, dtype=F32)
    blockdiag = jnp.einsum('ab,gts->gatbs', eye, small).reshape(GROUPS, n_seq * n_tok, n_seq * n_tok)
    padn = SAMPLE_ROWS - n_seq * n_tok
    wmix = jnp.pad(blockdiag, ((0, 0), (0, padn), (0, padn))).astype(BF16)
    brow = jnp.tile(b_s.T[:n_tok], (n_seq, 1))
    bmix = jnp.repeat(jnp.pad(brow, ((0, padn), (0, 0))), width // GROUPS, axis=1)
    return wmix, bmix


def kernel(x_prompt, x_sample, cache_k, cache_v, page_table, c_prompt, c_sample, mod_w, mod_b, norm_mix_g, norm_ffn_g, a_w_in, a_b_in, a_v_norm_g, a_w_s, a_b_s, a_w_out, kv_mod_w, kv_mod_b, kv_norm_g, kv_w, k_norm_g, b_w_q, b_q_norm_g, b_logit_bias, b_w_o, peer_w_q, peer_subkeys, peer_u, peer_v):
    nbp, seq, d = x_prompt.shape
    nbs, n_tok, _ = x_sample.shape
    n_samp = nbs * n_tok
    width = a_w_out.shape[1]

    c_all = jnp.concatenate([c_prompt, c_sample], axis=0)
    c_all = jnp.pad(c_all, ((0, BF16_ROWS - c_all.shape[0]), (0, 0)))
    mods = _adaln(c_all, mod_w, mod_b)
    kvmod = _adaln(c_all, kv_mod_w[None], kv_mod_b[None])[0]

    def split_mod(m, n):
        parts = jnp.split(m, n, axis=-1)
        prm = [p[:nbp, None, :] for p in parts]
        smp = [jnp.pad(jnp.repeat(p[nbp:nbp + nbs], n_tok, axis=0), ((0, SAMPLE_ROWS - n_samp), (0, 0)))[None]
               for p in parts]
        return prm, smp

    mod_p, mod_s = zip(*[split_mod(mods[l], 6) for l in range(mods.shape[0])])
    kvmod_p, kvmod_s = split_mod(kvmod, 2)

    w_in = a_w_in[0].astype(BF16)
    w_out = a_w_out[0].astype(BF16)
    wq_t = [jnp.transpose(peer_w_q[l]).astype(BF16) for l in range(2)]
    subk = [peer_subkeys[l].astype(BF16) for l in range(2)]
    u_tab = peer_u.astype(BF16)
    eb = 256
    n_l, n_exp, _ = peer_v.shape
    vt_tab = jnp.transpose(peer_v.reshape(n_l, n_exp // eb, eb, d), (0, 1, 3, 2)).astype(BF16)
    w_kv = kv_w.astype(BF16)
    w_q1 = b_w_q[0].astype(BF16)
    w_o1 = b_w_o[0].astype(BF16)
    k_gain = jnp.tile(k_norm_g, N_HEADS).reshape(1, d)
    q_gain = jnp.tile(b_q_norm_g[0], N_HEADS).reshape(1, d)
    wmix_p, bmix_p = _mix_weights_prompt(a_w_s[0], a_b_s[0], width)
    wmix_s, bmix_s = _mix_weights_sample(a_w_s[0], a_b_s[0], width, nbs, n_tok)

    xs = jnp.pad(x_sample.reshape(1, n_samp, d), ((0, 0), (0, SAMPLE_ROWS - n_samp), (0, 0)))

    def trunk(x, mod, kvm, wmix, bmix, rows, tb, n_sub, attend):
        sh_m, sc_m, g_m, sh_f, sc_f, g_f = mod[0]
        x1, v_rows = _mixer_a(x, sh_m, sc_m, g_m, norm_mix_g[0], w_in, a_b_in[0], a_v_norm_g[0],
                              wmix, bmix, w_out, rows)
        x2 = _peer(x1, sh_f, sc_f, g_f, norm_ffn_g[0], wq_t[0], subk[0], u_tab, vt_tab, 0, tb, n_sub, eb)
        sh_m, sc_m, g_m, sh_f, sc_f, g_f = mod[1]
        k, v, kb, vb, q = _kvq(x2, kvm[0], kvm[1], sh_m, sc_m, kv_norm_g, norm_mix_g[1], w_kv, w_q1,
                               k_gain, q_gain, rows)
        o = attend(q, k, v, kb, vb)
        x3 = _oproj(o, x2, g_m, w_o1, rows)
        y = _peer(x3, sh_f, sc_f, g_f, norm_ffn_g[1], wq_t[1], subk[1], u_tab, vt_tab, 1, tb, n_sub, eb)
        return y, k, v, v_rows

    def attend_prompt(q, k, v, kb, vb):
        return _sb_prompt(q, kb, vb, b_logit_bias[0], 256, 8)

    def attend_sample(q, k, v, kb, vb):
        qs = q[0, :n_samp].reshape(nbs, n_tok, d)
        ks = k[0, :n_samp].reshape(nbs, n_tok, d)
        vs = v[0, :n_samp].reshape(nbs, n_tok, d)
        full = _sb_paged(qs, ks, vs, cache_k, cache_v, page_table, b_logit_bias[0], 8)
        full = full[:, :N_HEADS * n_tok].reshape(nbs, N_HEADS, n_tok, N_HEADS, HEAD_DIM)
        idx = jnp.arange(N_HEADS)
        o = full[:, idx, :, idx, :]
        o = jnp.transpose(o, (1, 2, 0, 3)).reshape(1, n_samp, d)
        return jnp.pad(o, ((0, 0), (0, SAMPLE_ROWS - n_samp), (0, 0))).astype(BF16)

    y_p, k_p, v_p, _ = trunk(x_prompt, mod_p, kvmod_p, wmix_p, bmix_p, 256, 512, 2, attend_prompt)
    y_s, k_s, v_s, gv_s = trunk(xs, mod_s, kvmod_s, wmix_s, bmix_s, SAMPLE_ROWS, SAMPLE_ROWS, 1, attend_sample)

    def samp(a, shape):
        return a[0, :n_samp].reshape(shape)

    return (y_p,
            samp(y_s, (nbs, n_tok, d)),
            k_p.reshape(nbp, seq, N_HEADS, HEAD_DIM),
            v_p.reshape(nbp, seq, N_HEADS, HEAD_DIM),
            samp(k_s, (nbs, n_tok, N_HEADS, HEAD_DIM)),
            samp(v_s, (nbs, n_tok, N_HEADS, HEAD_DIM)),
            samp(gv_s, (1, nbs, n_tok, width)))
```

```python
import functools
import math

import jax
import jax.numpy as jnp
import numpy as np
from jax import lax
from jax.experimental import pallas as pl
from jax.experimental.pallas import tpu as pltpu

F32 = jnp.float32
BF16 = jnp.bfloat16
EPS = 1e-6

LANES = 128
SUBLANES = 8
BF16_ROWS = 16
VMEM_LIMIT = 56 * 1024 * 1024

CHUNK = 128
GROUPS = 16
N_HEADS = 16
HEAD_DIM = 128
PEER_HEADS = 8
PEER_NKEYS = 128
PEER_TOPK = 16
SAMPLE_ROWS = 128


def _cparams(sem):
    return pltpu.CompilerParams(dimension_semantics=sem, vmem_limit_bytes=VMEM_LIMIT)


def _const_spec(shape):
    nd = len(shape)
    return pl.BlockSpec(shape, lambda *_: (0,) * nd, pipeline_mode=pl.Buffered(1))


def _rms(x, g):
    ms = jnp.mean(x * x, axis=-1, keepdims=True)
    return x * lax.rsqrt(ms + EPS) * g


def _gelu(x):
    return 0.5 * x * (1.0 + lax.erf(x * np.float32(math.sqrt(0.5))))


def _head_rms(x, g):
    cols = []
    for h in range(x.shape[1] // HEAD_DIM):
        seg = x[:, h * HEAD_DIM:(h + 1) * HEAD_DIM]
        ms = jnp.mean(seg * seg, axis=-1, keepdims=True)
        cols.append(seg * lax.rsqrt(ms + EPS))
    return jnp.concatenate(cols, axis=1) * g


def _adaln_body(c_ref, w_ref, b_ref, o_ref):
    c = c_ref[...]
    s = c / (1.0 + jnp.exp(-c))
    o_ref[0] = jnp.dot(s.astype(BF16), w_ref[0].astype(BF16), preferred_element_type=F32) + b_ref[0]


def _adaln(c, w, b):
    n_l, d, n = w.shape
    m = c.shape[0]
    tn = 1024
    return pl.pallas_call(
        _adaln_body,
        grid=(n_l, n // tn),
        in_specs=[pl.BlockSpec((m, d), lambda l, j: (0, 0)),
                  pl.BlockSpec((1, d, tn), lambda l, j: (l, 0, j)),
                  pl.BlockSpec((1, 1, tn), lambda l, j: (l, 0, j))],
        out_specs=pl.BlockSpec((1, m, tn), lambda l, j: (l, 0, j)),
        out_shape=jax.ShapeDtypeStruct((n_l, m, n), F32),
        compiler_params=_cparams(("arbitrary", "arbitrary")),
        name="adaln",
    )(c, w, b.reshape(n_l, 1, n))


def _mod_spec(mod, rows):
    d = mod.shape[-1]
    if mod.shape[1] == 1:
        return pl.BlockSpec((1, 1, d), lambda b, i: (b, 0, 0))
    return pl.BlockSpec((1, rows, d), lambda b, i: (b, i, 0))


def _mixer_a_body(x_ref, sh_ref, sc_ref, gt_ref, ng_ref, win_ref, bin_ref, vg_ref, wmix_ref,
                  bmix_ref, wout_ref, x1_ref, v_ref, *, chunk):
    x = x_ref[0]
    rows = x.shape[0]
    h = _rms(x, ng_ref[...]) * (1.0 + sc_ref[0]) + sh_ref[0]
    z = jnp.dot(h.astype(BF16), win_ref[...], preferred_element_type=F32) + bin_ref[...]
    z = _gelu(z)
    width = z.shape[1] // 2
    u = z[:, :width]
    v = _rms(z[:, width:], vg_ref[...])
    v_ref[0] = v
    vb = v.astype(BF16)
    gd = width // GROUPS
    blocks = []
    for c in range(rows // chunk):
        cols = [jnp.dot(wmix_ref[g], vb[c * chunk:(c + 1) * chunk, g * gd:(g + 1) * gd],
                        preferred_element_type=F32) for g in range(GROUPS)]
        blocks.append(jnp.concatenate(cols, axis=1) + bmix_ref[...])
    mixed = blocks[0] if len(blocks) == 1 else jnp.concatenate(blocks, axis=0)
    um = (u * mixed).astype(BF16)
    mix = jnp.dot(um, wout_ref[...], preferred_element_type=F32)
    x1_ref[0] = x + gt_ref[0] * mix


def _mixer_a(x, shift, scale, gate, norm_g, w_in, b_in, v_g, wmix, bmix, w_out, rows):
    nb, s, d = x.shape
    width = w_out.shape[0]
    chunk = wmix.shape[1]
    xspec = pl.BlockSpec((1, rows, d), lambda b, i: (b, i, 0))
    return pl.pallas_call(
        functools.partial(_mixer_a_body, chunk=chunk),
        grid=(nb, s // rows),
        in_specs=[xspec, _mod_spec(shift, rows), _mod_spec(scale, rows), _mod_spec(gate, rows),
                  _const_spec((1, d)), _const_spec(w_in.shape), _const_spec((1, 2 * width)),
                  _const_spec((1, width)), _const_spec(wmix.shape), _const_spec(bmix.shape),
                  _const_spec(w_out.shape)],
        out_specs=[xspec, pl.BlockSpec((1, rows, width), lambda b, i: (b, i, 0))],
        out_shape=[jax.ShapeDtypeStruct((nb, s, d), F32), jax.ShapeDtypeStruct((nb, s, width), F32)],
        compiler_params=_cparams(("arbitrary", "arbitrary")),
        name="gmlp_mixer",
    )(x, shift, scale, gate, norm_g.reshape(1, d), w_in, b_in.reshape(1, -1), v_g.reshape(1, -1),
      wmix, bmix, w_out)


def _sort_network(n):
    pairs = []
    p = 1
    while p < n:
        k = p
        while k >= 1:
            for j in range(k % p, n - k, 2 * k):
                for i in range(min(k, n - j - k)):
                    if (i + j) // (2 * p) == (i + j + k) // (2 * p):
                        pairs.append((i + j, i + j + k))
            k //= 2
        p *= 2
    return pairs


_SORT16 = _sort_network(PEER_TOPK)
_BITONIC16 = [(i, i | d) for d in (8, 4, 2, 1) for i in range(PEER_TOPK) if not i & d]


def _apply_network(xs, pairs):
    xs = list(xs)
    for i, j in pairs:
        a, b = xs[i], xs[j]
        xs[i] = jnp.maximum(a, b)
        xs[j] = jnp.minimum(a, b)
    return xs


def _merge_top16(xs, ys):
    zs = [jnp.maximum(xs[r], ys[PEER_TOPK - 1 - r]) for r in range(PEER_TOPK)]
    return _apply_network(zs, _BITONIC16)


def _top16_of_keys(s):
    slabs = [s[SUBLANES * g:SUBLANES * (g + 1), :] for g in range(PEER_NKEYS // SUBLANES)]
    xs = _apply_network(slabs, _SORT16)
    for shift in (4, 2, 1):
        ys = [pltpu.roll(x, shift, axis=0) for x in xs]
        xs = _merge_top16(xs, ys)
    return xs


_CAND = [(r, c) for r in range(PEER_TOPK) for c in range(PEER_TOPK) if (r + 1) * (c + 1) <= PEER_TOPK]


def _route_body(x_ref, sh_ref, sc_ref, ng_ref, wq_ref, sk_ref,
                h2t_ref, r2_ref, e2_ref, c1_ref, e1_ref, s_ref):
    x = x_ref[0]
    h = _rms(x, ng_ref[...]) * (1.0 + sc_ref[0]) + sh_ref[0]
    ht = h.T.astype(BF16)
    h2t_ref[...] = ht
    qt = jnp.dot(wq_ref[...], ht, preferred_element_type=F32).astype(BF16)
    for hp in range(2 * PEER_HEADS):
        s_ref[hp] = jnp.dot(sk_ref[hp % 2], qt[hp * PEER_NKEYS:(hp + 1) * PEER_NKEYS, :],
                            preferred_element_type=F32)
    tb = x.shape[0]
    neg = jnp.full((SUBLANES, LANES), -jnp.inf, F32)
    sub = lax.broadcasted_iota(jnp.int32, (SUBLANES, LANES), 0)

    def chunk(lc, carry):
        lanes = pl.ds(pl.multiple_of(lc * LANES, LANES), LANES)
        top = [[neg] * PEER_TOPK, [neg] * PEER_TOPK]
        for hp in range(2 * PEER_HEADS):
            hh, p = hp // 2, hp % 2
            xs = _top16_of_keys(s_ref[hp, :, lanes])
            top[p] = [jnp.where(sub == hh, xs[r], top[p][r]) for r in range(PEER_TOPK)]
        a, b = top
        cand = {rc: a[rc[0]] + b[rc[1]] for rc in _CAND}
        g1 = [cand[(0, c)] for c in range(PEER_TOPK)]
        rest = [cand[rc] for rc in _CAND if rc[0] > 0]
        rest = rest + [neg] * (3 * PEER_TOPK - len(rest))
        gs = [_apply_network(rest[PEER_TOPK * k:PEER_TOPK * (k + 1)], _SORT16) for k in range(3)]
        m1 = _merge_top16(g1, gs[0])
        m2 = _merge_top16(gs[1], gs[2])
        zs = [jnp.maximum(m1[r], m2[PEER_TOPK - 1 - r]) for r in range(PEER_TOPK)]
        tau = functools.reduce(jnp.minimum, zs)
        top_sum = a[0] + b[0]
        zsum = jnp.zeros((SUBLANES, LANES), F32)
        for rc in _CAND:
            zsum = zsum + jnp.where(cand[rc] >= tau, jnp.exp(cand[rc] - top_sum), 0.0)
        zinv = 1.0 / zsum
        half = PEER_TOPK // 2
        extra0 = jnp.zeros((SUBLANES, LANES), F32)
        for c in range(half, PEER_TOPK):
            extra0 = extra0 + jnp.where(a[0] + b[c] >= tau, 1.0, 0.0)
        for hh in range(PEER_HEADS):
            def row(v):
                return jnp.broadcast_to(v[hh:hh + 1, :], (SUBLANES, LANES))
            bh = [row(b[c]) for c in range(PEER_TOPK)]
            tau_h, zinv_h, a0_h, extra_h = row(tau), row(zinv), row(a[0]), row(extra0)
            for gp in range(PEER_NKEYS // BF16_ROWS):
                r2s, e2s, c1s, e1s = [], [], [], []
                for g in (2 * gp, 2 * gp + 1):
                    rows = slice(SUBLANES * g, SUBLANES * (g + 1))
                    s1 = s_ref[2 * hh, rows, lanes]
                    s2 = s_ref[2 * hh + 1, rows, lanes]
                    r2 = jnp.zeros((SUBLANES, LANES), F32)
                    c1 = jnp.zeros((SUBLANES, LANES), F32)
                    for c in range(PEER_TOPK):
                        r2 = r2 + jnp.where(bh[c] > s2, 1.0, 0.0)
                    for c in range(half):
                        c1 = c1 + jnp.where(s1 + bh[c] >= tau_h, 1.0, 0.0)
                    r2s.append(r2)
                    c1s.append(c1 + jnp.where(s1 == a0_h, extra_h, 0.0))
                    e2s.append(jnp.exp(s2 - bh[0]))
                    e1s.append(jnp.exp(s1 - a0_h) * zinv_h)
                rows16 = slice(BF16_ROWS * gp, BF16_ROWS * (gp + 1))
                r2_ref[hh, rows16, lanes] = jnp.concatenate(r2s, axis=0).astype(BF16)
                e2_ref[hh, rows16, lanes] = jnp.concatenate(e2s, axis=0).astype(BF16)
                key_rows = pl.ds(BF16_ROWS * gp * PEER_HEADS + hh, BF16_ROWS, stride=PEER_HEADS)
                c1_ref[lc, key_rows, :] = jnp.concatenate(c1s, axis=0)
                e1_ref[lc, key_rows, :] = jnp.concatenate(e1s, axis=0)
        return carry

    lax.fori_loop(0, tb // LANES, chunk, 0)


def _peer_route(x, shift, scale, norm_g, wq_t, subkeys, tb):
    nb, s, d = x.shape
    nblk = s // tb
    nt = nb * nblk
    tspec3 = pl.BlockSpec((None, PEER_HEADS, PEER_NKEYS, tb), lambda b, i: (b * nblk + i, 0, 0, 0))
    kh_shape = (tb // LANES, PEER_NKEYS * PEER_HEADS, LANES)
    kspec = pl.BlockSpec((None,) + kh_shape, lambda b, i: (b * nblk + i, 0, 0, 0))
    return pl.pallas_call(
        _route_body,
        grid=(nb, nblk),
        in_specs=[pl.BlockSpec((1, tb, d), lambda b, i: (b, i, 0)), _mod_spec(shift, tb), _mod_spec(scale, tb),
                  _const_spec((1, d)), _const_spec(wq_t.shape), _const_spec(subkeys.shape)],
        out_specs=[pl.BlockSpec((None, d, tb), lambda b, i: (b * nblk + i, 0, 0)), tspec3, tspec3, kspec, kspec],
        out_shape=[jax.ShapeDtypeStruct((nt, d, tb), BF16),
                   jax.ShapeDtypeStruct((nt, PEER_HEADS, PEER_NKEYS, tb), BF16),
                   jax.ShapeDtypeStruct((nt, PEER_HEADS, PEER_NKEYS, tb), BF16),
                   jax.ShapeDtypeStruct((nt,) + kh_shape, F32),
                   jax.ShapeDtypeStruct((nt,) + kh_shape, F32)],
        scratch_shapes=[pltpu.VMEM((2 * PEER_HEADS, PEER_NKEYS, tb), F32)],
        compiler_params=_cparams(("arbitrary", "arbitrary")),
        name="peer_route",
    )(x, shift, scale, norm_g.reshape(1, d), wq_t, subkeys)


def _peer_gate_stage(act_ref, p_ref, r2_ref, e2_ref, c1_ref, e1_ref, key0, n_i, valid=None):
    n_sub, _, tb = act_ref.shape
    for j in range(n_sub):
        for il in range(n_i):
            key_rows = pl.ds(pl.multiple_of((key0 + il) * PEER_HEADS, PEER_HEADS), PEER_HEADS)
            for lc in range(tb // LANES):
                lanes = slice(lc * LANES, (lc + 1) * LANES)
                c1i = c1_ref[j, lc, key_rows, :]
                e1i = e1_ref[j, lc, key_rows, :]
                gate = [jnp.zeros((BF16_ROWS, LANES), BF16)] * (PEER_NKEYS // BF16_ROWS)
                for hh in range(PEER_HEADS):
                    c1b = jnp.broadcast_to(c1i[hh:hh + 1, :], (BF16_ROWS, LANES)).astype(BF16)
                    e1b = jnp.broadcast_to(e1i[hh:hh + 1, :], (BF16_ROWS, LANES)).astype(BF16)
                    for rb in range(PEER_NKEYS // BF16_ROWS):
                        rows = slice(rb * BF16_ROWS, (rb + 1) * BF16_ROWS)
                        r2 = r2_ref[j, hh, rows, lanes]
                        e2 = e2_ref[j, hh, rows, lanes]
                        gate[rb] = gate[rb] + jnp.where(r2 < c1b, e2, jnp.zeros_like(e2)) * e1b
                for rb in range(PEER_NKEYS // BF16_ROWS):
                    rows = slice(il * PEER_NKEYS + rb * BF16_ROWS, il * PEER_NKEYS + (rb + 1) * BF16_ROWS)
                    a = act_ref[j, rows, lanes]
                    p = (_gelu(a) * gate[rb].astype(F32)).astype(BF16)
                    p_ref[j, rows, lanes] = p if valid is None else jnp.where(valid, p, jnp.zeros_like(p))


def _peer_body(h2t_ref, u_ref, vta_ref, vtb_ref, r2_ref, e2_ref, c1_ref, e1_ref, acc_ref,
               acta_ref, actb_ref, pa_ref, pb_ref, *, n_i):
    s = pl.program_id(1)
    last = pl.num_programs(1) - 1
    n_blocks = 2 * last
    n_sub, eb, _ = acta_ref.shape
    routing = (r2_ref, e2_ref, c1_ref, e1_ref)

    @pl.when(s == 0)
    def _():
        acc_ref[...] = jnp.zeros_like(acc_ref)
        actb_ref[...] = jnp.zeros_like(actb_ref)

    for j in range(n_sub):
        acta_ref[j] = jnp.dot(u_ref[:eb, :], h2t_ref[j], preferred_element_type=F32)
    _peer_gate_stage(actb_ref, pb_ref, *routing, jnp.maximum(2 * s - 1, 0) * n_i, n_i)
    for j in range(n_sub):
        acc_ref[j] += jnp.dot(vta_ref[...], pb_ref[j], preferred_element_type=F32)
    for j in range(n_sub):
        actb_ref[j] = jnp.dot(u_ref[eb:, :], h2t_ref[j], preferred_element_type=F32)
    _peer_gate_stage(acta_ref, pa_ref, *routing, jnp.minimum(2 * s, n_blocks - 1) * n_i, n_i, valid=s < last)
    for j in range(n_sub):
        acc_ref[j] += jnp.dot(vtb_ref[...], pa_ref[j], preferred_element_type=F32)


def _peer_experts(h2t, u_tab, vt_tab, layer, r2, e2, c1, e1, n_sub, eb):
    nt, d, tb = h2t.shape
    n_exp = u_tab.shape[1]
    n_i = eb // PEER_NKEYS
    n_pairs = n_exp // (2 * eb)
    assert vt_tab.shape[1:] == (2 * n_pairs, d, eb) and nt % n_sub == 0
    one = pl.Buffered(1)
    t3 = pl.BlockSpec((n_sub, PEER_HEADS, PEER_NKEYS, tb), lambda i, e: (i, 0, 0, 0), pipeline_mode=one)
    k3 = pl.BlockSpec((n_sub, tb // LANES, PEER_NKEYS * PEER_HEADS, LANES), lambda i, e: (i, 0, 0, 0),
                      pipeline_mode=one)
    tile3 = pl.BlockSpec((n_sub, d, tb), lambda i, e: (i, 0, 0))
    return pl.pallas_call(
        functools.partial(_peer_body, n_i=n_i),
        grid=(nt // n_sub, n_pairs + 1),
        in_specs=[tile3,
                  pl.BlockSpec((None, 2 * eb, d), lambda i, e: (layer, jnp.minimum(e, n_pairs - 1), 0)),
                  pl.BlockSpec((None, None, d, eb), lambda i, e: (layer, jnp.maximum(2 * e - 1, 0), 0, 0)),
                  pl.BlockSpec((None, None, d, eb), lambda i, e: (layer, jnp.minimum(2 * e, 2 * n_pairs - 1), 0, 0)),
                  t3, t3, k3, k3],
        out_specs=tile3,
        out_shape=jax.ShapeDtypeStruct((nt, d, tb), F32),
        scratch_shapes=[pltpu.VMEM((n_sub, eb, tb), F32), pltpu.VMEM((n_sub, eb, tb), F32),
                        pltpu.VMEM((n_sub, eb, tb), BF16), pltpu.VMEM((n_sub, eb, tb), BF16)],
        compiler_params=_cparams(("arbitrary", "arbitrary")),
        name="peer_experts",
    )(h2t, u_tab, vt_tab, vt_tab, r2, e2, c1, e1)


def _peer_finish_body(acc_ref, x_ref, g_ref, y_ref):
    y_ref[0] = x_ref[0] + g_ref[0] * acc_ref[0].T


def _peer_finish(acc_t, x, gate):
    nb, s, d = x.shape
    tb = acc_t.shape[2]
    nblk = s // tb
    xspec = pl.BlockSpec((1, tb, d), lambda b, i: (b, i, 0))
    return pl.pallas_call(
        _peer_finish_body,
        grid=(nb, nblk),
        in_specs=[pl.BlockSpec((1, d, tb), lambda b, i: (b * nblk + i, 0, 0)), xspec, _mod_spec(gate, tb)],
        out_specs=xspec,
        out_shape=jax.ShapeDtypeStruct((nb, s, d), F32),
        compiler_params=_cparams(("arbitrary", "arbitrary")),
        name="peer_finish",
    )(acc_t, x, gate)


def _peer(x, shift, scale, gate, norm_g, wq_t, subkeys, u_tab, vt_tab, layer, tb, n_sub, eb):
    h2t, r2, e2, c1, e1 = _peer_route(x, shift, scale, norm_g, wq_t, subkeys, tb)
    acc_t = _peer_experts(h2t, u_tab, vt_tab, layer, r2, e2, c1, e1, n_sub, eb)
    return _peer_finish(acc_t, x, gate)


def _kvq_body(x_ref, ksh_ref, ksc_ref, msh_ref, msc_ref, kng_ref, mng_ref, wkv_ref, wq_ref,
              kg_ref, qg_ref, k_ref, v_ref, kb_ref, vb_ref, q_ref):
    x = x_ref[0]
    d = x.shape[1]
    ms = jnp.mean(x * x, axis=-1, keepdims=True)
    xn = x * lax.rsqrt(ms + EPS)
    hk = (xn * kng_ref[...]) * (1.0 + ksc_ref[0]) + ksh_ref[0]
    kv = jnp.dot(hk.astype(BF16), wkv_ref[...], preferred_element_type=F32)
    k = _head_rms(kv[:, :d], kg_ref[...])
    v = kv[:, d:]
    k_ref[0] = k
    v_ref[0] = v
    kb_ref[0] = k.astype(BF16)
    vb_ref[0] = v.astype(BF16)
    hq = (xn * mng_ref[...]) * (1.0 + msc_ref[0]) + msh_ref[0]
    q = jnp.dot(hq.astype(BF16), wq_ref[...], preferred_element_type=F32)
    q_ref[0] = _head_rms(q, qg_ref[...]).astype(BF16)


def _kvq(x, k_shift, k_scale, m_shift, m_scale, kv_norm_g, mix_norm_g, w_kv, w_q, k_gain, q_gain, rows):
    nb, s, d = x.shape
    xspec = pl.BlockSpec((1, rows, d), lambda b, i: (b, i, 0))
    return pl.pallas_call(
        _kvq_body,
        grid=(nb, s // rows),
        in_specs=[xspec, _mod_spec(k_shift, rows), _mod_spec(k_scale, rows), _mod_spec(m_shift, rows),
                  _mod_spec(m_scale, rows), _const_spec((1, d)), _const_spec((1, d)),
                  _const_spec(w_kv.shape), _const_spec(w_q.shape), _const_spec((1, d)), _const_spec((1, d))],
        out_specs=[xspec] * 5,
        out_shape=[jax.ShapeDtypeStruct((nb, s, d), F32), jax.ShapeDtypeStruct((nb, s, d), F32),
                   jax.ShapeDtypeStruct((nb, s, d), BF16), jax.ShapeDtypeStruct((nb, s, d), BF16),
                   jax.ShapeDtypeStruct((nb, s, d), BF16)],
        compiler_params=_cparams(("arbitrary", "arbitrary")),
        name="kv_q_proj",
    )(x, k_shift, k_scale, m_shift, m_scale, kv_norm_g.reshape(1, d), mix_norm_g.reshape(1, d),
      w_kv, w_q, k_gain, q_gain)


NEG_LOG2E = np.float32(-math.log2(math.e))
SB_NEG_SCALE2 = np.float32(-(HEAD_DIM ** -0.5) * math.log2(math.e))


def _log2_beta(y):
    neg_abs = lax.bitcast_convert_type(lax.bitcast_convert_type(y, jnp.int32) | jnp.int32(-2 ** 31), F32)
    lm = jnp.minimum(y, 0.0) - jnp.log2(1.0 + jnp.exp2(neg_abs))
    return lm - y, lm


def _split_bf16(x):
    hi = x.astype(BF16)
    lo = (x - hi.astype(F32)).astype(BF16)
    return hi, lo


def _sb_prompt_body(q_ref, k_ref, v_ref, bias_ref, tri_ref, o_ref,
                    d_ref, surv_ref, lms_ref, run_ref, acc_ref, *, blk, heads, scale):
    qi = pl.program_id(2)
    hsl = [slice(h * HEAD_DIM, (h + 1) * HEAD_DIM) for h in range(heads)]
    hr = range(heads)

    def logits(kb):
        ks = pl.multiple_of(kb * blk, blk)
        return [lax.dot_general(q_ref[0, :, hsl[h]], k_ref[0, pl.ds(ks, blk), hsl[h]],
                                (((1,), (1,)), ((), ())), preferred_element_type=F32) * scale
                + bias_ref[0, h:h + 1, :] for h in hr]

    def produce(z, mask):
        for h in hr:
            d, lm = _log2_beta(z[h])
            if mask is not None:
                d, lm = jnp.where(mask, d, -jnp.inf), jnp.where(mask, lm, 0.0)
            hi, lo = _split_bf16(lm)
            surv_ref[h] = jnp.dot(jnp.concatenate([hi, lo], axis=1), tri_ref[...],
                                  preferred_element_type=F32)
            d_ref[h] = d
            lms_ref[h] = jnp.sum(lm, axis=1, keepdims=True)

    def consume(kb):
        ks = pl.multiple_of(kb * blk, blk)
        for h in hr:
            a = jnp.exp2(d_ref[h] + surv_ref[h] + run_ref[h])
            acc_ref[h] += jnp.dot(a.astype(BF16), v_ref[0, pl.ds(ks, blk), hsl[h]],
                                  preferred_element_type=F32)
            run_ref[h] += lms_ref[h]

    acc_ref[...] = jnp.zeros_like(acc_ref)
    run_ref[...] = jnp.zeros_like(run_ref)
    row = lax.broadcasted_iota(jnp.int32, (blk, blk), 0)
    col = lax.broadcasted_iota(jnp.int32, (blk, blk), 1)
    produce(logits(qi), col < row)

    def trip(j, carry):
        z = logits(qi - j - 1)
        consume(qi - j)
        produce(z, None)
        return carry

    lax.fori_loop(0, qi, trip, 0)
    consume(0)
    for h in hr:
        o_ref[0, :, hsl[h]] = acc_ref[h].astype(o_ref.dtype)


def _suffix_matrix(n):
    m = (np.arange(n)[:, None] > np.arange(n)[None, :]).astype(np.float32)
    return jnp.asarray(np.concatenate([m, m], axis=0), dtype=BF16)


def _sb_prompt(q, k, v, logit_bias, blk, heads):
    nb, s, d = q.shape
    bias = jnp.broadcast_to((logit_bias.astype(F32) * NEG_LOG2E).reshape(N_HEADS // heads, heads, 1),
                            (N_HEADS // heads, heads, blk))
    qspec = pl.BlockSpec((1, blk, heads * HEAD_DIM), lambda b, h, i: (b, i, h))
    kvspec = pl.BlockSpec((1, s, heads * HEAD_DIM), lambda b, h, i: (b, 0, h))
    return pl.pallas_call(
        functools.partial(_sb_prompt_body, blk=blk, heads=heads, scale=SB_NEG_SCALE2),
        grid=(nb, N_HEADS // heads, s // blk),
        in_specs=[qspec, kvspec, kvspec, pl.BlockSpec((1, heads, blk), lambda b, h, i: (h, 0, 0)),
                  pl.BlockSpec((2 * blk, blk), lambda b, h, i: (0, 0))],
        out_specs=qspec,
        out_shape=jax.ShapeDtypeStruct((nb, s, d), BF16),
        scratch_shapes=[pltpu.VMEM((heads, blk, blk), F32), pltpu.VMEM((heads, blk, blk), F32),
                        pltpu.VMEM((heads, blk, 1), F32), pltpu.VMEM((heads, blk, 1), F32),
                        pltpu.VMEM((heads, blk, HEAD_DIM), F32)],
        compiler_params=_cparams(("arbitrary", "arbitrary", "arbitrary")),
        name="sb_prompt",
    )(q, k, v, bias, _suffix_matrix(blk))


def _sb_paged_body(pt_ref, *refs, pages_per_step, n_q, scale):
    kp = refs[:pages_per_step]
    vp = refs[pages_per_step:2 * pages_per_step]
    wq_ref, bias_ref, knew_ref, vnew_ref, tri_ref, o_ref, acc_ref, run_ref = refs[2 * pages_per_step:]
    step = pl.program_id(1)
    wq = wq_ref[0]
    bias = bias_ref[...]

    def heads_to_lanes(ref):
        n_pos = ref.shape[1] // N_HEADS
        return jnp.concatenate(
            [ref[0, pl.ds(h, n_pos, stride=N_HEADS), :].astype(BF16) for h in range(N_HEADS)], axis=1)

    def block(kblk, vblk, mask):
        d, lm = _log2_beta(jnp.dot(kblk, wq, preferred_element_type=F32) * scale + bias)
        if mask is not None:
            lm = jnp.where(mask, lm, 0.0)
        hi, lo = _split_bf16(lm)
        surv = jnp.dot(tri_ref[...], jnp.concatenate([hi, lo], axis=0),
                       preferred_element_type=F32) + run_ref[...]
        a = jnp.exp2(d + surv)
        if mask is not None:
            a = jnp.where(mask, a, 0.0)
        acc_ref[...] += jnp.dot(a.T.astype(BF16), vblk, preferred_element_type=F32)
        run_ref[...] += jnp.sum(lm, axis=0, keepdims=True)

    @pl.when(step == 0)
    def _():
        acc_ref[...] = jnp.zeros_like(acc_ref)
        run_ref[...] = jnp.zeros_like(run_ref)
        shape = (knew_ref.shape[1], wq.shape[1])
        kpos = lax.broadcasted_iota(jnp.int32, shape, 0)
        qpos = lax.broadcasted_iota(jnp.int32, shape, 1) % n_q
        block(knew_ref[0].astype(BF16), vnew_ref[0].astype(BF16), kpos < qpos)

    ds, survs, tots = [], [], []
    for i in range(pages_per_step):
        d, lm = _log2_beta(jnp.dot(heads_to_lanes(kp[i]), wq, preferred_element_type=F32) * scale + bias)
        hi, lo = _split_bf16(lm)
        ds.append(d)
        survs.append(jnp.dot(tri_ref[...], jnp.concatenate([hi, lo], axis=0), preferred_element_type=F32))
        tots.append(jnp.sum(lm, axis=0, keepdims=True))
    run = run_ref[...]
    weights = []
    for i in range(pages_per_step):
        weights.append(jnp.exp2(ds[i] + survs[i] + run).T.astype(BF16))
        run = run + tots[i]
    run_ref[...] = run
    acc_ref[...] += jnp.dot(jnp.concatenate(weights, axis=1),
                            jnp.concatenate([heads_to_lanes(vp[i]) for i in range(pages_per_step)], axis=0),
                            preferred_element_type=F32)

    @pl.when(step == pl.num_programs(1) - 1)
    def _():
        o_ref[0] = acc_ref[...]


def _suffix_matrix_t(n):
    m = (np.arange(n)[None, :] > np.arange(n)[:, None]).astype(np.float32)
    return jnp.asarray(np.concatenate([m, m], axis=1), dtype=BF16)


def _sb_paged(q, k_new, v_new, cache_k, cache_v, page_table, logit_bias, pages_per_step):
    nb, n_q, d = q.shape
    n_pool, page = cache_k.shape[:2]
    n_pages = page_table.shape[1]
    ck = cache_k.reshape(n_pool, page * N_HEADS, HEAD_DIM)
    cv = cache_v.reshape(n_pool, page * N_HEADS, HEAD_DIM)
    ncol = LANES
    assert N_HEADS * n_q <= ncol and n_q <= page
    qh = q.reshape(nb, n_q, N_HEADS, HEAD_DIM)
    eye = jnp.eye(N_HEADS, dtype=BF16)
    wq = jnp.einsum('bqhd,hg->bhdgq', qh, eye).reshape(nb, d, N_HEADS * n_q)
    wq = jnp.pad(wq, ((0, 0), (0, 0), (0, ncol - N_HEADS * n_q)))
    bias = jnp.pad(jnp.repeat(logit_bias.astype(F32) * NEG_LOG2E, n_q), (0, ncol - N_HEADS * n_q)).reshape(1, ncol)
    knew = jnp.pad(k_new, ((0, 0), (0, page - n_q), (0, 0)))
    vnew = jnp.pad(v_new, ((0, 0), (0, page - n_q), (0, 0)))
    pt = page_table.reshape(-1).astype(jnp.int32)

    def page_map(i):
        return lambda b, s, pt_ref: (pt_ref[b * n_pages + n_pages - 1 - (s * pages_per_step + i)], 0, 0)

    page_specs = [pl.BlockSpec((1, page * N_HEADS, HEAD_DIM), page_map(i)) for i in range(pages_per_step)]
    grid_spec = pltpu.PrefetchScalarGridSpec(
        num_scalar_prefetch=1,
        grid=(nb, n_pages // pages_per_step),
        in_specs=page_specs + page_specs + [
            pl.BlockSpec((1, d, ncol), lambda b, s, pt_ref: (b, 0, 0)),
            pl.BlockSpec((1, ncol), lambda b, s, pt_ref: (0, 0)),
            pl.BlockSpec((1, page, d), lambda b, s, pt_ref: (b, 0, 0)),
            pl.BlockSpec((1, page, d), lambda b, s, pt_ref: (b, 0, 0)),
            pl.BlockSpec((page, 2 * page), lambda b, s, pt_ref: (0, 0))],
        out_specs=pl.BlockSpec((1, ncol, d), lambda b, s, pt_ref: (b, 0, 0)),
        scratch_shapes=[pltpu.VMEM((ncol, d), F32), pltpu.VMEM((1, ncol), F32)])
    return pl.pallas_call(
        functools.partial(_sb_paged_body, pages_per_step=pages_per_step, n_q=n_q,
                          scale=SB_NEG_SCALE2),
        grid_spec=grid_spec,
        out_shape=jax.ShapeDtypeStruct((nb, ncol, d), F32),
        compiler_params=_cparams(("arbitrary", "arbitrary")),
        name="sb_paged",
    )(pt, *([ck] * pages_per_step), *([cv] * pages_per_step), wq, bias, knew, vnew,
      _suffix_matrix_t(page))


def _oproj_body(o_ref, x_ref, gt_ref, w_ref, y_ref):
    mix = jnp.dot(o_ref[0], w_ref[...], preferred_element_type=F32)
    y_ref[0] = x_ref[0] + gt_ref[0] * mix


def _oproj(o, x, gate, w_o, rows):
    nb, s, d = x.shape
    xspec = pl.BlockSpec((1, rows, d), lambda b, i: (b, i, 0))
    return pl.pallas_call(
        _oproj_body,
        grid=(nb, s // rows),
        in_specs=[xspec, xspec, _mod_spec(gate, rows), _const_spec(w_o.shape)],
        out_specs=xspec,
        out_shape=jax.ShapeDtypeStruct((nb, s, d), F32),
        compiler_params=_cparams(("arbitrary", "arbitrary")),
        name="attn_out_proj",
    )(o, x, gate, w_o)


def _mix_weights_prompt(w_s, b_s, width):
    causal = jnp.tril(jnp.ones((CHUNK, CHUNK), dtype=bool))
    wmix = jnp.where(causal[None], w_s, 0.0).astype(BF16)
    bmix = jnp.repeat(b_s.T, width // GROUPS, axis=1)
    return wmix, bmix


def _mix_weights_sample(w_s, b_s, width, n_seq, n_tok):
    causal = jnp.tril(jnp.ones((n_tok, n_tok), dtype=bool))
    small = jnp.where(causal[None], w_s[:, :n_tok, :n_tok], 0.0)
    eye = jnp.eye(n_seq, dtype=F32)
    blockdiag = jnp.einsum('ab,gts->gatbs', eye, small).reshape(GROUPS, n_seq * n_tok, n_seq * n_tok)
    padn = SAMPLE_ROWS - n_seq * n_tok
    wmix = jnp.pad(blockdiag, ((0, 0), (0, padn), (0, padn))).astype(BF16)
    brow = jnp.tile(b_s.T[:n_tok], (n_seq, 1))
    bmix = jnp.repeat(jnp.pad(brow, ((0, padn), (0, 0))), width // GROUPS, axis=1)
    return wmix, bmix


def kernel(x_prompt, x_sample, cache_k, cache_v, page_table, c_prompt, c_sample, mod_w, mod_b, norm_mix_g, norm_ffn_g, a_w_in, a_b_in, a_v_norm_g, a_w_s, a_b_s, a_w_out, kv_mod_w, kv_mod_b, kv_norm_g, kv_w, k_norm_g, b_w_q, b_q_norm_g, b_logit_bias, b_w_o, peer_w_q, peer_subkeys, peer_u, peer_v):
    nbp, seq, d = x_prompt.shape
    nbs, n_tok, _ = x_sample.shape
    n_samp = nbs * n_tok
    width = a_w_out.shape[1]

    c_all = jnp.concatenate([c_prompt, c_sample], axis=0)
    c_all = jnp.pad(c_all, ((0, BF16_ROWS - c_all.shape[0]), (0, 0)))
    mods = _adaln(c_all, mod_w, mod_b)
    kvmod = _adaln(c_all, kv_mod_w[None], kv_mod_b[None])[0]

    def split_mod(m, n):
        parts = jnp.split(m, n, axis=-1)
        prm = [p[:nbp, None, :] for p in parts]
        smp = [jnp.pad(jnp.repeat(p[nbp:nbp + nbs], n_tok, axis=0), ((0, SAMPLE_ROWS - n_samp), (0, 0)))[None]
               for p in parts]
        return prm, smp

    mod_p, mod_s = zip(*[split_mod(mods[l], 6) for l in range(mods.shape[0])])
    kvmod_p, kvmod_s = split_mod(kvmod, 2)

    w_in = a_w_in[0].astype(BF16)
    w_out = a_w_out[0].astype(BF16)
    wq_t = [jnp.transpose(peer_w_q[l]).astype(BF16) for l in range(2)]
    subk = [peer_subkeys[l].astype(BF16) for l in range(2)]
    u_tab = peer_u.astype(BF16)
    eb = 256
    n_l, n_exp, _ = peer_v.shape
    vt_tab = jnp.transpose(peer_v.reshape(n_l, n_exp // eb, eb, d), (0, 1, 3, 2)).astype(BF16)
    w_kv = kv_w.astype(BF16)
    w_q1 = b_w_q[0].astype(BF16)
    w_o1 = b_w_o[0].astype(BF16)
    k_gain = jnp.tile(k_norm_g, N_HEADS).reshape(1, d)
    q_gain = jnp.tile(b_q_norm_g[0], N_HEADS).reshape(1, d)
    wmix_p, bmix_p = _mix_weights_prompt(a_w_s[0], a_b_s[0], width)
    wmix_s, bmix_s = _mix_weights_sample(a_w_s[0], a_b_s[0], width, nbs, n_tok)

    xs = jnp.pad(x_sample.reshape(1, n_samp, d), ((0, 0), (0, SAMPLE_ROWS - n_samp), (0, 0)))

    def trunk(x, mod, kvm, wmix, bmix, rows, tb, n_sub, attend):
        sh_m, sc_m, g_m, sh_f, sc_f, g_f = mod[0]
        x1, v_rows = _mixer_a(x, sh_m, sc_m, g_m, norm_mix_g[0], w_in, a_b_in[0], a_v_norm_g[0],
                              wmix, bmix, w_out, rows)
        x2 = _peer(x1, sh_f, sc_f, g_f, norm_ffn_g[0], wq_t[0], subk[0], u_tab, vt_tab, 0, tb, n_sub, eb)
        sh_m, sc_m, g_m, sh_f, sc_f, g_f = mod[1]
        k, v, kb, vb, q = _kvq(x2, kvm[0], kvm[1], sh_m, sc_m, kv_norm_g, norm_mix_g[1], w_kv, w_q1,
                               k_gain, q_gain, rows)
        o = attend(q, k, v, kb, vb)
        x3 = _oproj(o, x2, g_m, w_o1, rows)
        y = _peer(x3, sh_f, sc_f, g_f, norm_ffn_g[1], wq_t[1], subk[1], u_tab, vt_tab, 1, tb, n_sub, eb)
        return y, k, v, v_rows

    def attend_prompt(q, k, v, kb, vb):
        return _sb_prompt(q, kb, vb, b_logit_bias[0], 256, 8)

    def attend_sample(q, k, v, kb, vb):
        qs = q[0, :n_samp].reshape(nbs, n_tok, d)
        ks = k[0, :n_samp].reshape(nbs, n_tok, d)
        vs = v[0, :n_samp].reshape(nbs, n_tok, d)
        full = _sb_paged(qs, ks, vs, cache_k, cache_v, page_table, b_logit_bias[0], 8)
        full = full[:, :N_HEADS * n_tok].reshape(nbs, N_HEADS, n_tok, N_HEADS, HEAD_DIM)
        idx = jnp.arange(N_HEADS)
        o = full[:, idx, :, idx, :]
        o = jnp.transpose(o, (1, 2, 0, 3)).reshape(1, n_samp, d)
        return jnp.pad(o, ((0, 0), (0, SAMPLE_ROWS - n_samp), (0, 0))).astype(BF16)

    y_p, k_p, v_p, _ = trunk(x_prompt, mod_p, kvmod_p, wmix_p, bmix_p, 256, 512, 2, attend_prompt)
    y_s, k_s, v_s, gv_s = trunk(xs, mod_s, kvmod_s, wmix_s, bmix_s, SAMPLE_ROWS, SAMPLE_ROWS, 1, attend_sample)

    def samp(a, shape):
        return a[0, :n_samp].reshape(shape)

    return (y_p,
            samp(y_s, (nbs, n_tok, d)),
            k_p.reshape(nbp, seq, N_HEADS, HEAD_DIM),
            v_p.reshape(nbp, seq, N_HEADS, HEAD_DIM),
            samp(k_s, (nbs, n_tok, N_HEADS, HEAD_DIM)),
            samp(v_s, (nbs, n_tok, N_HEADS, HEAD_DIM)),
            samp(gv_s, (1, nbs, n_tok, width)))
```

```python
import functools
import math

import jax
import jax.numpy as jnp
import numpy as np
from jax import lax
from jax.experimental import pallas as pl
from jax.experimental.pallas import tpu as pltpu

F32 = jnp.float32
BF16 = jnp.bfloat16
EPS = 1e-6

LANES = 128
SUBLANES = 8
BF16_ROWS = 16
VMEM_LIMIT = 56 * 1024 * 1024

CHUNK = 128
GROUPS = 16
N_HEADS = 16
HEAD_DIM = 128
PEER_HEADS = 8
PEER_NKEYS = 128
PEER_TOPK = 16
SAMPLE_ROWS = 128


def _cparams(sem):
    return pltpu.CompilerParams(dimension_semantics=sem, vmem_limit_bytes=VMEM_LIMIT)


def _const_spec(shape):
    nd = len(shape)
    return pl.BlockSpec(shape, lambda *_: (0,) * nd, pipeline_mode=pl.Buffered(1))


def _rms(x, g):
    ms = jnp.mean(x * x, axis=-1, keepdims=True)
    return x * lax.rsqrt(ms + EPS) * g


def _gelu(x):
    return 0.5 * x * (1.0 + lax.erf(x * np.float32(math.sqrt(0.5))))


def _head_rms(x, g):
    cols = []
    for h in range(x.shape[1] // HEAD_DIM):
        seg = x[:, h * HEAD_DIM:(h + 1) * HEAD_DIM]
        ms = jnp.mean(seg * seg, axis=-1, keepdims=True)
        cols.append(seg * lax.rsqrt(ms + EPS))
    return jnp.concatenate(cols, axis=1) * g


def _adaln_body(c_ref, w_ref, b_ref, o_ref):
    c = c_ref[...]
    s = c / (1.0 + jnp.exp(-c))
    o_ref[0] = jnp.dot(s.astype(BF16), w_ref[0].astype(BF16), preferred_element_type=F32) + b_ref[0]


def _adaln(c, w, b):
    n_l, d, n = w.shape
    m = c.shape[0]
    tn = 1024
    return pl.pallas_call(
        _adaln_body,
        grid=(n_l, n // tn),
        in_specs=[pl.BlockSpec((m, d), lambda l, j: (0, 0)),
                  pl.BlockSpec((1, d, tn), lambda l, j: (l, 0, j)),
                  pl.BlockSpec((1, 1, tn), lambda l, j: (l, 0, j))],
        out_specs=pl.BlockSpec((1, m, tn), lambda l, j: (l, 0, j)),
        out_shape=jax.ShapeDtypeStruct((n_l, m, n), F32),
        compiler_params=_cparams(("arbitrary", "arbitrary")),
        name="adaln",
    )(c, w, b.reshape(n_l, 1, n))


def _mod_spec(mod, rows):
    d = mod.shape[-1]
    if mod.shape[1] == 1:
        return pl.BlockSpec((1, 1, d), lambda b, i: (b, 0, 0))
    return pl.BlockSpec((1, rows, d), lambda b, i: (b, i, 0))


def _mixer_a_body(x_ref, sh_ref, sc_ref, gt_ref, ng_ref, win_ref, bin_ref, vg_ref, wmix_ref,
                  bmix_ref, wout_ref, x1_ref, v_ref, *, chunk):
    x = x_ref[0]
    rows = x.shape[0]
    h = _rms(x, ng_ref[...]) * (1.0 + sc_ref[0]) + sh_ref[0]
    z = jnp.dot(h.astype(BF16), win_ref[...], preferred_element_type=F32) + bin_ref[...]
    z = _gelu(z)
    width = z.shape[1] // 2
    u = z[:, :width]
    v = _rms(z[:, width:], vg_ref[...])
    v_ref[0] = v
    vb = v.astype(BF16)
    gd = width // GROUPS
    blocks = []
    for c in range(rows // chunk):
        cols = [jnp.dot(wmix_ref[g], vb[c * chunk:(c + 1) * chunk, g * gd:(g + 1) * gd],
                        preferred_element_type=F32) for g in range(GROUPS)]
        blocks.append(jnp.concatenate(cols, axis=1) + bmix_ref[...])
    mixed = blocks[0] if len(blocks) == 1 else jnp.concatenate(blocks, axis=0)
    um = (u * mixed).astype(BF16)
    mix = jnp.dot(um, wout_ref[...], preferred_element_type=F32)
    x1_ref[0] = x + gt_ref[0] * mix


def _mixer_a(x, shift, scale, gate, norm_g, w_in, b_in, v_g, wmix, bmix, w_out, rows):
    nb, s, d = x.shape
    width = w_out.shape[0]
    chunk = wmix.shape[1]
    xspec = pl.BlockSpec((1, rows, d), lambda b, i: (b, i, 0))
    return pl.pallas_call(
        functools.partial(_mixer_a_body, chunk=chunk),
        grid=(nb, s // rows),
        in_specs=[xspec, _mod_spec(shift, rows), _mod_spec(scale, rows), _mod_spec(gate, rows),
                  _const_spec((1, d)), _const_spec(w_in.shape), _const_spec((1, 2 * width)),
                  _const_spec((1, width)), _const_spec(wmix.shape), _const_spec(bmix.shape),
                  _const_spec(w_out.shape)],
        out_specs=[xspec, pl.BlockSpec((1, rows, width), lambda b, i: (b, i, 0))],
        out_shape=[jax.ShapeDtypeStruct((nb, s, d), F32), jax.ShapeDtypeStruct((nb, s, width), F32)],
        compiler_params=_cparams(("arbitrary", "arbitrary")),
        name="gmlp_mixer",
    )(x, shift, scale, gate, norm_g.reshape(1, d), w_in, b_in.reshape(1, -1), v_g.reshape(1, -1),
      wmix, bmix, w_out)


def _sort_network(n):
    pairs = []
    p = 1
    while p < n:
        k = p
        while k >= 1:
            for j in range(k % p, n - k, 2 * k):
                for i in range(min(k, n - j - k)):
                    if (i + j) // (2 * p) == (i + j + k) // (2 * p):
                        pairs.append((i + j, i + j + k))
            k //= 2
        p *= 2
    return pairs


_SORT16 = _sort_network(PEER_TOPK)
_BITONIC16 = [(i, i | d) for d in (8, 4, 2, 1) for i in range(PEER_TOPK) if not i & d]


def _apply_network(xs, pairs):
    xs = list(xs)
    for i, j in pairs:
        a, b = xs[i], xs[j]
        xs[i] = jnp.maximum(a, b)
        xs[j] = jnp.minimum(a, b)
    return xs


def _merge_top16(xs, ys):
    zs = [jnp.maximum(xs[r], ys[PEER_TOPK - 1 - r]) for r in range(PEER_TOPK)]
    return _apply_network(zs, _BITONIC16)


def _top16_of_keys(s):
    slabs = [s[SUBLANES * g:SUBLANES * (g + 1), :] for g in range(PEER_NKEYS // SUBLANES)]
    xs = _apply_network(slabs, _SORT16)
    for shift in (4, 2, 1):
        ys = [pltpu.roll(x, shift, axis=0) for x in xs]
        xs = _merge_top16(xs, ys)
    return xs


_CAND = [(r, c) for r in range(PEER_TOPK) for c in range(PEER_TOPK) if (r + 1) * (c + 1) <= PEER_TOPK]


def _route_body(x_ref, sh_ref, sc_ref, ng_ref, wq_ref, sk_ref,
                h2t_ref, r2_ref, e2_ref, c1_ref, e1_ref, s_ref):
    x = x_ref[0]
    h = _rms(x, ng_ref[...]) * (1.0 + sc_ref[0]) + sh_ref[0]
    ht = h.T.astype(BF16)
    h2t_ref[...] = ht
    qt = jnp.dot(wq_ref[...], ht, preferred_element_type=F32).astype(BF16)
    for hp in range(2 * PEER_HEADS):
        s_ref[hp] = jnp.dot(sk_ref[hp % 2], qt[hp * PEER_NKEYS:(hp + 1) * PEER_NKEYS, :],
                            preferred_element_type=F32)
    tb = x.shape[0]
    neg = jnp.full((SUBLANES, LANES), -jnp.inf, F32)
    sub = lax.broadcasted_iota(jnp.int32, (SUBLANES, LANES), 0)

    def chunk(lc, carry):
        lanes = pl.ds(pl.multiple_of(lc * LANES, LANES), LANES)
        top = [[neg] * PEER_TOPK, [neg] * PEER_TOPK]
        for hp in range(2 * PEER_HEADS):
            hh, p = hp // 2, hp % 2
            xs = _top16_of_keys(s_ref[hp, :, lanes])
            top[p] = [jnp.where(sub == hh, xs[r], top[p][r]) for r in range(PEER_TOPK)]
        a, b = top
        cand = {rc: a[rc[0]] + b[rc[1]] for rc in _CAND}
        g1 = [cand[(0, c)] for c in range(PEER_TOPK)]
        rest = [cand[rc] for rc in _CAND if rc[0] > 0]
        rest = rest + [neg] * (3 * PEER_TOPK - len(rest))
        gs = [_apply_network(rest[PEER_TOPK * k:PEER_TOPK * (k + 1)], _SORT16) for k in range(3)]
        m1 = _merge_top16(g1, gs[0])
        m2 = _merge_top16(gs[1], gs[2])
        zs = [jnp.maximum(m1[r], m2[PEER_TOPK - 1 - r]) for r in range(PEER_TOPK)]
        tau = functools.reduce(jnp.minimum, zs)
        top_sum = a[0] + b[0]
        zsum = jnp.zeros((SUBLANES, LANES), F32)
        for rc in _CAND:
            zsum = zsum + jnp.where(cand[rc] >= tau, jnp.exp(cand[rc] - top_sum), 0.0)
        zinv = 1.0 / zsum
        half = PEER_TOPK // 2
        extra0 = jnp.zeros((SUBLANES, LANES), F32)
        for c in range(half, PEER_TOPK):
            extra0 = extra0 + jnp.where(a[0] + b[c] >= tau, 1.0, 0.0)
        for hh in range(PEER_HEADS):
            def row(v):
                return jnp.broadcast_to(v[hh:hh + 1, :], (SUBLANES, LANES))
            bh = [row(b[c]) for c in range(PEER_TOPK)]
            tau_h, zinv_h, a0_h, extra_h = row(tau), row(zinv), row(a[0]), row(extra0)
            for gp in range(PEER_NKEYS // BF16_ROWS):
                r2s, e2s, c1s, e1s = [], [], [], []
                for g in (2 * gp, 2 * gp + 1):
                    rows = slice(SUBLANES * g, SUBLANES * (g + 1))
                    s1 = s_ref[2 * hh, rows, lanes]
                    s2 = s_ref[2 * hh + 1, rows, lanes]
                    r2 = jnp.zeros((SUBLANES, LANES), F32)
                    c1 = jnp.zeros((SUBLANES, LANES), F32)
                    for c in range(PEER_TOPK):
                        r2 = r2 + jnp.where(bh[c] > s2, 1.0, 0.0)
                    for c in range(half):
                        c1 = c1 + jnp.where(s1 + bh[c] >= tau_h, 1.0, 0.0)
                    r2s.append(r2)
                    c1s.append(c1 + jnp.where(s1 == a0_h, extra_h, 0.0))
                    e2s.append(jnp.exp(s2 - bh[0]))
                    e1s.append(jnp.exp(s1 - a0_h) * zinv_h)
                rows16 = slice(BF16_ROWS * gp, BF16_ROWS * (gp + 1))
                r2_ref[hh, rows16, lanes] = jnp.concatenate(r2s, axis=0).astype(BF16)
                e2_ref[hh, rows16, lanes] = jnp.concatenate(e2s, axis=0).astype(BF16)
                key_rows = pl.ds(BF16_ROWS * gp * PEER_HEADS + hh, BF16_ROWS, stride=PEER_HEADS)
                c1_ref[lc, key_rows, :] = jnp.concatenate(c1s, axis=0)
                e1_ref[lc, key_rows, :] = jnp.concatenate(e1s, axis=0)
        return carry

    lax.fori_loop(0, tb // LANES, chunk, 0)


def _peer_route(x, shift, scale, norm_g, wq_t, subkeys, tb):
    nb, s, d = x.shape
    nblk = s // tb
    nt = nb * nblk
    tspec3 = pl.BlockSpec((None, PEER_HEADS, PEER_NKEYS, tb), lambda b, i: (b * nblk + i, 0, 0, 0))
    kh_shape = (tb // LANES, PEER_NKEYS * PEER_HEADS, LANES)
    kspec = pl.BlockSpec((None,) + kh_shape, lambda b, i: (b * nblk + i, 0, 0, 0))
    return pl.pallas_call(
        _route_body,
        grid=(nb, nblk),
        in_specs=[pl.BlockSpec((1, tb, d), lambda b, i: (b, i, 0)), _mod_spec(shift, tb), _mod_spec(scale, tb),
                  _const_spec((1, d)), _const_spec(wq_t.shape), _const_spec(subkeys.shape)],
        out_specs=[pl.BlockSpec((None, d, tb), lambda b, i: (b * nblk + i, 0, 0)), tspec3, tspec3, kspec, kspec],
        out_shape=[jax.ShapeDtypeStruct((nt, d, tb), BF16),
                   jax.ShapeDtypeStruct((nt, PEER_HEADS, PEER_NKEYS, tb), BF16),
                   jax.ShapeDtypeStruct((nt, PEER_HEADS, PEER_NKEYS, tb), BF16),
                   jax.ShapeDtypeStruct((nt,) + kh_shape, F32),
                   jax.ShapeDtypeStruct((nt,) + kh_shape, F32)],
        scratch_shapes=[pltpu.VMEM((2 * PEER_HEADS, PEER_NKEYS, tb), F32)],
        compiler_params=_cparams(("arbitrary", "arbitrary")),
        name="peer_route",
    )(x, shift, scale, norm_g.reshape(1, d), wq_t, subkeys)


def _peer_gate_stage(act_ref, p_ref, r2_ref, e2_ref, c1_ref, e1_ref, key0, n_i, valid=None):
    n_sub, _, tb = act_ref.shape
    for j in range(n_sub):
        for il in range(n_i):
            key_rows = pl.ds(pl.multiple_of((key0 + il) * PEER_HEADS, PEER_HEADS), PEER_HEADS)
            for lc in range(tb // LANES):
                lanes = slice(lc * LANES, (lc + 1) * LANES)
                c1i = c1_ref[j, lc, key_rows, :]
                e1i = e1_ref[j, lc, key_rows, :]
                gate = [jnp.zeros((BF16_ROWS, LANES), BF16)] * (PEER_NKEYS // BF16_ROWS)
                for hh in range(PEER_HEADS):
                    c1b = jnp.broadcast_to(c1i[hh:hh + 1, :], (BF16_ROWS, LANES)).astype(BF16)
                    e1b = jnp.broadcast_to(e1i[hh:hh + 1, :], (BF16_ROWS, LANES)).astype(BF16)
                    for rb in range(PEER_NKEYS // BF16_ROWS):
                        rows = slice(rb * BF16_ROWS, (rb + 1) * BF16_ROWS)
                        r2 = r2_ref[j, hh, rows, lanes]
                        e2 = e2_ref[j, hh, rows, lanes]
                        gate[rb] = gate[rb] + jnp.where(r2 < c1b, e2, jnp.zeros_like(e2)) * e1b
                for rb in range(PEER_NKEYS // BF16_ROWS):
                    rows = slice(il * PEER_NKEYS + rb * BF16_ROWS, il * PEER_NKEYS + (rb + 1) * BF16_ROWS)
                    a = act_ref[j, rows, lanes]
                    p = (_gelu(a) * gate[rb].astype(F32)).astype(BF16)
                    p_ref[j, rows, lanes] = p if valid is None else jnp.where(valid, p, jnp.zeros_like(p))


def _peer_body(h2t_ref, u_ref, vta_ref, vtb_ref, r2_ref, e2_ref, c1_ref, e1_ref, acc_ref,
               acta_ref, actb_ref, pa_ref, pb_ref, *, n_i):
    s = pl.program_id(1)
    last = pl.num_programs(1) - 1
    n_blocks = 2 * last
    n_sub, eb, _ = acta_ref.shape
    routing = (r2_ref, e2_ref, c1_ref, e1_ref)

    @pl.when(s == 0)
    def _():
        acc_ref[...] = jnp.zeros_like(acc_ref)
        actb_ref[...] = jnp.zeros_like(actb_ref)

    for j in range(n_sub):
        acta_ref[j] = jnp.dot(u_ref[:eb, :], h2t_ref[j], preferred_element_type=F32)
    _peer_gate_stage(actb_ref, pb_ref, *routing, jnp.maximum(2 * s - 1, 0) * n_i, n_i)
    for j in range(n_sub):
        acc_ref[j] += jnp.dot(vta_ref[...], pb_ref[j], preferred_element_type=F32)
    for j in range(n_sub):
        actb_ref[j] = jnp.dot(u_ref[eb:, :], h2t_ref[j], preferred_element_type=F32)
    _peer_gate_stage(acta_ref, pa_ref, *routing, jnp.minimum(2 * s, n_blocks - 1) * n_i, n_i, valid=s < last)
    for j in range(n_sub):
        acc_ref[j] += jnp.dot(vtb_ref[...], pa_ref[j], preferred_element_type=F32)


def _peer_experts(h2t, u_tab, vt_tab, layer, r2, e2, c1, e1, n_sub, eb):
    nt, d, tb = h2t.shape
    n_exp = u_tab.shape[1]
    n_i = eb // PEER_NKEYS
    n_pairs = n_exp // (2 * eb)
    assert vt_tab.shape[1:] == (2 * n_pairs, d, eb) and nt % n_sub == 0
    one = pl.Buffered(1)
    t3 = pl.BlockSpec((n_sub, PEER_HEADS, PEER_NKEYS, tb), lambda i, e: (i, 0, 0, 0), pipeline_mode=one)
    k3 = pl.BlockSpec((n_sub, tb // LANES, PEER_NKEYS * PEER_HEADS, LANES), lambda i, e: (i, 0, 0, 0),
                      pipeline_mode=one)
    tile3 = pl.BlockSpec((n_sub, d, tb), lambda i, e: (i, 0, 0))
    return pl.pallas_call(
        functools.partial(_peer_body, n_i=n_i),
        grid=(nt // n_sub, n_pairs + 1),
        in_specs=[tile3,
                  pl.BlockSpec((None, 2 * eb, d), lambda i, e: (layer, jnp.minimum(e, n_pairs - 1), 0)),
                  pl.BlockSpec((None, None, d, eb), lambda i, e: (layer, jnp.maximum(2 * e - 1, 0), 0, 0)),
                  pl.BlockSpec((None, None, d, eb), lambda i, e: (layer, jnp.minimum(2 * e, 2 * n_pairs - 1), 0, 0)),
                  t3, t3, k3, k3],
        out_specs=tile3,
        out_shape=jax.ShapeDtypeStruct((nt, d, tb), F32),
        scratch_shapes=[pltpu.VMEM((n_sub, eb, tb), F32), pltpu.VMEM((n_sub, eb, tb), F32),
                        pltpu.VMEM((n_sub, eb, tb), BF16), pltpu.VMEM((n_sub, eb, tb), BF16)],
        compiler_params=_cparams(("arbitrary", "arbitrary")),
        name="peer_experts",
    )(h2t, u_tab, vt_tab, vt_tab, r2, e2, c1, e1)


def _peer_finish_body(acc_ref, x_ref, g_ref, y_ref):
    y_ref[0] = x_ref[0] + g_ref[0] * acc_ref[0].T


def _peer_finish(acc_t, x, gate):
    nb, s, d = x.shape
    tb = acc_t.shape[2]
    nblk = s // tb
    xspec = pl.BlockSpec((1, tb, d), lambda b, i: (b, i, 0))
    return pl.pallas_call(
        _peer_finish_body,
        grid=(nb, nblk),
        in_specs=[pl.BlockSpec((1, d, tb), lambda b, i: (b * nblk + i, 0, 0)), xspec, _mod_spec(gate, tb)],
        out_specs=xspec,
        out_shape=jax.ShapeDtypeStruct((nb, s, d), F32),
        compiler_params=_cparams(("arbitrary", "arbitrary")),
        name="peer_finish",
    )(acc_t, x, gate)


def _peer(x, shift, scale, gate, norm_g, wq_t, subkeys, u_tab, vt_tab, layer, tb, n_sub, eb):
    h2t, r2, e2, c1, e1 = _peer_route(x, shift, scale, norm_g, wq_t, subkeys, tb)
    acc_t = _peer_experts(h2t, u_tab, vt_tab, layer, r2, e2, c1, e1, n_sub, eb)
    return _peer_finish(acc_t, x, gate)


def _kvq_body(x_ref, ksh_ref, ksc_ref, msh_ref, msc_ref, kng_ref, mng_ref, wkv_ref, wq_ref,
              kg_ref, qg_ref, k_ref, v_ref, kb_ref, vb_ref, q_ref):
    x = x_ref[0]
    d = x.shape[1]
    ms = jnp.mean(x * x, axis=-1, keepdims=True)
    xn = x * lax.rsqrt(ms + EPS)
    hk = (xn * kng_ref[...]) * (1.0 + ksc_ref[0]) + ksh_ref[0]
    kv = jnp.dot(hk.astype(BF16), wkv_ref[...], preferred_element_type=F32)
    k = _head_rms(kv[:, :d], kg_ref[...])
    v = kv[:, d:]
    k_ref[0] = k
    v_ref[0] = v
    kb_ref[0] = k.astype(BF16)
    vb_ref[0] = v.astype(BF16)
    hq = (xn * mng_ref[...]) * (1.0 + msc_ref[0]) + msh_ref[0]
    q = jnp.dot(hq.astype(BF16), wq_ref[...], preferred_element_type=F32)
    q_ref[0] = _head_rms(q, qg_ref[...]).astype(BF16)


def _kvq(x, k_shift, k_scale, m_shift, m_scale, kv_norm_g, mix_norm_g, w_kv, w_q, k_gain, q_gain, rows):
    nb, s, d = x.shape
    xspec = pl.BlockSpec((1, rows, d), lambda b, i: (b, i, 0))
    return pl.pallas_call(
        _kvq_body,
        grid=(nb, s // rows),
        in_specs=[xspec, _mod_spec(k_shift, rows), _mod_spec(k_scale, rows), _mod_spec(m_shift, rows),
                  _mod_spec(m_scale, rows), _const_spec((1, d)), _const_spec((1, d)),
                  _const_spec(w_kv.shape), _const_spec(w_q.shape), _const_spec((1, d)), _const_spec((1, d))],
        out_specs=[xspec] * 5,
        out_shape=[jax.ShapeDtypeStruct((nb, s, d), F32), jax.ShapeDtypeStruct((nb, s, d), F32),
                   jax.ShapeDtypeStruct((nb, s, d), BF16), jax.ShapeDtypeStruct((nb, s, d), BF16),
                   jax.ShapeDtypeStruct((nb, s, d), BF16)],
        compiler_params=_cparams(("arbitrary", "arbitrary")),
        name="kv_q_proj",
    )(x, k_shift, k_scale, m_shift, m_scale, kv_norm_g.reshape(1, d), mix_norm_g.reshape(1, d),
      w_kv, w_q, k_gain, q_gain)


NEG_LOG2E = np.float32(-math.log2(math.e))
SB_NEG_SCALE2 = np.float32(-(HEAD_DIM ** -0.5) * math.log2(math.e))


def _log2_beta(y):
    neg_abs = lax.bitcast_convert_type(lax.bitcast_convert_type(y, jnp.int32) | jnp.int32(-2 ** 31), F32)
    lm = jnp.minimum(y, 0.0) - jnp.log2(1.0 + jnp.exp2(neg_abs))
    return lm - y, lm


def _split_bf16(x):
    hi = x.astype(BF16)
    lo = (x - hi.astype(F32)).astype(BF16)
    return hi, lo


def _sb_prompt_body(q_ref, k_ref, v_ref, bias_ref, tri_ref, o_ref,
                    d_ref, surv_ref, lms_ref, run_ref, acc_ref, *, blk, heads, scale):
    qi = pl.program_id(2)
    hsl = [slice(h * HEAD_DIM, (h + 1) * HEAD_DIM) for h in range(heads)]
    hr = range(heads)

    def logits(kb):
        ks = pl.multiple_of(kb * blk, blk)
        return [lax.dot_general(q_ref[0, :, hsl[h]], k_ref[0, pl.ds(ks, blk), hsl[h]],
                                (((1,), (1,)), ((), ())), preferred_element_type=F32) * scale
                + bias_ref[0, h:h + 1, :] for h in hr]

    def produce(z, mask):
        for h in hr:
            d, lm = _log2_beta(z[h])
            if mask is not None:
                d, lm = jnp.where(mask, d, -jnp.inf), jnp.where(mask, lm, 0.0)
            hi, lo = _split_bf16(lm)
            surv_ref[h] = jnp.dot(jnp.concatenate([hi, lo], axis=1), tri_ref[...],
                                  preferred_element_type=F32)
            d_ref[h] = d
            lms_ref[h] = jnp.sum(lm, axis=1, keepdims=True)

    def consume(kb):
        ks = pl.multiple_of(kb * blk, blk)
        for h in hr:
            a = jnp.exp2(d_ref[h] + surv_ref[h] + run_ref[h])
            acc_ref[h] += jnp.dot(a.astype(BF16), v_ref[0, pl.ds(ks, blk), hsl[h]],
                                  preferred_element_type=F32)
            run_ref[h] += lms_ref[h]

    acc_ref[...] = jnp.zeros_like(acc_ref)
    run_ref[...] = jnp.zeros_like(run_ref)
    row = lax.broadcasted_iota(jnp.int32, (blk, blk), 0)
    col = lax.broadcasted_iota(jnp.int32, (blk, blk), 1)
    produce(logits(qi), col < row)

    def trip(j, carry):
        z = logits(qi - j - 1)
        consume(qi - j)
        produce(z, None)
        return carry

    lax.fori_loop(0, qi, trip, 0)
    consume(0)
    for h in hr:
        o_ref[0, :, hsl[h]] = acc_ref[h].astype(o_ref.dtype)


def _suffix_matrix(n):
    m = (np.arange(n)[:, None] > np.arange(n)[None, :]).astype(np.float32)
    return jnp.asarray(np.concatenate([m, m], axis=0), dtype=BF16)


def _sb_prompt(q, k, v, logit_bias, blk, heads):
    nb, s, d = q.shape
    bias = jnp.broadcast_to((logit_bias.astype(F32) * NEG_LOG2E).reshape(N_HEADS // heads, heads, 1),
                            (N_HEADS // heads, heads, blk))
    qspec = pl.BlockSpec((1, blk, heads * HEAD_DIM), lambda b, h, i: (b, i, h))
    kvspec = pl.BlockSpec((1, s, heads * HEAD_DIM), lambda b, h, i: (b, 0, h))
    return pl.pallas_call(
        functools.partial(_sb_prompt_body, blk=blk, heads=heads, scale=SB_NEG_SCALE2),
        grid=(nb, N_HEADS // heads, s // blk),
        in_specs=[qspec, kvspec, kvspec, pl.BlockSpec((1, heads, blk), lambda b, h, i: (h, 0, 0)),
                  pl.BlockSpec((2 * blk, blk), lambda b, h, i: (0, 0))],
        out_specs=qspec,
        out_shape=jax.ShapeDtypeStruct((nb, s, d), BF16),
        scratch_shapes=[pltpu.VMEM((heads, blk, blk), F32), pltpu.VMEM((heads, blk, blk), F32),
                        pltpu.VMEM((heads, blk, 1), F32), pltpu.VMEM((heads, blk, 1), F32),
                        pltpu.VMEM((heads, blk, HEAD_DIM), F32)],
        compiler_params=_cparams(("arbitrary", "arbitrary", "arbitrary")),
        name="sb_prompt",
    )(q, k, v, bias, _suffix_matrix(blk))


def _sb_paged_body(pt_ref, *refs, pages_per_step, n_q, scale):
    kp = refs[:pages_per_step]
    vp = refs[pages_per_step:2 * pages_per_step]
    wq_ref, bias_ref, knew_ref, vnew_ref, tri_ref, o_ref, acc_ref, run_ref = refs[2 * pages_per_step:]
    step = pl.program_id(1)
    wq = wq_ref[0]
    bias = bias_ref[...]
    nt_dims = (((1,), (1,)), ((), ()))

    def heads_to_lanes(ref):
        n_pos = ref.shape[1] // N_HEADS
        return jnp.concatenate(
            [ref[0, pl.ds(h, n_pos, stride=N_HEADS), :].astype(BF16) for h in range(N_HEADS)], axis=1)

    def block(kblk, vblk, mask):
        d, lm = _log2_beta(lax.dot_general(wq, kblk, nt_dims, preferred_element_type=F32) * scale + bias)
        if mask is not None:
            lm = jnp.where(mask, lm, 0.0)
        hi, lo = _split_bf16(lm)
        surv = jnp.dot(jnp.concatenate([hi, lo], axis=1), tri_ref[...],
                       preferred_element_type=F32) + run_ref[...]
        a = jnp.exp2(d + surv)
        if mask is not None:
            a = jnp.where(mask, a, 0.0)
        acc_ref[...] += jnp.dot(a.astype(BF16), vblk, preferred_element_type=F32)
        run_ref[...] += jnp.sum(lm, axis=1, keepdims=True)

    @pl.when(step == 0)
    def _():
        acc_ref[...] = jnp.zeros_like(acc_ref)
        run_ref[...] = jnp.zeros_like(run_ref)
        shape = (wq.shape[0], knew_ref.shape[1])
        kpos = lax.broadcasted_iota(jnp.int32, shape, 1)
        qpos = lax.broadcasted_iota(jnp.int32, shape, 0) % n_q
        block(knew_ref[0].astype(BF16), vnew_ref[0].astype(BF16), kpos < qpos)

    ds, survs, tots = [], [], []
    for i in range(pages_per_step):
        d, lm = _log2_beta(lax.dot_general(wq, heads_to_lanes(kp[i]), nt_dims,
                                           preferred_element_type=F32) * scale + bias)
        hi, lo = _split_bf16(lm)
        ds.append(d)
        survs.append(jnp.dot(jnp.concatenate([hi, lo], axis=1), tri_ref[...], preferred_element_type=F32))
        tots.append(jnp.sum(lm, axis=1, keepdims=True))
    run = run_ref[...]
    weights = []
    for i in range(pages_per_step):
        weights.append(jnp.exp2(ds[i] + survs[i] + run).astype(BF16))
        run = run + tots[i]
    run_ref[...] = run
    acc_ref[...] += jnp.dot(jnp.concatenate(weights, axis=1),
                            jnp.concatenate([heads_to_lanes(vp[i]) for i in range(pages_per_step)], axis=0),
                            preferred_element_type=F32)

    @pl.when(step == pl.num_programs(1) - 1)
    def _():
        o_ref[0] = acc_ref[...]


def _suffix_matrix_t(n):
    m = (np.arange(n)[None, :] > np.arange(n)[:, None]).astype(np.float32)
    return jnp.asarray(np.concatenate([m, m], axis=1), dtype=BF16)


def _sb_paged(q, k_new, v_new, cache_k, cache_v, page_table, logit_bias, pages_per_step):
    nb, n_q, d = q.shape
    n_pool, page = cache_k.shape[:2]
    n_pages = page_table.shape[1]
    ck = cache_k.reshape(n_pool, page * N_HEADS, HEAD_DIM)
    cv = cache_v.reshape(n_pool, page * N_HEADS, HEAD_DIM)
    ncol = LANES
    assert N_HEADS * n_q <= ncol and n_q <= page
    qh = q.reshape(nb, n_q, N_HEADS, HEAD_DIM)
    eye = jnp.eye(N_HEADS, dtype=BF16)
    wq = jnp.einsum('bqhd,hg->bgqhd', qh, eye).reshape(nb, N_HEADS * n_q, d)
    wq = jnp.pad(wq, ((0, 0), (0, ncol - N_HEADS * n_q), (0, 0)))
    bias = jnp.pad(jnp.repeat(logit_bias.astype(F32) * NEG_LOG2E, n_q), (0, ncol - N_HEADS * n_q)).reshape(ncol, 1)
    knew = jnp.pad(k_new, ((0, 0), (0, page - n_q), (0, 0)))
    vnew = jnp.pad(v_new, ((0, 0), (0, page - n_q), (0, 0)))
    pt = page_table.reshape(-1).astype(jnp.int32)

    def page_map(i):
        return lambda b, s, pt_ref: (pt_ref[b * n_pages + n_pages - 1 - (s * pages_per_step + i)], 0, 0)

    page_specs = [pl.BlockSpec((1, page * N_HEADS, HEAD_DIM), page_map(i)) for i in range(pages_per_step)]
    grid_spec = pltpu.PrefetchScalarGridSpec(
        num_scalar_prefetch=1,
        grid=(nb, n_pages // pages_per_step),
        in_specs=page_specs + page_specs + [
            pl.BlockSpec((1, ncol, d), lambda b, s, pt_ref: (b, 0, 0)),
            pl.BlockSpec((ncol, 1), lambda b, s, pt_ref: (0, 0)),
            pl.BlockSpec((1, page, d), lambda b, s, pt_ref: (b, 0, 0)),
            pl.BlockSpec((1, page, d), lambda b, s, pt_ref: (b, 0, 0)),
            pl.BlockSpec((2 * page, page), lambda b, s, pt_ref: (0, 0))],
        out_specs=pl.BlockSpec((1, ncol, d), lambda b, s, pt_ref: (b, 0, 0)),
        scratch_shapes=[pltpu.VMEM((ncol, d), F32), pltpu.VMEM((ncol, 1), F32)])
    return pl.pallas_call(
        functools.partial(_sb_paged_body, pages_per_step=pages_per_step, n_q=n_q,
                          scale=SB_NEG_SCALE2),
        grid_spec=grid_spec,
        out_shape=jax.ShapeDtypeStruct((nb, ncol, d), F32),
        compiler_params=_cparams(("arbitrary", "arbitrary")),
        name="sb_paged",
    )(pt, *([ck] * pages_per_step), *([cv] * pages_per_step), wq, bias, knew, vnew,
      _suffix_matrix(page))


def _oproj_body(o_ref, x_ref, gt_ref, w_ref, y_ref):
    mix = jnp.dot(o_ref[0], w_ref[...], preferred_element_type=F32)
    y_ref[0] = x_ref[0] + gt_ref[0] * mix


def _oproj(o, x, gate, w_o, rows):
    nb, s, d = x.shape
    xspec = pl.BlockSpec((1, rows, d), lambda b, i: (b, i, 0))
    return pl.pallas_call(
        _oproj_body,
        grid=(nb, s // rows),
        in_specs=[xspec, xspec, _mod_spec(gate, rows), _const_spec(w_o.shape)],
        out_specs=xspec,
        out_shape=jax.ShapeDtypeStruct((nb, s, d), F32),
        compiler_params=_cparams(("arbitrary", "arbitrary")),
        name="attn_out_proj",
    )(o, x, gate, w_o)


def _mix_weights_prompt(w_s, b_s, width):
    causal = jnp.tril(jnp.ones((CHUNK, CHUNK), dtype=bool))
    wmix = jnp.where(causal[None], w_s, 0.0).astype(BF16)
    bmix = jnp.repeat(b_s.T, width // GROUPS, axis=1)
    return wmix, bmix


def _mix_weights_sample(w_s, b_s, width, n_seq, n_tok):
    causal = jnp.tril(jnp.ones((n_tok, n_tok), dtype=bool))
    small = jnp.where(causal[None], w_s[:, :n_tok, :n_tok], 0.0)
    eye = jnp.eye(n_seq, dtype=F32)
    blockdiag = jnp.einsum('ab,gts->gatbs', eye, small).reshape(GROUPS, n_seq * n_tok, n_seq * n_tok)
    padn = SAMPLE_ROWS - n_seq * n_tok
    wmix = jnp.pad(blockdiag, ((0, 0), (0, padn), (0, padn))).astype(BF16)
    brow = jnp.tile(b_s.T[:n_tok], (n_seq, 1))
    bmix = jnp.repeat(jnp.pad(brow, ((0, padn), (0, 0))), width // GROUPS, axis=1)
    return wmix, bmix


def kernel(x_prompt, x_sample, cache_k, cache_v, page_table, c_prompt, c_sample, mod_w, mod_b, norm_mix_g, norm_ffn_g, a_w_in, a_b_in, a_v_norm_g, a_w_s, a_b_s, a_w_out, kv_mod_w, kv_mod_b, kv_norm_g, kv_w, k_norm_g, b_w_q, b_q_norm_g, b_logit_bias, b_w_o, peer_w_q, peer_subkeys, peer_u, peer_v):
    nbp, seq, d = x_prompt.shape
    nbs, n_tok, _ = x_sample.shape
    n_samp = nbs * n_tok
    width = a_w_out.shape[1]

    c_all = jnp.concatenate([c_prompt, c_sample], axis=0)
    c_all = jnp.pad(c_all, ((0, BF16_ROWS - c_all.shape[0]), (0, 0)))
    mods = _adaln(c_all, mod_w, mod_b)
    kvmod = _adaln(c_all, kv_mod_w[None], kv_mod_b[None])[0]

    def split_mod(m, n):
        parts = jnp.split(m, n, axis=-1)
        prm = [p[:nbp, None, :] for p in parts]
        smp = [jnp.pad(jnp.repeat(p[nbp:nbp + nbs], n_tok, axis=0), ((0, SAMPLE_ROWS - n_samp), (0, 0)))[None]
               for p in parts]
        return prm, smp

    mod_p, mod_s = zip(*[split_mod(mods[l], 6) for l in range(mods.shape[0])])
    kvmod_p, kvmod_s = split_mod(kvmod, 2)

    w_in = a_w_in[0].astype(BF16)
    w_out = a_w_out[0].astype(BF16)
    wq_t = [jnp.transpose(peer_w_q[l]).astype(BF16) for l in range(2)]
    subk = [peer_subkeys[l].astype(BF16) for l in range(2)]
    u_tab = peer_u.astype(BF16)
    eb = 256
    n_l, n_exp, _ = peer_v.shape
    vt_tab = jnp.transpose(peer_v.reshape(n_l, n_exp // eb, eb, d), (0, 1, 3, 2)).astype(BF16)
    w_kv = kv_w.astype(BF16)
    w_q1 = b_w_q[0].astype(BF16)
    w_o1 = b_w_o[0].astype(BF16)
    k_gain = jnp.tile(k_norm_g, N_HEADS).reshape(1, d)
    q_gain = jnp.tile(b_q_norm_g[0], N_HEADS).reshape(1, d)
    wmix_p, bmix_p = _mix_weights_prompt(a_w_s[0], a_b_s[0], width)
    wmix_s, bmix_s = _mix_weights_sample(a_w_s[0], a_b_s[0], width, nbs, n_tok)

    xs = jnp.pad(x_sample.reshape(1, n_samp, d), ((0, 0), (0, SAMPLE_ROWS - n_samp), (0, 0)))

    def trunk(x, mod, kvm, wmix, bmix, rows, tb, n_sub, attend):
        sh_m, sc_m, g_m, sh_f, sc_f, g_f = mod[0]
        x1, v_rows = _mixer_a(x, sh_m, sc_m, g_m, norm_mix_g[0], w_in, a_b_in[0], a_v_norm_g[0],
                              wmix, bmix, w_out, rows)
        x2 = _peer(x1, sh_f, sc_f, g_f, norm_ffn_g[0], wq_t[0], subk[0], u_tab, vt_tab, 0, tb, n_sub, eb)
        sh_m, sc_m, g_m, sh_f, sc_f, g_f = mod[1]
        k, v, kb, vb, q = _kvq(x2, kvm[0], kvm[1], sh_m, sc_m, kv_norm_g, norm_mix_g[1], w_kv, w_q1,
                               k_gain, q_gain, rows)
        o = attend(q, k, v, kb, vb)
        x3 = _oproj(o, x2, g_m, w_o1, rows)
        y = _peer(x3, sh_f, sc_f, g_f, norm_ffn_g[1], wq_t[1], subk[1], u_tab, vt_tab, 1, tb, n_sub, eb)
        return y, k, v, v_rows

    def attend_prompt(q, k, v, kb, vb):
        return _sb_prompt(q, kb, vb, b_logit_bias[0], 256, 8)

    def attend_sample(q, k, v, kb, vb):
        qs = q[0, :n_samp].reshape(nbs, n_tok, d)
        ks = k[0, :n_samp].reshape(nbs, n_tok, d)
        vs = v[0, :n_samp].reshape(nbs, n_tok, d)
        full = _sb_paged(qs, ks, vs, cache_k, cache_v, page_table, b_logit_bias[0], 8)
        full = full[:, :N_HEADS * n_tok].reshape(nbs, N_HEADS, n_tok, N_HEADS, HEAD_DIM)
        idx = jnp.arange(N_HEADS)
        o = full[:, idx, :, idx, :]
        o = jnp.transpose(o, (1, 2, 0, 3)).reshape(1, n_samp, d)
        return jnp.pad(o, ((0, 0), (0, SAMPLE_ROWS - n_samp), (0, 0))).astype(BF16)

    y_p, k_p, v_p, _ = trunk(x_prompt, mod_p, kvmod_p, wmix_p, bmix_p, 256, 512, 2, attend_prompt)
    y_s, k_s, v_s, gv_s = trunk(xs, mod_s, kvmod_s, wmix_s, bmix_s, SAMPLE_ROWS, SAMPLE_ROWS, 1, attend_sample)

    def samp(a, shape):
        return a[0, :n_samp].reshape(shape)

    return (y_p,
            samp(y_s, (nbs, n_tok, d)),
            k_p.reshape(nbp, seq, N_HEADS, HEAD_DIM),
            v_p.reshape(nbp, seq, N_HEADS, HEAD_DIM),
            samp(k_s, (nbs, n_tok, N_HEADS, HEAD_DIM)),
            samp(v_s, (nbs, n_tok, N_HEADS, HEAD_DIM)),
            samp(gv_s, (1, nbs, n_tok, width)))
```
